```python
import math
import jax, jax.numpy as jnp
from jax import lax
import numpy as np

D_MODEL = 1024
BATCH = 8
SEQ = 2048
DEPTH = 4

GRID_W = 64
CTX_LEN = 256
N_MIXERS = 4
EPS = 1e-6
ROPE_THETA = 10000.0
Q_BLOCK = 128

FNET_GROUPS = 8
FNET_GROUP_DIM = D_MODEL // FNET_GROUPS
DIFF_HEAD_DIM = 64
DIFF_HEADS = D_MODEL // (2 * DIFF_HEAD_DIM)
HGRN_EXPAND = 128
HGRN_HEADS = D_MODEL // HGRN_EXPAND
HGRN_HEAD_V = D_MODEL // HGRN_HEADS
HGRN_FORGET_DIM = HGRN_HEADS * HGRN_EXPAND
HGRN_CHUNK = 64
GQA_HEAD_DIM = 128
GQA_Q_HEADS = D_MODEL // GQA_HEAD_DIM
GQA_KV_HEADS = 2
GQA_GROUP = GQA_Q_HEADS // GQA_KV_HEADS
D_FF = -(-8 * D_MODEL // (3 * 256)) * 256
N_FNET_LAYERS = len(range(0, DEPTH, N_MIXERS))
N_DIFF_LAYERS = len(range(1, DEPTH, N_MIXERS))
N_HGRN_LAYERS = len(range(2, DEPTH, N_MIXERS))
N_GQA_LAYERS = len(range(3, DEPTH, N_MIXERS))

kernel_name = 'hybrid_interleaved_dit_trunk'


def rms_norm(x, gain):
    xf = x.astype(jnp.float32)
    y = xf * lax.rsqrt(jnp.mean(xf * xf, axis=-1, keepdims=True) + EPS)
    return (y * gain.astype(jnp.float32)).astype(x.dtype)


def modulate(h, shift, scale):
    return h * (1.0 + scale) + shift


def swiglu(h, w_in, w_out):
    gate, up = jnp.split(h @ w_in, 2, axis=-1)
    return (jax.nn.silu(gate) * up) @ w_out


def axial_rope_tables(rows, head_dim):
    row = jnp.repeat(jnp.arange(rows, dtype=jnp.float32), GRID_W)
    col = jnp.tile(jnp.arange(GRID_W, dtype=jnp.float32), rows)
    n_freq = head_dim // 4
    inv_freq = ROPE_THETA ** (-jnp.arange(n_freq, dtype=jnp.float32) / n_freq)
    ang = jnp.concatenate([row[:, None] * inv_freq, col[:, None] * inv_freq], axis=-1)
    return jnp.cos(ang), jnp.sin(ang)


def apply_rope(x, cos, sin):
    shape = (cos.shape[0],) + (1,) * (x.ndim - 3) + (cos.shape[1],)
    cos = cos.reshape(shape)
    sin = sin.reshape(shape)
    x1, x2 = jnp.split(x.astype(jnp.float32), 2, axis=-1)
    return jnp.concatenate([x1 * cos - x2 * sin, x1 * sin + x2 * cos], axis=-1).astype(x.dtype)


def sweep_query_blocks(q, attend):
    b, l = q.shape[:2]
    nb = l // Q_BLOCK
    qb = jnp.moveaxis(q.reshape((b, nb, Q_BLOCK) + q.shape[2:]), 1, 0)
    out = lax.map(attend, qb)
    return jnp.moveaxis(out, 0, 1).reshape((b, l) + out.shape[3:])


def fnet_mixer(h_lat, h_ctx, w_out, b_out, need_ctx):
    def mix(h):
        b, l, _ = h.shape
        hg = h.astype(jnp.float32).reshape(b, l, FNET_GROUPS, FNET_GROUP_DIM)
        y = jnp.fft.fftn(hg, axes=(1, 3), norm='ortho').real
        return y.reshape(b, l, D_MODEL).astype(h.dtype) @ w_out + b_out
    return mix(h_lat), (mix(h_ctx) if need_ctx else None)


def diff_attn_core(q, k, v, lam):
    s = jnp.einsum('bqhcd,bkhcd->bhcqk', q.astype(jnp.float32), k.astype(jnp.float32)) * DIFF_HEAD_DIM ** -0.5
    p = jax.nn.softmax(s, axis=-1)
    w = p[:, :, 0] - lam * p[:, :, 1]
    return jnp.einsum('bhqk,bkhe->bqhe', w, v.astype(jnp.float32)).astype(v.dtype)


def diff_attention_mixer(h_lat, h_ctx, w_in, q_gain, k_gain, lam_par, subln_gain, w_out, layer_idx, rows, need_ctx):
    lam_init = 0.8 - 0.6 * math.exp(-0.3 * layer_idx)
    lp = lam_par.astype(jnp.float32)
    lam = jnp.exp(jnp.sum(lp[0] * lp[1])) - jnp.exp(jnp.sum(lp[2] * lp[3])) + lam_init
    cos, sin = axial_rope_tables(rows, DIFF_HEAD_DIM)

    def project(h, with_q):
        b, l, _ = h.shape
        if with_q:
            q, k, v = jnp.split(h @ w_in, 3, axis=-1)
            q = rms_norm(q.reshape(b, l, DIFF_HEADS, 2, DIFF_HEAD_DIM), q_gain)
        else:
            k, v = jnp.split(h @ w_in[:, D_MODEL:], 2, axis=-1)
            q = None
        k = rms_norm(k.reshape(b, l, DIFF_HEADS, 2, DIFF_HEAD_DIM), k_gain)
        return q, k, v.reshape(b, l, DIFF_HEADS, 2 * DIFF_HEAD_DIM)

    def finish(o):
        b, l = o.shape[:2]
        o = rms_norm(o, subln_gain) * (1.0 - lam_init)
        return o.reshape(b, l, D_MODEL) @ w_out

    q_c, k_c, v_c = project(h_ctx, need_ctx)
    q_l, k_l, v_l = project(h_lat, True)
    q_l = apply_rope(q_l, cos, sin)
    k_l = apply_rope(k_l, cos, sin)
    k_all = jnp.concatenate([k_c, k_l], axis=1)
    v_all = jnp.concatenate([v_c, v_l], axis=1)
    o_lat = finish(sweep_query_blocks(q_l, lambda qb: diff_attn_core(qb, k_all, v_all, lam)))
    o_ctx = finish(diff_attn_core(q_c, k_c, v_c, lam)) if need_ctx else None
    return o_lat, o_ctx


def gla_chunk_scan(q, k, v, log_f, s0):
    b, l, h, dk = q.shape
    dv = v.shape[-1]
    n = l // HGRN_CHUNK

    def chunks(t):
        return jnp.moveaxis(t.reshape((b, n, HGRN_CHUNK) + t.shape[2:]), 1, 0)

    lower = jnp.tril(jnp.ones((HGRN_CHUNK, HGRN_CHUNK), dtype=bool))[None, :, :, None, None]

    def step(s, inp):
        qc, kc, vc, gc = inp
        bc = jnp.cumsum(gc, axis=1)
        decay = jnp.exp(jnp.where(lower, bc[:, :, None] - bc[:, None, :], -jnp.inf))
        scores = jnp.einsum('bthk,bshk,btshk->bhts', qc, kc, decay)
        o = jnp.einsum('bhts,bshv->bthv', scores, vc) + jnp.einsum('bthk,bhkv->bthv', qc * jnp.exp(bc), s)
        b_last = bc[:, -1]
        s = jnp.exp(b_last)[..., None] * s + jnp.einsum('bshk,bshv->bhkv', kc * jnp.exp(b_last[:, None] - bc), vc)
        return s, o

    s_fin, o = lax.scan(step, s0, (chunks(q), chunks(k), chunks(v), chunks(log_f)))
    return jnp.moveaxis(o, 0, 1).reshape(b, l, h, dv), s_fin


def hgrn2_mixer(h_lat, h_ctx, w_in, lower_bound, norm_gain, w_out, layer_idx, need_ctx):
    lbs = jnp.cumsum(jax.nn.softmax(lower_bound.astype(jnp.float32), axis=1), axis=1)
    lb = (lbs[:, layer_idx] - lbs[:, 0]).reshape(2, HGRN_HEADS, HGRN_EXPAND)

    def project(h):
        b, l, _ = h.shape
        q, z_fwd, z_bwd, v, g = jnp.split(h @ w_in, 5, axis=-1)
        q = jax.nn.silu(q).reshape(b, l, HGRN_HEADS, HGRN_EXPAND).astype(jnp.float32)
        v = v.reshape(b, l, HGRN_HEADS, HGRN_HEAD_V).astype(jnp.float32)
        log_f = [jnp.log(lb[d] + (1.0 - lb[d]) * jax.nn.sigmoid(z.reshape(b, l, HGRN_HEADS, HGRN_EXPAND).astype(jnp.float32)))
                 for d, z in enumerate((z_fwd, z_bwd))]
        return q, v, log_f, g

    def run(q, v, lf, s0, reverse):
        if reverse:
            q, v, lf = jnp.flip(q, 1), jnp.flip(v, 1), jnp.flip(lf, 1)
        o, s = gla_chunk_scan(q, -jnp.expm1(lf), v, lf, s0)
        return (jnp.flip(o, 1) if reverse else o), s

    def finish(o, g):
        b, l = o.shape[:2]
        o = rms_norm(o, norm_gain) * jax.nn.silu(g.reshape(b, l, HGRN_HEADS, HGRN_HEAD_V).astype(jnp.float32))
        return o.reshape(b, l, D_MODEL).astype(g.dtype) @ w_out

    q_c, v_c, lf_c, g_c = project(h_ctx)
    q_l, v_l, lf_l, g_l = project(h_lat)
    s0 = jnp.zeros((h_ctx.shape[0], HGRN_HEADS, HGRN_EXPAND, HGRN_HEAD_V), jnp.float32)
    o_ctx_f, s_ctx_f = run(q_c, v_c, lf_c[0], s0, False)
    o_ctx_b, s_ctx_b = run(q_c, v_c, lf_c[1], s0, True)
    o_lat_f, _ = run(q_l, v_l, lf_l[0], s_ctx_f, False)
    o_lat_b, _ = run(q_l, v_l, lf_l[1], s_ctx_b, True)
    o_lat = finish(o_lat_f + o_lat_b, g_l)
    o_ctx = finish(o_ctx_f + o_ctx_b, g_c) if need_ctx else None
    return o_lat, o_ctx


def gqa_core(q, k, v):
    s = jnp.einsum('bqhgd,bkhd->bhgqk', q.astype(jnp.float32), k.astype(jnp.float32)) * GQA_HEAD_DIM ** -0.5
    p = jax.nn.softmax(s, axis=-1)
    return jnp.einsum('bhgqk,bkhd->bqhgd', p, v.astype(jnp.float32)).astype(v.dtype)


def gqa_mixer(h_lat, h_ctx, w_in, q_gain, k_gain, w_out, rows, need_ctx):
    qd = GQA_Q_HEADS * GQA_HEAD_DIM
    cos, sin = axial_rope_tables(rows, GQA_HEAD_DIM)

    def project(h, with_q):
        b, l, _ = h.shape
        if with_q:
            q, kv = jnp.split(h @ w_in, [qd], axis=-1)
            q = rms_norm(q.reshape(b, l, GQA_KV_HEADS, GQA_GROUP, GQA_HEAD_DIM), q_gain)
        else:
            kv = h @ w_in[:, qd:]
            q = None
        k, v = jnp.split(kv, 2, axis=-1)
        k = rms_norm(k.reshape(b, l, GQA_KV_HEADS, GQA_HEAD_DIM), k_gain)
        return q, k, v.reshape(b, l, GQA_KV_HEADS, GQA_HEAD_DIM)

    def finish(o):
        b, l = o.shape[:2]
        return o.reshape(b, l, D_MODEL) @ w_out

    q_c, k_c, v_c = project(h_ctx, need_ctx)
    q_l, k_l, v_l = project(h_lat, True)
    q_l = apply_rope(q_l, cos, sin)
    k_l = apply_rope(k_l, cos, sin)
    k_all = jnp.concatenate([k_c, k_l], axis=1)
    v_all = jnp.concatenate([v_c, v_l], axis=1)
    o_lat = finish(sweep_query_blocks(q_l, lambda qb: gqa_core(qb, k_all, v_all)))
    o_ctx = finish(gqa_core(q_c, k_c, v_c)) if need_ctx else None
    return o_lat, o_ctx


def setup_inputs(seed: int = 0) -> dict:
    key = jax.random.key(seed)
    ks = iter(jax.random.split(key, 32))
    D = D_MODEL

    def nrm(shape, scale):
        return jax.random.normal(next(ks), shape, jnp.float32) * scale

    def gain(shape):
        return 1.0 + nrm(shape, 0.05)

    return {
        'x': nrm((BATCH, SEQ, D), 1.0),
        'c': nrm((BATCH, D), 1.0),
        'ctx': nrm((BATCH, CTX_LEN, D), 1.0),
        'c_ctx': nrm((D,), 1.0),
        'w_ada': nrm((DEPTH, D, 6 * D), 0.5 * D ** -0.5),
        'b_ada': nrm((DEPTH, 6 * D), 0.02),
        'norm_gain': gain((DEPTH, 2, D)),
        'ffn_w_in': nrm((DEPTH, D, 2 * D_FF), D ** -0.5),
        'ffn_w_out': nrm((DEPTH, D_FF, D), D_FF ** -0.5),
        'fnet_w_out': nrm((N_FNET_LAYERS, D, D), D ** -0.5),
        'fnet_b_out': nrm((N_FNET_LAYERS, D), 0.02),
        'diff_w_in': nrm((N_DIFF_LAYERS, D, 3 * D), D ** -0.5),
        'diff_q_gain': gain((N_DIFF_LAYERS, 2, DIFF_HEAD_DIM)),
        'diff_k_gain': gain((N_DIFF_LAYERS, 2, DIFF_HEAD_DIM)),
        'diff_lambda': nrm((N_DIFF_LAYERS, 4, DIFF_HEAD_DIM), 0.1),
        'diff_subln_gain': gain((N_DIFF_LAYERS, 2 * DIFF_HEAD_DIM)),
        'diff_w_out': nrm((N_DIFF_LAYERS, D, D), D ** -0.5),
        'hgrn_w_in': nrm((N_HGRN_LAYERS, D, 5 * D), D ** -0.5),
        'hgrn_lower_bound': nrm((2, DEPTH, HGRN_FORGET_DIM), 1.0),
        'hgrn_norm_gain': gain((N_HGRN_LAYERS, HGRN_HEAD_V)),
        'hgrn_w_out': nrm((N_HGRN_LAYERS, D, D), D ** -0.5),
        'gqa_w_in': nrm((N_GQA_LAYERS, D, (GQA_Q_HEADS + 2 * GQA_KV_HEADS) * GQA_HEAD_DIM), D ** -0.5),
        'gqa_q_gain': gain((N_GQA_LAYERS, GQA_HEAD_DIM)),
        'gqa_k_gain': gain((N_GQA_LAYERS, GQA_HEAD_DIM)),
        'gqa_w_out': nrm((N_GQA_LAYERS, D, D), D ** -0.5),
    }


def reference(x, c, ctx, c_ctx, w_ada, b_ada, norm_gain, ffn_w_in, ffn_w_out, fnet_w_out, fnet_b_out,
              diff_w_in, diff_q_gain, diff_k_gain, diff_lambda, diff_subln_gain, diff_w_out,
              hgrn_w_in, hgrn_lower_bound, hgrn_norm_gain, hgrn_w_out,
              gqa_w_in, gqa_q_gain, gqa_k_gain, gqa_w_out):
    rows = x.shape[1] // GRID_W
    cond_lat = jax.nn.silu(c)[:, None, :]
    cond_ctx = jax.nn.silu(c_ctx)[None, None, :]
    for i in range(DEPTH):
        m, j = i % N_MIXERS, i // N_MIXERS
        need_ctx = i < DEPTH - 1
        sh1, sc1, g1, sh2, sc2, g2 = jnp.split(cond_lat @ w_ada[i] + b_ada[i], 6, axis=-1)
        csh1, csc1, cg1, csh2, csc2, cg2 = jnp.split(cond_ctx @ w_ada[i] + b_ada[i], 6, axis=-1)
        h_lat = modulate(rms_norm(x, norm_gain[i, 0]), sh1, sc1)
        h_ctx = modulate(rms_norm(ctx, norm_gain[i, 0]), csh1, csc1) if (need_ctx or m != 0) else None
        if m == 0:
            o_lat, o_ctx = fnet_mixer(h_lat, h_ctx, fnet_w_out[j], fnet_b_out[j], need_ctx)
        elif m == 1:
            o_lat, o_ctx = diff_attention_mixer(h_lat, h_ctx, diff_w_in[j], diff_q_gain[j], diff_k_gain[j], diff_lambda[j],
                                                diff_subln_gain[j], diff_w_out[j], i, rows, need_ctx)
        elif m == 2:
            o_lat, o_ctx = hgrn2_mixer(h_lat, h_ctx, hgrn_w_in[j], hgrn_lower_bound, hgrn_norm_gain[j], hgrn_w_out[j], i, need_ctx)
        else:
            o_lat, o_ctx = gqa_mixer(h_lat, h_ctx, gqa_w_in[j], gqa_q_gain[j], gqa_k_gain[j], gqa_w_out[j], rows, need_ctx)
        x = x + g1 * o_lat
        x = x + g2 * swiglu(modulate(rms_norm(x, norm_gain[i, 1]), sh2, sc2), ffn_w_in[i], ffn_w_out[i])
        if need_ctx:
            ctx = ctx + cg1 * o_ctx
            ctx = ctx + cg2 * swiglu(modulate(rms_norm(ctx, norm_gain[i, 1]), csh2, csc2), ffn_w_in[i], ffn_w_out[i])
    return x
```

```python
import functools
import math

import numpy as np
import jax
import jax.numpy as jnp
from jax import lax
from jax.experimental import pallas as pl
from jax.experimental.pallas import tpu as pltpu

F32 = jnp.float32
BF16 = jnp.bfloat16

EPS = 1e-6
GRID_W = 64
ROPE_THETA = 10000.0
N_MIXERS = 4

LANES = 128
TM = 256
COND_ROWS = 16
VMEM_LIMIT = 56 * 1024 * 1024

DIFF_HEAD_DIM = 64
GQA_HEAD_DIM = 128
GQA_GROUP = 4
HGRN_CHUNK = 64
HGRN_HALF = 32
HGRN_SAFE_LOG_DECAY = -80.0
FF_CHUNK = 256


def _cparams(*sem):
    return pltpu.CompilerParams(dimension_semantics=sem, vmem_limit_bytes=VMEM_LIMIT)


def _resident(shape):
    nd = len(shape)
    return pl.BlockSpec(shape, lambda *_: (0,) * nd, pipeline_mode=pl.Buffered(1))


def _silu(x):
    return x * jax.nn.sigmoid(x)


def _norm_mod(x, gain, shift, scale):
    ms = jnp.mean(x * x, axis=-1, keepdims=True)
    y = x * lax.rsqrt(ms + EPS) * gain
    return y * (1.0 + scale) + shift


def _dot(a, b):
    return jnp.dot(a, b, preferred_element_type=F32)


def _dot_nt(a, b):
    return lax.dot_general(a, b, (((1,), (1,)), ((), ())), preferred_element_type=F32)


def _dot_tn(a, b):
    return lax.dot_general(a, b, (((0,), (0,)), ((), ())), preferred_element_type=F32)


def _ada_kernel(cond_ref, w_ref, b_ref, o_ref):
    a = _silu(cond_ref[...]).astype(BF16)
    o_ref[...] = _dot(a, w_ref[...].astype(BF16)) + b_ref[...]


def _ada_modulation(cond, w_ada, b_ada):
    depth, d, n6 = w_ada.shape
    bn = n6 // 4
    out = pl.pallas_call(
        _ada_kernel,
        grid=(depth, n6 // bn),
        in_specs=[
            pl.BlockSpec((COND_ROWS, d), lambda i, j: (0, 0)),
            pl.BlockSpec((None, d, bn), lambda i, j: (i, 0, j)),
            pl.BlockSpec((None, 1, bn), lambda i, j: (i, 0, j)),
        ],
        out_specs=pl.BlockSpec((None, COND_ROWS, bn), lambda i, j: (i, 0, j)),
        out_shape=jax.ShapeDtypeStruct((depth, COND_ROWS, n6), F32),
        compiler_params=_cparams("parallel", "parallel"),
        name="ada_modulation",
    )(cond, w_ada, b_ada.reshape(depth, 1, n6))
    return out.reshape(depth * COND_ROWS * 6, 1, d)


class _Tokens:
    def __init__(self, batch, n_ctx, seq):
        assert n_ctx % TM == 0 and seq % TM == 0
        self.batch, self.n_ctx, self.seq = batch, n_ctx, seq
        self.t = n_ctx + seq
        self.ctx_tiles = n_ctx // TM
        self.tiles = self.t // TM

    def mod_spec(self, d, layer, chunk, tile_of):
        def index(i):
            b, r = tile_of(i)
            row = jnp.where(r < self.ctx_tiles, 0, 1 + b)
            return ((layer * COND_ROWS + row) * 6 + chunk, 0, 0)
        return pl.BlockSpec((None, 1, d), index)


def _vec_spec(n):
    return pl.BlockSpec((1, n), lambda *_: (0, 0))


def _dft_tables(n_ctx, seq, group_dim):
    def cs(n):
        k = np.arange(n, dtype=np.int64)
        ang = 2.0 * np.pi * ((k[:, None] * k[None, :]) % n).astype(np.float64) / n
        return np.cos(ang), np.sin(ang)
    cc, sc = cs(group_dim)
    chan = np.concatenate([cc, sc], axis=1)
    cl, sl = cs(seq)
    pos_lat = np.concatenate([cl, -sl], axis=1)
    cx, sx = cs(n_ctx)
    pos_ctx = np.concatenate([cx, -sx], axis=1)
    as_bf16 = lambda a: jnp.asarray(a.astype(np.float32)).astype(BF16)
    return as_bf16(chan), as_bf16(pos_ctx), as_bf16(pos_lat)


def _fnet_kernel(x_ref, ng_ref, shc_ref, scc_ref, shl_ref, scl_ref, chan_ref, pctx_ref, plat_ref,
                 y_ref, ab_ctx, ab_lat, *, n_ctx, seq, group_dim):
    j = pl.program_id(1)
    ctx_tiles = n_ctx // TM
    d = x_ref.shape[-1]

    @pl.when(j == 0)
    def _():
        for r in range((n_ctx + seq) // TM):
            is_ctx = r < ctx_tiles
            sh, sc = (shc_ref, scc_ref) if is_ctx else (shl_ref, scl_ref)
            h = _norm_mod(x_ref[r * TM:(r + 1) * TM, :], ng_ref[...], sh[...], sc[...]).astype(BF16)
            dst, n, row0 = (ab_ctx, n_ctx, r * TM) if is_ctx else (ab_lat, seq, r * TM - n_ctx)
            for g in range(d // group_dim):
                cols = slice(g * group_dim, (g + 1) * group_dim)
                ab = _dot(h[:, cols], chan_ref[...])
                dst[row0:row0 + TM, cols] = ab[:, :group_dim].astype(BF16)
                dst[n + row0:n + row0 + TM, cols] = ab[:, group_dim:].astype(BF16)

    @pl.when(j < ctx_tiles)
    def _():
        rows = pl.ds(pl.multiple_of(j * TM, TM), TM)
        y = _dot(pctx_ref[rows, :], ab_ctx[...]) * (1.0 / math.sqrt(n_ctx * group_dim))
        y_ref[...] = y.astype(BF16)

    @pl.when(j >= ctx_tiles)
    def _():
        y = _dot(plat_ref[...], ab_lat[...]) * (1.0 / math.sqrt(seq * group_dim))
        y_ref[...] = y.astype(BF16)


def _fnet_mix(tok, x_all, mod, layer, norm_gain, group_dim):
    b, t, d = tok.batch, tok.t, x_all.shape[-1]
    chan, pos_ctx, pos_lat = _dft_tables(tok.n_ctx, tok.seq, group_dim)
    ct = tok.ctx_tiles

    def mod_spec(chunk, ctx):
        return pl.BlockSpec((None, 1, d), lambda i, j: ((layer * COND_ROWS + (0 if ctx else 1 + i)) * 6 + chunk, 0, 0))

    kern = functools.partial(_fnet_kernel, n_ctx=tok.n_ctx, seq=tok.seq, group_dim=group_dim)
    return pl.pallas_call(
        kern,
        grid=(b, tok.tiles),
        in_specs=[
            pl.BlockSpec((None, t, d), lambda i, j: (i, 0, 0)),
            pl.BlockSpec((1, d), lambda i, j: (0, 0)),
            mod_spec(0, True), mod_spec(1, True), mod_spec(0, False), mod_spec(1, False),
            _resident(chan.shape),
            _resident(pos_ctx.shape),
            pl.BlockSpec((TM, 2 * tok.seq), lambda i, j: (jnp.maximum(j - ct, 0), 0)),
        ],
        out_specs=pl.BlockSpec((TM, d), lambda i, j: (i * tok.tiles + j, 0)),
        out_shape=jax.ShapeDtypeStruct((b * t, d), BF16),
        scratch_shapes=[pltpu.VMEM((2 * tok.n_ctx, d), BF16), pltpu.VMEM((2 * tok.seq, d), BF16)],
        compiler_params=_cparams("parallel", "arbitrary"),
        name="fnet_mix",
    )(x_all, norm_gain.reshape(1, d), mod, mod, mod, mod, chan, pos_ctx, pos_lat)


def _rope_tables(tok, head_dim):
    rows = tok.seq // GRID_W
    row = jnp.repeat(jnp.arange(rows, dtype=F32), GRID_W)
    col = jnp.tile(jnp.arange(GRID_W, dtype=F32), rows)
    n_freq = head_dim // 4
    inv_freq = ROPE_THETA ** (-jnp.arange(n_freq, dtype=F32) / n_freq)
    ang = jnp.concatenate([row[:, None] * inv_freq, col[:, None] * inv_freq], axis=-1)
    cos = jnp.concatenate([jnp.ones((tok.n_ctx, head_dim // 2), F32), jnp.cos(ang)], axis=0)
    sin = jnp.concatenate([jnp.zeros((tok.n_ctx, head_dim // 2), F32), jnp.sin(ang)], axis=0)
    reps = LANES // head_dim
    cos = jnp.tile(jnp.concatenate([cos, cos], axis=-1), (1, reps))
    sin = jnp.tile(jnp.concatenate([-sin, sin], axis=-1), (1, reps))
    return cos, sin


def _group_mean_matrix(group):
    g = np.arange(LANES) // group
    return jnp.asarray((g[:, None] == g[None, :]).astype(np.float32) / group).astype(BF16)


def _head_norm_rope(y, gain, gmean, cos, sin, head_dim):
    ms = _dot((y * y).astype(BF16), gmean)
    yn = y * lax.rsqrt(ms + EPS) * gain
    half = head_dim // 2
    if head_dim == LANES:
        partner = pltpu.roll(yn, half, 1)
    else:
        lane = lax.broadcasted_iota(jnp.int32, yn.shape, 1)
        partner = jnp.where(lane % head_dim < half, pltpu.roll(yn, LANES - half, 1), pltpu.roll(yn, half, 1))
    return yn * cos + partner * sin


def _qkv_kernel(x_ref, ng_ref, sh_ref, sc_ref, w_ref, qg_ref, kg_ref, cos_ref, sin_ref, gm_ref,
                q_ref, k_ref, v_ref, *, head_dim):
    h = _norm_mod(x_ref[...], ng_ref[...], sh_ref[...], sc_ref[...]).astype(BF16)
    cos, sin, gm = cos_ref[...], sin_ref[...], gm_ref[...]
    nq, nk = q_ref.shape[-1], k_ref.shape[-1]
    for o_ref, g_ref, col0, n in ((q_ref, qg_ref, 0, nq), (k_ref, kg_ref, nq, nk)):
        y = _dot(h, w_ref[:, col0:col0 + n])
        for blk in range(n // LANES):
            cols = slice(blk * LANES, (blk + 1) * LANES)
            o_ref[:, cols] = _head_norm_rope(y[:, cols], g_ref[...], gm, cos, sin, head_dim).astype(BF16)
    v_ref[...] = _dot(h, w_ref[:, nq + nk:]).astype(BF16)


def _qkv_project(tok, x_all, mod, layer, norm_gain, w_in, q_gain, k_gain, head_dim, nq, nk):
    n_rows, d = x_all.shape
    nv = w_in.shape[1] - nq - nk
    cos, sin = _rope_tables(tok, head_dim)
    mod_spec = lambda chunk: tok.mod_spec(d, layer, chunk, lambda i: (i // tok.tiles, i % tok.tiles))
    row_spec = lambda n: pl.BlockSpec((TM, n), lambda i: (i, 0))
    tab_spec = pl.BlockSpec((TM, LANES), lambda i: (i % tok.tiles, 0))
    return pl.pallas_call(
        functools.partial(_qkv_kernel, head_dim=head_dim),
        grid=(n_rows // TM,),
        in_specs=[row_spec(d), _vec_spec(d), mod_spec(0), mod_spec(1), _resident(w_in.shape),
                  _vec_spec(LANES), _vec_spec(LANES), tab_spec, tab_spec, _resident((LANES, LANES))],
        out_specs=[row_spec(nq), row_spec(nk), row_spec(nv)],
        out_shape=[jax.ShapeDtypeStruct((n_rows, n), BF16) for n in (nq, nk, nv)],
        compiler_params=_cparams("parallel"),
        name=f"qkv_project_hd{head_dim}",
    )(x_all, norm_gain.reshape(1, d), mod, mod, w_in.astype(BF16), q_gain, k_gain, cos, sin,
      _group_mean_matrix(head_dim))


def _softmax_parts(q, k):
    s = _dot_nt(q, k)
    p = jnp.exp(s - jnp.max(s, axis=-1, keepdims=True))
    return p, jnp.sum(p, axis=-1, keepdims=True)


def _diff_attn_kernel(lp_ref, sg_ref, q_ref, k_ref, v_ref, o_ref, *, lam_init, n_ctx):
    lp = lp_ref[...]
    lam = (jnp.exp(jnp.sum(lp[0:1] * lp[1:2], axis=-1, keepdims=True))
           - jnp.exp(jnp.sum(lp[2:3] * lp[3:4], axis=-1, keepdims=True)) + lam_init)
    q = q_ref[...]
    lane = lax.broadcasted_iota(jnp.int32, q.shape, 1)
    zero = jnp.zeros_like(q)
    q1 = jnp.where(lane < DIFF_HEAD_DIM, q, zero)
    q2 = jnp.where(lane >= DIFF_HEAD_DIM, q, zero)

    def attend(nk):
        k, v = k_ref[0:nk, :], v_ref[0:nk, :]
        p1, l1 = _softmax_parts(q1, k)
        p2, l2 = _softmax_parts(q2, k)
        w = p1 * (1.0 / l1) - p2 * (lam / l2)
        o = _dot(w.astype(BF16), v)
        ms = jnp.mean(o * o, axis=-1, keepdims=True)
        o_ref[...] = (o * lax.rsqrt(ms + EPS) * sg_ref[...]).astype(BF16)

    is_ctx = pl.program_id(2) < n_ctx // TM
    pl.when(is_ctx)(lambda: attend(n_ctx))
    pl.when(jnp.logical_not(is_ctx))(lambda: attend(k_ref.shape[0]))


def _diff_attention(tok, q, k, v, lam_par, subln_gain, lam_init):
    n_rows, d = q.shape
    heads = d // LANES
    t = tok.t
    k3, v3 = k.reshape(tok.batch, t, d), v.reshape(tok.batch, t, d)
    kv_spec = pl.BlockSpec((None, t, LANES), lambda b, h, j: (b, 0, h))
    qo_spec = pl.BlockSpec((TM, LANES), lambda b, h, j: (b * tok.tiles + j, h))
    return pl.pallas_call(
        functools.partial(_diff_attn_kernel, lam_init=lam_init, n_ctx=tok.n_ctx),
        grid=(tok.batch, heads, tok.tiles),
        in_specs=[pl.BlockSpec(lam_par.shape, lambda b, h, j: (0, 0)),
                  pl.BlockSpec((1, LANES), lambda b, h, j: (0, 0)), qo_spec, kv_spec, kv_spec],
        out_specs=qo_spec,
        out_shape=jax.ShapeDtypeStruct((n_rows, d), BF16),
        compiler_params=_cparams("parallel", "parallel", "arbitrary"),
        name="diff_attention",
    )(lam_par, (subln_gain * (1.0 - lam_init)).reshape(1, LANES), q, k3, v3)


def _gqa_attn_kernel(q_ref, k_ref, v_ref, o_ref, *, n_ctx, first_tile):
    def attend(nk):
        k, v = k_ref[0:nk, :], v_ref[0:nk, :]
        for g in range(GQA_GROUP):
            cols = slice(g * LANES, (g + 1) * LANES)
            p, l = _softmax_parts(q_ref[:, cols], k)
            o_ref[:, cols] = (_dot(p.astype(BF16), v) * (1.0 / l)).astype(BF16)

    if first_tile >= n_ctx // TM:
        attend(k_ref.shape[0])
    else:
        is_ctx = pl.program_id(2) + first_tile < n_ctx // TM
        pl.when(is_ctx)(lambda: attend(n_ctx))
        pl.when(jnp.logical_not(is_ctx))(lambda: attend(k_ref.shape[0]))


def _gqa_attention(tok, q, k, v, need_ctx):
    d = q.shape[1]
    kv_heads = k.shape[1] // LANES
    t = tok.t
    first_tile = 0 if need_ctx else tok.ctx_tiles
    per_b = tok.tiles - first_tile
    k3, v3 = k.reshape(tok.batch, t, k.shape[1]), v.reshape(tok.batch, t, v.shape[1])
    kv_spec = pl.BlockSpec((None, t, LANES), lambda b, h, j: (b, 0, h))
    q_spec = pl.BlockSpec((TM, GQA_GROUP * LANES), lambda b, h, j: (b * tok.tiles + first_tile + j, h))
    o_spec = pl.BlockSpec((TM, GQA_GROUP * LANES), lambda b, h, j: (b * per_b + j, h))
    return pl.pallas_call(
        functools.partial(_gqa_attn_kernel, n_ctx=tok.n_ctx, first_tile=first_tile),
        grid=(tok.batch, kv_heads, per_b),
        in_specs=[q_spec, kv_spec, kv_spec],
        out_specs=o_spec,
        out_shape=jax.ShapeDtypeStruct((tok.batch * per_b * TM, d), BF16),
        compiler_params=_cparams("parallel", "parallel", "arbitrary"),
        name="gqa_attention",
    )(q, k3, v3)


def _hgrn_proj_kernel(x_ref, ng_ref, sh_ref, sc_ref, w_ref, lb_ref, q_ref, lff_ref, lfb_ref, v_ref, g_ref,
                      *, layer_idx, depth):
    h = _norm_mod(x_ref[...], ng_ref[...], sh_ref[...], sc_ref[...]).astype(BF16)
    d = x_ref.shape[-1]
    q_ref[...] = _silu(_dot(h, w_ref[:, 0:d]))
    for direction, o_ref in enumerate((lff_ref, lfb_ref)):
        rows = [lb_ref[direction * depth + i:direction * depth + i + 1, :] for i in range(depth)]
        m = functools.reduce(jnp.maximum, rows)
        e = [jnp.exp(r - m) for r in rows]
        lb = sum(e[1:layer_idx + 1]) / sum(e) if layer_idx > 0 else jnp.zeros_like(m)
        z = _dot(h, w_ref[:, (1 + direction) * d:(2 + direction) * d])
        o_ref[...] = jnp.log(lb + (1.0 - lb) * jax.nn.sigmoid(z))
    v_ref[...] = _dot(h, w_ref[:, 3 * d:4 * d]).astype(BF16)
    g_ref[...] = _dot(h, w_ref[:, 4 * d:5 * d])


def _hgrn_project(tok, x_all, mod, layer, norm_gain, w_in, lower_bound):
    n_rows, d = x_all.shape
    depth = lower_bound.shape[1]
    mod_spec = lambda chunk: tok.mod_spec(d, layer, chunk, lambda i: (i // tok.tiles, i % tok.tiles))
    row_spec = pl.BlockSpec((TM, d), lambda i: (i, 0))
    return pl.pallas_call(
        functools.partial(_hgrn_proj_kernel, layer_idx=layer, depth=depth),
        grid=(n_rows // TM,),
        in_specs=[row_spec, _vec_spec(d), mod_spec(0), mod_spec(1), _resident(w_in.shape),
                  pl.BlockSpec((2 * depth, d), lambda i: (0, 0))],
        out_specs=[row_spec] * 5,
        out_shape=[jax.ShapeDtypeStruct((n_rows, d), dt) for dt in (F32, F32, F32, BF16, F32)],
        compiler_params=_cparams("parallel"),
        name="hgrn_project",
    )(x_all, norm_gain.reshape(1, d), mod, mod, w_in.astype(BF16), lower_bound.reshape(2 * depth, d))


def _split3(x):
    hi = x.astype(BF16)
    r1 = x - hi.astype(F32)
    mid = r1.astype(BF16)
    lo = (r1 - mid.astype(F32)).astype(BF16)
    return hi, mid, lo


def _hgrn_scan_kernel(q_ref, lff_ref, lfb_ref, v_ref, g_ref, ng_ref, o_ref,
                      acc, cum, kk, a_scr, *, n_ctx):
    t, dk = q_ref.shape
    c, hh = HGRN_CHUNK, HGRN_HALF
    n_chunks, ctx_chunks = t // c, n_ctx // c
    row = lax.broadcasted_iota(jnp.int32, (c, c), 0)
    col = lax.broadcasted_iota(jnp.int32, (c, c), 1)
    same_half = (row // hh) == (col // hh)
    row_k = lax.broadcasted_iota(jnp.int32, (c, dk), 0)
    top = row_k < hh

    def seg_min(ref):
        return jnp.min(ref[...].reshape(t // hh, hh, dk).sum(axis=1))
    unsafe = jnp.minimum(seg_min(lff_ref), seg_min(lfb_ref)) < HGRN_SAFE_LOG_DECAY

    def run(lf_ref, reverse, first):
        tri = (col >= row) if reverse else (col <= row)
        tri_b = jnp.where(tri, 1.0, 0.0).astype(BF16)
        valid = tri & same_half
        second = jnp.logical_not(top) if not reverse else top
        edge = hh if reverse else hh - 1
        last = 0 if reverse else c - 1

        def chunk_step(i, state_t):
            if reverse:
                ci = jnp.where(i < ctx_chunks, ctx_chunks - 1 - i, n_chunks - 1 + ctx_chunks - i)
            else:
                ci = i
            rows = pl.ds(pl.multiple_of(ci * c, c), c)
            lf = lf_ref[rows, :]
            hi, mid, lo = _split3(lf)
            bc = _dot(tri_b, hi) + _dot(tri_b, mid) + _dot(tri_b, lo)
            q, v = q_ref[rows, :], v_ref[rows, :]
            k = 1.0 - jnp.exp(lf)
            ref_half = jnp.where(top, bc[hh // 2:hh // 2 + 1, :], bc[hh + hh // 2:hh + hh // 2 + 1, :])
            b_edge, b_last = bc[edge:edge + 1, :], bc[last:last + 1, :]
            q_s = (q * jnp.exp(bc)).astype(BF16)
            k_bar = (k * jnp.exp(b_last - bc)).astype(BF16)

            @pl.when(jnp.logical_not(unsafe))
            def _():
                q_d = q * jnp.exp(bc - ref_half)
                k_d = k * jnp.exp(ref_half - bc)
                q_o = jnp.where(second, q_d * jnp.exp(ref_half - b_edge), 0.0)
                k_o = jnp.where(second, 0.0, k_d * jnp.exp(b_edge - ref_half))
                a_d = _dot_nt(q_d.astype(BF16), k_d.astype(BF16))
                a_o = _dot_nt(q_o.astype(BF16), k_o.astype(BF16))
                a_scr[...] = jnp.where(valid, a_d, 0.0) + a_o

            @pl.when(unsafe)
            def _():
                cum[...] = bc
                kk[...] = k

                def col_step(s, a):
                    decay = jnp.exp(jnp.minimum(bc - cum[pl.ds(s, 1), :], 0.0))
                    w = jnp.sum(q * kk[pl.ds(s, 1), :] * decay, axis=-1, keepdims=True)
                    return jnp.where(col == s, w, a)
                a = lax.fori_loop(0, c, col_step, jnp.zeros((c, c), F32))
                a_scr[...] = jnp.where(tri, a, 0.0)

            o = _dot(a_scr[...].astype(BF16), v) + _dot_nt(q_s, state_t.astype(BF16))
            if first:
                acc[rows, :] = o
            else:
                acc[rows, :] = acc[rows, :] + o
            return state_t * jnp.exp(b_last) + _dot_tn(v, k_bar)

        lax.fori_loop(0, n_chunks, chunk_step, jnp.zeros((dk, dk), F32))

    run(lff_ref, False, True)
    run(lfb_ref, True, False)
    o = acc[...]
    ms = jnp.mean(o * o, axis=-1, keepdims=True)
    o_ref[...] = (o * lax.rsqrt(ms + EPS) * ng_ref[...] * _silu(g_ref[...])).astype(BF16)


def _hgrn_scan(tok, q, lff, lfb, v, g, norm_gain):
    n_rows, d = q.shape
    heads = d // LANES
    t = tok.t
    spec = pl.BlockSpec((None, t, LANES), lambda b, h: (b, 0, h))
    as3 = lambda a: a.reshape(tok.batch, t, d)
    out = pl.pallas_call(
        functools.partial(_hgrn_scan_kernel, n_ctx=tok.n_ctx),
        grid=(tok.batch, heads),
        in_specs=[spec] * 5 + [pl.BlockSpec((1, LANES), lambda b, h: (0, 0))],
        out_specs=spec,
        out_shape=jax.ShapeDtypeStruct((tok.batch, t, d), BF16),
        scratch_shapes=[pltpu.VMEM((t, LANES), F32), pltpu.VMEM((HGRN_CHUNK, LANES), F32),
                        pltpu.VMEM((HGRN_CHUNK, LANES), F32), pltpu.VMEM((HGRN_CHUNK, HGRN_CHUNK), F32)],
        compiler_params=_cparams("parallel", "parallel"),
        name="hgrn_scan",
    )(as3(q), as3(lff), as3(lfb), as3(v), as3(g), norm_gain.reshape(1, LANES))
    return out.reshape(n_rows, d)


def _ffn_kernel(x_ref, o_ref, wo_ref, bo_ref, g1_ref, ng_ref, sh_ref, sc_ref, g2_ref, wi_ref, wf_ref, y_ref):
    x1 = x_ref[...] + g1_ref[...] * (_dot(o_ref[...], wo_ref[...]) + bo_ref[...])
    h = _norm_mod(x1, ng_ref[...], sh_ref[...], sc_ref[...]).astype(BF16)
    d_ff = wf_ref.shape[0]
    acc = jnp.zeros(x1.shape, F32)
    for c0 in range(0, d_ff, FF_CHUNK):
        gate = _dot(h, wi_ref[:, c0:c0 + FF_CHUNK])
        up = _dot(h, wi_ref[:, d_ff + c0:d_ff + c0 + FF_CHUNK])
        acc = acc + _dot((_silu(gate) * up).astype(BF16), wf_ref[c0:c0 + FF_CHUNK, :])
    y_ref[...] = x1 + g2_ref[...] * acc


def _out_proj_ffn(tok, x_all, o_all, mod, layer, w_o, b_o, norm_gain, w_in, w_out, lat_only):
    n_rows, d = x_all.shape
    d_ff = w_out.shape[0]
    assert d_ff % FF_CHUNK == 0
    first = tok.ctx_tiles if lat_only else 0
    per_b = tok.tiles - first
    tile_of = lambda i: (i // per_b, first + i % per_b)
    mod_spec = lambda chunk: tok.mod_spec(d, layer, chunk, tile_of)
    stream_row = pl.BlockSpec((TM, d), lambda i: (tile_of(i)[0] * tok.tiles + tile_of(i)[1], 0))
    out_row = pl.BlockSpec((TM, d), lambda i: (i, 0))
    o_row = stream_row if o_all.shape[0] == n_rows else out_row
    return pl.pallas_call(
        _ffn_kernel,
        grid=(tok.batch * per_b,),
        in_specs=[stream_row, o_row, _resident(w_o.shape), _vec_spec(d), mod_spec(2), _vec_spec(d),
                  mod_spec(3), mod_spec(4), mod_spec(5), _resident(w_in.shape), _resident(w_out.shape)],
        out_specs=out_row,
        out_shape=jax.ShapeDtypeStruct((tok.batch * per_b * TM, d), F32),
        compiler_params=_cparams("parallel"),
        name="out_proj_ffn",
    )(x_all, o_all, w_o.astype(BF16), b_o.reshape(1, d), mod, norm_gain.reshape(1, d), mod, mod, mod,
      w_in.astype(BF16), w_out.astype(BF16))


def kernel(x, c, ctx, c_ctx, w_ada, b_ada, norm_gain, ffn_w_in, ffn_w_out, fnet_w_out, fnet_b_out,
           diff_w_in, diff_q_gain, diff_k_gain, diff_lambda, diff_subln_gain, diff_w_out,
           hgrn_w_in, hgrn_lower_bound, hgrn_norm_gain, hgrn_w_out,
           gqa_w_in, gqa_q_gain, gqa_k_gain, gqa_w_out):
    batch, seq, d = x.shape
    n_ctx = ctx.shape[1]
    depth = w_ada.shape[0]
    assert batch + 1 <= COND_ROWS and d % LANES == 0
    tok = _Tokens(batch, n_ctx, seq)

    cond = jnp.zeros((COND_ROWS, d), F32).at[0].set(c_ctx).at[1:1 + batch].set(c)
    mod = _ada_modulation(cond, w_ada, b_ada)
    x_all = jnp.concatenate([ctx, x], axis=1).reshape(batch * tok.t, d)
    zero_bias = jnp.zeros((d,), F32)

    for i in range(depth):
        m, j = i % N_MIXERS, i // N_MIXERS
        need_ctx = i < depth - 1
        if m == 0:
            group_dim = d // 8
            o = _fnet_mix(tok, x_all.reshape(batch, tok.t, d), mod, i, norm_gain[i, 0], group_dim)
            w_o, b_o = fnet_w_out[j], fnet_b_out[j]
        elif m == 1:
            lam_init = 0.8 - 0.6 * math.exp(-0.3 * i)
            qg = diff_q_gain[j].reshape(1, LANES) * DIFF_HEAD_DIM ** -0.5
            kg = diff_k_gain[j].reshape(1, LANES)
            q, k, v = _qkv_project(tok, x_all, mod, i, norm_gain[i, 0], diff_w_in[j], qg, kg,
                                   DIFF_HEAD_DIM, d, d)
            o = _diff_attention(tok, q, k, v, diff_lambda[j], diff_subln_gain[j], lam_init)
            w_o, b_o = diff_w_out[j], zero_bias
        elif m == 2:
            q, lff, lfb, v, g = _hgrn_project(tok, x_all, mod, i, norm_gain[i, 0], hgrn_w_in[j], hgrn_lower_bound)
            o = _hgrn_scan(tok, q, lff, lfb, v, g, hgrn_norm_gain[j])
            w_o, b_o = hgrn_w_out[j], zero_bias
        else:
            kv = (gqa_w_in.shape[-1] - d) // 2
            qg = gqa_q_gain[j].reshape(1, LANES) * GQA_HEAD_DIM ** -0.5
            kg = gqa_k_gain[j].reshape(1, LANES)
            q, k, v = _qkv_project(tok, x_all, mod, i, norm_gain[i, 0], gqa_w_in[j], qg, kg, GQA_HEAD_DIM, d, kv)
            o = _gqa_attention(tok, q, k, v, need_ctx)
            w_o, b_o = gqa_w_out[j], zero_bias
        x_all = _out_proj_ffn(tok, x_all, o, mod, i, w_o, b_o, norm_gain[i, 1], ffn_w_in[i], ffn_w_out[i],
                              lat_only=not need_ctx)
    return x_all.reshape(batch, seq, d)
```

```python
import functools
import math

import numpy as np
import jax
import jax.numpy as jnp
from jax import lax
from jax.experimental import pallas as pl
from jax.experimental.pallas import tpu as pltpu

F32 = jnp.float32
BF16 = jnp.bfloat16

EPS = 1e-6
GRID_W = 64
ROPE_THETA = 10000.0
N_MIXERS = 4

LANES = 128
SUBLANES = 8
TM = 256
COND_ROWS = 16
VMEM_LIMIT = 56 * 1024 * 1024

DIFF_HEAD_DIM = 64
GQA_HEAD_DIM = 128
GQA_GROUP = 4
HGRN_CHUNK = 64
HGRN_HALF = 32
HGRN_SAFE_LOG_DECAY = -80.0
FF_CHUNK = 256


def _cparams(*sem):
    return pltpu.CompilerParams(dimension_semantics=sem, vmem_limit_bytes=VMEM_LIMIT)


def _resident(shape):
    nd = len(shape)
    return pl.BlockSpec(shape, lambda *_: (0,) * nd, pipeline_mode=pl.Buffered(1))


def _silu(x):
    return x * jax.nn.sigmoid(x)


def _norm_mod(x, gain, shift, scale):
    ms = jnp.mean(x * x, axis=-1, keepdims=True)
    y = x * lax.rsqrt(ms + EPS) * gain
    return y * (1.0 + scale) + shift


def _dot(a, b):
    return jnp.dot(a, b, preferred_element_type=F32)


def _dot_nt(a, b):
    return lax.dot_general(a, b, (((1,), (1,)), ((), ())), preferred_element_type=F32)


def _dot_tn(a, b):
    return lax.dot_general(a, b, (((0,), (0,)), ((), ())), preferred_element_type=F32)


def _ada_kernel(cond_ref, w_ref, b_ref, o_ref):
    a = _silu(cond_ref[...]).astype(BF16)
    o_ref[...] = _dot(a, w_ref[...].astype(BF16)) + b_ref[...]


def _ada_modulation(cond, w_ada, b_ada):
    depth, d, n6 = w_ada.shape
    bn = n6 // 4
    out = pl.pallas_call(
        _ada_kernel,
        grid=(depth, n6 // bn),
        in_specs=[
            pl.BlockSpec((COND_ROWS, d), lambda i, j: (0, 0)),
            pl.BlockSpec((None, d, bn), lambda i, j: (i, 0, j)),
            pl.BlockSpec((None, 1, bn), lambda i, j: (i, 0, j)),
        ],
        out_specs=pl.BlockSpec((None, COND_ROWS, bn), lambda i, j: (i, 0, j)),
        out_shape=jax.ShapeDtypeStruct((depth, COND_ROWS, n6), F32),
        compiler_params=_cparams("parallel", "parallel"),
        name="ada_modulation",
    )(cond, w_ada, b_ada.reshape(depth, 1, n6))
    return out.reshape(depth * COND_ROWS * 6, 1, d)


class _Tokens:
    def __init__(self, batch, n_ctx, seq):
        assert n_ctx % TM == 0 and seq % TM == 0
        self.batch, self.n_ctx, self.seq = batch, n_ctx, seq
        self.t = n_ctx + seq
        self.ctx_tiles = n_ctx // TM
        self.tiles = self.t // TM

    def mod_spec(self, d, layer, chunk, tile_of):
        def index(i):
            b, r = tile_of(i)
            row = jnp.where(r < self.ctx_tiles, 0, 1 + b)
            return ((layer * COND_ROWS + row) * 6 + chunk, 0, 0)
        return pl.BlockSpec((None, 1, d), index)


def _vec_spec(n):
    return pl.BlockSpec((1, n), lambda *_: (0, 0))


def _dft_tables(n_ctx, seq, group_dim):
    def cs(n):
        k = np.arange(n, dtype=np.int64)
        ang = 2.0 * np.pi * ((k[:, None] * k[None, :]) % n).astype(np.float64) / n
        return np.cos(ang), np.sin(ang)
    cc, sc = cs(group_dim)
    chan = np.concatenate([cc, sc], axis=1)
    cl, sl = cs(seq)
    pos_lat = np.concatenate([cl, -sl], axis=1)
    cx, sx = cs(n_ctx)
    pos_ctx = np.concatenate([cx, -sx], axis=1)
    as_bf16 = lambda a: jnp.asarray(a.astype(np.float32)).astype(BF16)
    return as_bf16(chan), as_bf16(pos_ctx), as_bf16(pos_lat)


def _fnet_kernel(x_ref, ng_ref, shc_ref, scc_ref, shl_ref, scl_ref, chan_ref, pctx_ref, plat_ref,
                 y_ref, ab_ctx, ab_lat, *, n_ctx, seq, group_dim):
    j = pl.program_id(1)
    ctx_tiles = n_ctx // TM
    d = x_ref.shape[-1]

    @pl.when(j == 0)
    def _():
        for r in range((n_ctx + seq) // TM):
            is_ctx = r < ctx_tiles
            sh, sc = (shc_ref, scc_ref) if is_ctx else (shl_ref, scl_ref)
            h = _norm_mod(x_ref[r * TM:(r + 1) * TM, :], ng_ref[...], sh[...], sc[...]).astype(BF16)
            dst, n, row0 = (ab_ctx, n_ctx, r * TM) if is_ctx else (ab_lat, seq, r * TM - n_ctx)
            for g in range(d // group_dim):
                cols = slice(g * group_dim, (g + 1) * group_dim)
                ab = _dot(h[:, cols], chan_ref[...])
                dst[row0:row0 + TM, cols] = ab[:, :group_dim].astype(BF16)
                dst[n + row0:n + row0 + TM, cols] = ab[:, group_dim:].astype(BF16)

    @pl.when(j < ctx_tiles)
    def _():
        rows = pl.ds(pl.multiple_of(j * TM, TM), TM)
        y = _dot(pctx_ref[rows, :], ab_ctx[...]) * (1.0 / math.sqrt(n_ctx * group_dim))
        y_ref[...] = y.astype(BF16)

    @pl.when(j >= ctx_tiles)
    def _():
        y = _dot(plat_ref[...], ab_lat[...]) * (1.0 / math.sqrt(seq * group_dim))
        y_ref[...] = y.astype(BF16)


def _fnet_mix(tok, x_all, mod, layer, norm_gain, group_dim):
    b, t, d = tok.batch, tok.t, x_all.shape[-1]
    chan, pos_ctx, pos_lat = _dft_tables(tok.n_ctx, tok.seq, group_dim)
    ct = tok.ctx_tiles

    def mod_spec(chunk, ctx):
        return pl.BlockSpec((None, 1, d), lambda i, j: ((layer * COND_ROWS + (0 if ctx else 1 + i)) * 6 + chunk, 0, 0))

    kern = functools.partial(_fnet_kernel, n_ctx=tok.n_ctx, seq=tok.seq, group_dim=group_dim)
    return pl.pallas_call(
        kern,
        grid=(b, tok.tiles),
        in_specs=[
            pl.BlockSpec((None, t, d), lambda i, j: (i, 0, 0)),
            pl.BlockSpec((1, d), lambda i, j: (0, 0)),
            mod_spec(0, True), mod_spec(1, True), mod_spec(0, False), mod_spec(1, False),
            _resident(chan.shape),
            _resident(pos_ctx.shape),
            pl.BlockSpec((TM, 2 * tok.seq), lambda i, j: (jnp.maximum(j - ct, 0), 0)),
        ],
        out_specs=pl.BlockSpec((TM, d), lambda i, j: (i * tok.tiles + j, 0)),
        out_shape=jax.ShapeDtypeStruct((b * t, d), BF16),
        scratch_shapes=[pltpu.VMEM((2 * tok.n_ctx, d), BF16), pltpu.VMEM((2 * tok.seq, d), BF16)],
        compiler_params=_cparams("parallel", "arbitrary"),
        name="fnet_mix",
    )(x_all, norm_gain.reshape(1, d), mod, mod, mod, mod, chan, pos_ctx, pos_lat)


def _rope_tables(tok, head_dim):
    rows = tok.seq // GRID_W
    row = jnp.repeat(jnp.arange(rows, dtype=F32), GRID_W)
    col = jnp.tile(jnp.arange(GRID_W, dtype=F32), rows)
    n_freq = head_dim // 4
    inv_freq = ROPE_THETA ** (-jnp.arange(n_freq, dtype=F32) / n_freq)
    ang = jnp.concatenate([row[:, None] * inv_freq, col[:, None] * inv_freq], axis=-1)
    cos = jnp.concatenate([jnp.ones((tok.n_ctx, head_dim // 2), F32), jnp.cos(ang)], axis=0)
    sin = jnp.concatenate([jnp.zeros((tok.n_ctx, head_dim // 2), F32), jnp.sin(ang)], axis=0)
    reps = LANES // head_dim
    cos = jnp.tile(jnp.concatenate([cos, cos], axis=-1), (1, reps))
    sin = jnp.tile(jnp.concatenate([-sin, sin], axis=-1), (1, reps))
    return cos, sin


def _group_mean_matrix(group):
    g = np.arange(LANES) // group
    return jnp.asarray((g[:, None] == g[None, :]).astype(np.float32) / group).astype(BF16)


def _head_norm_rope(y, gain, gmean, cos, sin, head_dim):
    ms = _dot((y * y).astype(BF16), gmean)
    yn = y * lax.rsqrt(ms + EPS) * gain
    half = head_dim // 2
    if head_dim == LANES:
        partner = pltpu.roll(yn, half, 1)
    else:
        lane = lax.broadcasted_iota(jnp.int32, yn.shape, 1)
        partner = jnp.where(lane % head_dim < half, pltpu.roll(yn, LANES - half, 1), pltpu.roll(yn, half, 1))
    return yn * cos + partner * sin


def _qkv_kernel(x_ref, ng_ref, sh_ref, sc_ref, w_ref, qg_ref, kg_ref, cos_ref, sin_ref, gm_ref,
                q_ref, k_ref, v_ref, *, head_dim):
    h = _norm_mod(x_ref[...], ng_ref[...], sh_ref[...], sc_ref[...]).astype(BF16)
    cos, sin, gm = cos_ref[...], sin_ref[...], gm_ref[...]
    nq, nk = q_ref.shape[-1], k_ref.shape[-1]
    for o_ref, g_ref, col0, n in ((q_ref, qg_ref, 0, nq), (k_ref, kg_ref, nq, nk)):
        y = _dot(h, w_ref[:, col0:col0 + n])
        for blk in range(n // LANES):
            cols = slice(blk * LANES, (blk + 1) * LANES)
            o_ref[:, cols] = _head_norm_rope(y[:, cols], g_ref[...], gm, cos, sin, head_dim).astype(BF16)
    v_ref[...] = _dot(h, w_ref[:, nq + nk:]).astype(BF16)


def _qkv_project(tok, x_all, mod, layer, norm_gain, w_in, q_gain, k_gain, head_dim, nq, nk):
    n_rows, d = x_all.shape
    nv = w_in.shape[1] - nq - nk
    cos, sin = _rope_tables(tok, head_dim)
    mod_spec = lambda chunk: tok.mod_spec(d, layer, chunk, lambda i: (i // tok.tiles, i % tok.tiles))
    row_spec = lambda n: pl.BlockSpec((TM, n), lambda i: (i, 0))
    tab_spec = pl.BlockSpec((TM, LANES), lambda i: (i % tok.tiles, 0))
    return pl.pallas_call(
        functools.partial(_qkv_kernel, head_dim=head_dim),
        grid=(n_rows // TM,),
        in_specs=[row_spec(d), _vec_spec(d), mod_spec(0), mod_spec(1), _resident(w_in.shape),
                  _vec_spec(LANES), _vec_spec(LANES), tab_spec, tab_spec, _resident((LANES, LANES))],
        out_specs=[row_spec(nq), row_spec(nk), row_spec(nv)],
        out_shape=[jax.ShapeDtypeStruct((n_rows, n), BF16) for n in (nq, nk, nv)],
        compiler_params=_cparams("parallel"),
        name=f"qkv_project_hd{head_dim}",
    )(x_all, norm_gain.reshape(1, d), mod, mod, w_in.astype(BF16), q_gain, k_gain, cos, sin,
      _group_mean_matrix(head_dim))


def _softmax_parts(q, k):
    s = _dot_nt(q, k)
    p = jnp.exp(s - jnp.max(s, axis=-1, keepdims=True))
    return p, jnp.sum(p, axis=-1, keepdims=True)


def _diff_attn_kernel(lp_ref, sg_ref, q_ref, k_ref, v_ref, o_ref, *, lam_init, n_ctx):
    lp = lp_ref[...]
    lam = (jnp.exp(jnp.sum(lp[0:1] * lp[1:2], axis=-1, keepdims=True))
           - jnp.exp(jnp.sum(lp[2:3] * lp[3:4], axis=-1, keepdims=True)) + lam_init)
    q = q_ref[...]
    lane = lax.broadcasted_iota(jnp.int32, q.shape, 1)
    zero = jnp.zeros_like(q)
    q1 = jnp.where(lane < DIFF_HEAD_DIM, q, zero)
    q2 = jnp.where(lane >= DIFF_HEAD_DIM, q, zero)

    def attend(nk):
        k, v = k_ref[0:nk, :], v_ref[0:nk, :]
        p1, l1 = _softmax_parts(q1, k)
        p2, l2 = _softmax_parts(q2, k)
        w = p1 * (1.0 / l1) - p2 * (lam / l2)
        o = _dot(w.astype(BF16), v)
        ms = jnp.mean(o * o, axis=-1, keepdims=True)
        o_ref[...] = (o * lax.rsqrt(ms + EPS) * sg_ref[...]).astype(BF16)

    is_ctx = pl.program_id(2) < n_ctx // TM
    pl.when(is_ctx)(lambda: attend(n_ctx))
    pl.when(jnp.logical_not(is_ctx))(lambda: attend(k_ref.shape[0]))


def _diff_attention(tok, q, k, v, lam_par, subln_gain, lam_init):
    n_rows, d = q.shape
    heads = d // LANES
    t = tok.t
    k3, v3 = k.reshape(tok.batch, t, d), v.reshape(tok.batch, t, d)
    kv_spec = pl.BlockSpec((None, t, LANES), lambda b, h, j: (b, 0, h))
    qo_spec = pl.BlockSpec((TM, LANES), lambda b, h, j: (b * tok.tiles + j, h))
    return pl.pallas_call(
        functools.partial(_diff_attn_kernel, lam_init=lam_init, n_ctx=tok.n_ctx),
        grid=(tok.batch, heads, tok.tiles),
        in_specs=[pl.BlockSpec(lam_par.shape, lambda b, h, j: (0, 0)),
                  pl.BlockSpec((1, LANES), lambda b, h, j: (0, 0)), qo_spec, kv_spec, kv_spec],
        out_specs=qo_spec,
        out_shape=jax.ShapeDtypeStruct((n_rows, d), BF16),
        compiler_params=_cparams("parallel", "parallel", "arbitrary"),
        name="diff_attention",
    )(lam_par, (subln_gain * (1.0 - lam_init)).reshape(1, LANES), q, k3, v3)


def _gqa_attn_kernel(q_ref, k_ref, v_ref, o_ref, *, n_ctx, first_tile):
    def attend(nk):
        k, v = k_ref[0:nk, :], v_ref[0:nk, :]
        for g in range(GQA_GROUP):
            cols = slice(g * LANES, (g + 1) * LANES)
            p, l = _softmax_parts(q_ref[:, cols], k)
            o_ref[:, cols] = (_dot(p.astype(BF16), v) * (1.0 / l)).astype(BF16)

    if first_tile >= n_ctx // TM:
        attend(k_ref.shape[0])
    else:
        is_ctx = pl.program_id(2) + first_tile < n_ctx // TM
        pl.when(is_ctx)(lambda: attend(n_ctx))
        pl.when(jnp.logical_not(is_ctx))(lambda: attend(k_ref.shape[0]))


def _gqa_attention(tok, q, k, v, need_ctx):
    d = q.shape[1]
    kv_heads = k.shape[1] // LANES
    t = tok.t
    first_tile = 0 if need_ctx else tok.ctx_tiles
    per_b = tok.tiles - first_tile
    k3, v3 = k.reshape(tok.batch, t, k.shape[1]), v.reshape(tok.batch, t, v.shape[1])
    kv_spec = pl.BlockSpec((None, t, LANES), lambda b, h, j: (b, 0, h))
    q_spec = pl.BlockSpec((TM, GQA_GROUP * LANES), lambda b, h, j: (b * tok.tiles + first_tile + j, h))
    o_spec = pl.BlockSpec((TM, GQA_GROUP * LANES), lambda b, h, j: (b * per_b + j, h))
    return pl.pallas_call(
        functools.partial(_gqa_attn_kernel, n_ctx=tok.n_ctx, first_tile=first_tile),
        grid=(tok.batch, kv_heads, per_b),
        in_specs=[q_spec, kv_spec, kv_spec],
        out_specs=o_spec,
        out_shape=jax.ShapeDtypeStruct((tok.batch * per_b * TM, d), BF16),
        compiler_params=_cparams("parallel", "parallel", "arbitrary"),
        name="gqa_attention",
    )(q, k3, v3)


def _hgrn_proj_kernel(x_ref, ng_ref, sh_ref, sc_ref, w_ref, lb_ref, q_ref, lff_ref, lfb_ref, v_ref, g_ref,
                      *, layer_idx, depth):
    h = _norm_mod(x_ref[...], ng_ref[...], sh_ref[...], sc_ref[...]).astype(BF16)
    d = x_ref.shape[-1]
    q_ref[...] = _silu(_dot(h, w_ref[:, 0:d]))
    for direction, o_ref in enumerate((lff_ref, lfb_ref)):
        rows = [lb_ref[direction * depth + i:direction * depth + i + 1, :] for i in range(depth)]
        m = functools.reduce(jnp.maximum, rows)
        e = [jnp.exp(r - m) for r in rows]
        lb = sum(e[1:layer_idx + 1]) / sum(e) if layer_idx > 0 else jnp.zeros_like(m)
        z = _dot(h, w_ref[:, (1 + direction) * d:(2 + direction) * d])
        o_ref[...] = jnp.log(lb + (1.0 - lb) * jax.nn.sigmoid(z))
    v_ref[...] = _dot(h, w_ref[:, 3 * d:4 * d]).astype(BF16)
    g_ref[...] = _dot(h, w_ref[:, 4 * d:5 * d])


def _hgrn_project(tok, x_all, mod, layer, norm_gain, w_in, lower_bound):
    n_rows, d = x_all.shape
    depth = lower_bound.shape[1]
    mod_spec = lambda chunk: tok.mod_spec(d, layer, chunk, lambda i: (i // tok.tiles, i % tok.tiles))
    row_spec = pl.BlockSpec((TM, d), lambda i: (i, 0))
    return pl.pallas_call(
        functools.partial(_hgrn_proj_kernel, layer_idx=layer, depth=depth),
        grid=(n_rows // TM,),
        in_specs=[row_spec, _vec_spec(d), mod_spec(0), mod_spec(1), _resident(w_in.shape),
                  pl.BlockSpec((2 * depth, d), lambda i: (0, 0))],
        out_specs=[row_spec] * 5,
        out_shape=[jax.ShapeDtypeStruct((n_rows, d), dt) for dt in (F32, F32, F32, BF16, F32)],
        compiler_params=_cparams("parallel"),
        name="hgrn_project",
    )(x_all, norm_gain.reshape(1, d), mod, mod, w_in.astype(BF16), lower_bound.reshape(2 * depth, d))


def _split3(x):
    hi = x.astype(BF16)
    r1 = x - hi.astype(F32)
    mid = r1.astype(BF16)
    lo = (r1 - mid.astype(F32)).astype(BF16)
    return hi, mid, lo


def _hgrn_scan_kernel(q_ref, lff_ref, lfb_ref, v_ref, g_ref, ng_ref, o_ref,
                      acc, qs, xs, st, ds, cum, kk, *, n_ctx):
    t, dk = q_ref.shape
    c, hh = HGRN_CHUNK, HGRN_HALF
    n_chunks, ctx_chunks = t // c, n_ctx // c
    cpt = TM // c
    row = lax.broadcasted_iota(jnp.int32, (TM, TM), 0)
    col = lax.broadcasted_iota(jnp.int32, (TM, TM), 1)
    same_chunk = (row // c) == (col // c)
    same_half = (row // hh) == (col // hh)
    cross_half = same_chunk & jnp.logical_not(same_half)
    cum_mat = jnp.where(same_chunk & (col <= row), 1.0, 0.0).astype(BF16)
    in_chunk = lax.broadcasted_iota(jnp.int32, (cpt, c, dk), 1)
    row_c = lax.broadcasted_iota(jnp.int32, (c, c), 0)
    col_c = lax.broadcasted_iota(jnp.int32, (c, c), 1)

    def seg_min(ref):
        return jnp.min(ref[...].reshape(t // hh, hh, dk).sum(axis=1))
    unsafe = jnp.minimum(seg_min(lff_ref), seg_min(lfb_ref)) < HGRN_SAFE_LOG_DECAY

    def tile_step(ti, _):
        rows = pl.ds(pl.multiple_of(ti * TM, TM), TM)
        lf_f, lf_b = lff_ref[rows, :], lfb_ref[rows, :]
        hi, mid, lo = _split3(jnp.concatenate([lf_f, lf_b], axis=1))
        pre = _dot(cum_mat, hi) + _dot(cum_mat, mid) + _dot(cum_mat, lo)
        as4 = lambda a: a.reshape(cpt, c, dk)
        bc_f = as4(pre[:, :dk])
        pre_b = as4(pre[:, dk:])
        bc_b = pre_b[:, c - 1:c, :] - pre_b + as4(lf_b)
        q4, v = as4(q_ref[rows, :]), v_ref[rows, :]

        def direction(lf4, bc4, reverse):
            edge, last = (hh, 0) if reverse else (hh - 1, c - 1)
            second = (in_chunk < hh) if reverse else (in_chunk >= hh)
            k4 = 1.0 - jnp.exp(lf4)
            b_last, b_edge = bc4[:, last:last + 1, :], bc4[:, edge:edge + 1, :]
            q_s = q4 * jnp.exp(bc4)
            k_bar = k4 * jnp.exp(b_last - bc4)
            decay = jnp.exp(b_last)
            return k4, b_edge, second, q_s, k_bar, decay

        parts = [direction(as4(lf_f), bc_f, False), direction(as4(lf_b), bc_b, True)]
        flat = lambda a: a.reshape(TM, dk)
        qs[rows, :] = jnp.concatenate([flat(p[3]) for p in parts], axis=1).astype(BF16)
        k_bar2 = jnp.concatenate([flat(p[4]) for p in parts], axis=1).astype(BF16)
        decay2 = jnp.concatenate([p[5] for p in parts], axis=2)
        for cc in range(cpt):
            ci = ti * cpt + cc
            xs[ci] = _dot_tn(v[cc * c:(cc + 1) * c, :], k_bar2[cc * c:(cc + 1) * c, :])
            ds[ci] = jnp.broadcast_to(decay2[cc], ds.shape[1:])

        @pl.when(jnp.logical_not(unsafe))
        def _():
            a = jnp.zeros((TM, TM), F32)
            for (k4, b_edge, second, _, _, _), bc4, reverse in zip(parts, (bc_f, bc_b), (False, True)):
                tri = (col >= row) if reverse else (col <= row)
                as8 = lambda a_: a_.reshape(2 * cpt, hh, dk)
                bc8 = as8(bc4)
                ref = bc8[:, hh // 2:hh // 2 + 1, :]
                q_d = flat(as8(q4) * jnp.exp(bc8 - ref)).astype(BF16)
                k_d = flat(as8(k4) * jnp.exp(ref - bc8)).astype(BF16)
                e_o = jnp.exp(jnp.where(second, bc4 - b_edge, b_edge - bc4))
                q_o = flat(jnp.where(second, q4 * e_o, 0.0)).astype(BF16)
                k_o = flat(jnp.where(second, 0.0, k4 * e_o)).astype(BF16)
                a = a + jnp.where(same_half & tri, _dot_nt(q_d, k_d), jnp.where(cross_half, _dot_nt(q_o, k_o), 0.0))
            acc[rows, :] = _dot(a.astype(BF16), v)

        @pl.when(unsafe)
        def _():
            for cc in range(cpt):
                a = jnp.zeros((c, c), F32)
                q_c = q4[cc]
                for (k4, _, _, _, _, _), bc4, reverse in zip(parts, (bc_f, bc_b), (False, True)):
                    bc_c = bc4[cc]
                    cum[...] = bc_c
                    kk[...] = k4[cc]

                    def col_step(s, a_):
                        d_s = jnp.exp(jnp.minimum(bc_c - cum[pl.ds(s, 1), :], 0.0))
                        w = jnp.sum(q_c * kk[pl.ds(s, 1), :] * d_s, axis=-1, keepdims=True)
                        return jnp.where(col_c == s, w, a_)
                    a_dir = lax.fori_loop(0, c, col_step, jnp.zeros((c, c), F32))
                    a = a + jnp.where((col_c >= row_c) if reverse else (col_c <= row_c), a_dir, 0.0)
                crow = pl.ds(pl.multiple_of(ti * TM + cc * c, c), c)
                acc[crow, :] = _dot(a.astype(BF16), v[cc * c:(cc + 1) * c, :])
        return 0

    lax.fori_loop(0, t // TM, tile_step, 0)

    def state_step(i, carry):
        s_f, s_b = carry
        cf = i
        cb = jnp.where(i < ctx_chunks, ctx_chunks - 1 - i, n_chunks - 1 + ctx_chunks - i)
        st[cf, :, 0:dk] = s_f.astype(BF16)
        st[cb, :, dk:2 * dk] = s_b.astype(BF16)
        s_f = s_f * ds[cf, 0:1, 0:dk] + xs[cf, :, 0:dk]
        s_b = s_b * ds[cb, 0:1, dk:2 * dk] + xs[cb, :, dk:2 * dk]
        return s_f, s_b

    zero = jnp.zeros((dk, dk), F32)
    lax.fori_loop(0, n_chunks, state_step, (zero, zero))

    def finish_step(ti, _):
        rows = pl.ds(pl.multiple_of(ti * TM, TM), TM)
        inter = [_dot_nt(qs[pl.ds(pl.multiple_of(ti * TM + cc * c, c), c), :], st[ti * cpt + cc]) for cc in range(cpt)]
        o = acc[rows, :] + jnp.concatenate(inter, axis=0)
        ms = jnp.mean(o * o, axis=-1, keepdims=True)
        o_ref[rows, :] = (o * lax.rsqrt(ms + EPS) * ng_ref[...] * _silu(g_ref[rows, :])).astype(BF16)
        return 0

    lax.fori_loop(0, t // TM, finish_step, 0)


def _hgrn_scan(tok, q, lff, lfb, v, g, norm_gain):
    n_rows, d = q.shape
    heads = d // LANES
    t = tok.t
    n_chunks = t // HGRN_CHUNK
    spec = pl.BlockSpec((None, t, LANES), lambda b, h: (b, 0, h))
    as3 = lambda a: a.reshape(tok.batch, t, d)
    out = pl.pallas_call(
        functools.partial(_hgrn_scan_kernel, n_ctx=tok.n_ctx),
        grid=(tok.batch, heads),
        in_specs=[spec] * 5 + [pl.BlockSpec((1, LANES), lambda b, h: (0, 0))],
        out_specs=spec,
        out_shape=jax.ShapeDtypeStruct((tok.batch, t, d), BF16),
        scratch_shapes=[pltpu.VMEM((t, LANES), F32), pltpu.VMEM((t, 2 * LANES), BF16),
                        pltpu.VMEM((n_chunks, LANES, 2 * LANES), F32), pltpu.VMEM((n_chunks, LANES, 2 * LANES), BF16),
                        pltpu.VMEM((n_chunks, SUBLANES, 2 * LANES), F32),
                        pltpu.VMEM((HGRN_CHUNK, LANES), F32), pltpu.VMEM((HGRN_CHUNK, LANES), F32)],
        compiler_params=_cparams("parallel", "parallel"),
        name="hgrn_scan",
    )(as3(q), as3(lff), as3(lfb), as3(v), as3(g), norm_gain.reshape(1, LANES))
    return out.reshape(n_rows, d)


def _ffn_kernel(x_ref, o_ref, wo_ref, bo_ref, g1_ref, ng_ref, sh_ref, sc_ref, g2_ref, wi_ref, wf_ref, y_ref):
    x1 = x_ref[...] + g1_ref[...] * (_dot(o_ref[...], wo_ref[...]) + bo_ref[...])
    h = _norm_mod(x1, ng_ref[...], sh_ref[...], sc_ref[...]).astype(BF16)
    d_ff = wf_ref.shape[0]
    acc = jnp.zeros(x1.shape, F32)
    for c0 in range(0, d_ff, FF_CHUNK):
        gate = _dot(h, wi_ref[:, c0:c0 + FF_CHUNK])
        up = _dot(h, wi_ref[:, d_ff + c0:d_ff + c0 + FF_CHUNK])
        acc = acc + _dot((_silu(gate) * up).astype(BF16), wf_ref[c0:c0 + FF_CHUNK, :])
    y_ref[...] = x1 + g2_ref[...] * acc


def _out_proj_ffn(tok, x_all, o_all, mod, layer, w_o, b_o, norm_gain, w_in, w_out, lat_only):
    n_rows, d = x_all.shape
    d_ff = w_out.shape[0]
    assert d_ff % FF_CHUNK == 0
    first = tok.ctx_tiles if lat_only else 0
    per_b = tok.tiles - first
    tile_of = lambda i: (i // per_b, first + i % per_b)
    mod_spec = lambda chunk: tok.mod_spec(d, layer, chunk, tile_of)
    stream_row = pl.BlockSpec((TM, d), lambda i: (tile_of(i)[0] * tok.tiles + tile_of(i)[1], 0))
    out_row = pl.BlockSpec((TM, d), lambda i: (i, 0))
    o_row = stream_row if o_all.shape[0] == n_rows else out_row
    return pl.pallas_call(
        _ffn_kernel,
        grid=(tok.batch * per_b,),
        in_specs=[stream_row, o_row, _resident(w_o.shape), _vec_spec(d), mod_spec(2), _vec_spec(d),
                  mod_spec(3), mod_spec(4), mod_spec(5), _resident(w_in.shape), _resident(w_out.shape)],
        out_specs=out_row,
        out_shape=jax.ShapeDtypeStruct((tok.batch * per_b * TM, d), F32),
        compiler_params=_cparams("parallel"),
        name="out_proj_ffn",
    )(x_all, o_all, w_o.astype(BF16), b_o.reshape(1, d), mod, norm_gain.reshape(1, d), mod, mod, mod,
      w_in.astype(BF16), w_out.astype(BF16))


def kernel(x, c, ctx, c_ctx, w_ada, b_ada, norm_gain, ffn_w_in, ffn_w_out, fnet_w_out, fnet_b_out,
           diff_w_in, diff_q_gain, diff_k_gain, diff_lambda, diff_subln_gain, diff_w_out,
           hgrn_w_in, hgrn_lower_bound, hgrn_norm_gain, hgrn_w_out,
           gqa_w_in, gqa_q_gain, gqa_k_gain, gqa_w_out):
    batch, seq, d = x.shape
    n_ctx = ctx.shape[1]
    depth = w_ada.shape[0]
    assert batch + 1 <= COND_ROWS and d % LANES == 0
    tok = _Tokens(batch, n_ctx, seq)

    cond = jnp.zeros((COND_ROWS, d), F32).at[0].set(c_ctx).at[1:1 + batch].set(c)
    mod = _ada_modulation(cond, w_ada, b_ada)
    x_all = jnp.concatenate([ctx, x], axis=1).reshape(batch * tok.t, d)
    zero_bias = jnp.zeros((d,), F32)

    for i in range(depth):
        m, j = i % N_MIXERS, i // N_MIXERS
        need_ctx = i < depth - 1
        if m == 0:
            group_dim = d // 8
            o = _fnet_mix(tok, x_all.reshape(batch, tok.t, d), mod, i, norm_gain[i, 0], group_dim)
            w_o, b_o = fnet_w_out[j], fnet_b_out[j]
        elif m == 1:
            lam_init = 0.8 - 0.6 * math.exp(-0.3 * i)
            qg = diff_q_gain[j].reshape(1, LANES) * DIFF_HEAD_DIM ** -0.5
            kg = diff_k_gain[j].reshape(1, LANES)
            q, k, v = _qkv_project(tok, x_all, mod, i, norm_gain[i, 0], diff_w_in[j], qg, kg,
                                   DIFF_HEAD_DIM, d, d)
            o = _diff_attention(tok, q, k, v, diff_lambda[j], diff_subln_gain[j], lam_init)
            w_o, b_o = diff_w_out[j], zero_bias
        elif m == 2:
            q, lff, lfb, v, g = _hgrn_project(tok, x_all, mod, i, norm_gain[i, 0], hgrn_w_in[j], hgrn_lower_bound)
            o = _hgrn_scan(tok, q, lff, lfb, v, g, hgrn_norm_gain[j])
            w_o, b_o = hgrn_w_out[j], zero_bias
        else:
            kv = (gqa_w_in.shape[-1] - d) // 2
            qg = gqa_q_gain[j].reshape(1, LANES) * GQA_HEAD_DIM ** -0.5
            kg = gqa_k_gain[j].reshape(1, LANES)
            q, k, v = _qkv_project(tok, x_all, mod, i, norm_gain[i, 0], gqa_w_in[j], qg, kg, GQA_HEAD_DIM, d, kv)
            o = _gqa_attention(tok, q, k, v, need_ctx)
            w_o, b_o = gqa_w_out[j], zero_bias
        x_all = _out_proj_ffn(tok, x_all, o, mod, i, w_o, b_o, norm_gain[i, 1], ffn_w_in[i], ffn_w_out[i],
                              lat_only=not need_ctx)
    return x_all.reshape(batch, seq, d)
```

```python
import functools
import math

import numpy as np
import jax
import jax.numpy as jnp
from jax import lax
from jax.experimental import pallas as pl
from jax.experimental.pallas import tpu as pltpu

F32 = jnp.float32
BF16 = jnp.bfloat16

EPS = 1e-6
GRID_W = 64
ROPE_THETA = 10000.0
N_MIXERS = 4

LANES = 128
SUBLANES = 8
TM = 256
COND_ROWS = 16
VMEM_LIMIT = 56 * 1024 * 1024

DIFF_HEAD_DIM = 64
GQA_HEAD_DIM = 128
GQA_GROUP = 4
HGRN_CHUNK = 64
HGRN_HALF = 32
HGRN_SAFE_LOG_DECAY = -80.0
FF_CHUNK = 256
FFN_MAX_ROWS = 768
LOG2_E = math.log2(math.e)
EXP2_SAFE_LOGIT = 100.0
BF16_SLACK = 1.02


def _cparams(*sem):
    return pltpu.CompilerParams(dimension_semantics=sem, vmem_limit_bytes=VMEM_LIMIT)


def _resident(shape):
    nd = len(shape)
    return pl.BlockSpec(shape, lambda *_: (0,) * nd, pipeline_mode=pl.Buffered(1))


def _silu(x):
    return x * jax.nn.sigmoid(x)


def _norm_mod(x, gain, shift, scale):
    ms = jnp.mean(x * x, axis=-1, keepdims=True)
    y = x * lax.rsqrt(ms + EPS) * gain
    return y * (1.0 + scale) + shift


def _dot(a, b):
    return jnp.dot(a, b, preferred_element_type=F32)


def _dot_nt(a, b):
    return lax.dot_general(a, b, (((1,), (1,)), ((), ())), preferred_element_type=F32)


def _dot_tn(a, b):
    return lax.dot_general(a, b, (((0,), (0,)), ((), ())), preferred_element_type=F32)


def _ada_kernel(cond_ref, w_ref, b_ref, o_ref):
    a = _silu(cond_ref[...]).astype(BF16)
    o_ref[...] = _dot(a, w_ref[...].astype(BF16)) + b_ref[...]


def _ada_modulation(cond, w_ada, b_ada):
    depth, d, n6 = w_ada.shape
    bn = n6 // 4
    out = pl.pallas_call(
        _ada_kernel,
        grid=(depth, n6 // bn),
        in_specs=[
            pl.BlockSpec((COND_ROWS, d), lambda i, j: (0, 0)),
            pl.BlockSpec((None, d, bn), lambda i, j: (i, 0, j)),
            pl.BlockSpec((None, 1, bn), lambda i, j: (i, 0, j)),
        ],
        out_specs=pl.BlockSpec((None, COND_ROWS, bn), lambda i, j: (i, 0, j)),
        out_shape=jax.ShapeDtypeStruct((depth, COND_ROWS, n6), F32),
        compiler_params=_cparams("parallel", "parallel"),
        name="ada_modulation",
    )(cond, w_ada, b_ada.reshape(depth, 1, n6))
    return out.reshape(depth * COND_ROWS * 6, 1, d)


class _Tokens:
    def __init__(self, batch, n_ctx, seq):
        assert n_ctx % TM == 0 and seq % TM == 0
        self.batch, self.n_ctx, self.seq = batch, n_ctx, seq
        self.t = n_ctx + seq
        self.ctx_tiles = n_ctx // TM
        self.tiles = self.t // TM

    def mod_spec(self, d, layer, chunk, tile_of):
        def index(i):
            b, r = tile_of(i)
            row = jnp.where(r < self.ctx_tiles, 0, 1 + b)
            return ((layer * COND_ROWS + row) * 6 + chunk, 0, 0)
        return pl.BlockSpec((None, 1, d), index)


def _vec_spec(n):
    return pl.BlockSpec((1, n), lambda *_: (0, 0))


def _dft_tables(n_ctx, seq, group_dim):
    def cs(n):
        k = np.arange(n, dtype=np.int64)
        ang = 2.0 * np.pi * ((k[:, None] * k[None, :]) % n).astype(np.float64) / n
        return np.cos(ang), np.sin(ang)
    cc, sc = cs(group_dim)
    chan = np.concatenate([cc, sc], axis=1)
    cl, sl = cs(seq)
    pos_lat = np.concatenate([cl, -sl], axis=1)
    cx, sx = cs(n_ctx)
    pos_ctx = np.concatenate([cx, -sx], axis=1)
    as_bf16 = lambda a: jnp.asarray(a.astype(np.float32)).astype(BF16)
    return as_bf16(chan), as_bf16(pos_ctx), as_bf16(pos_lat)


def _fnet_kernel(x_ref, ng_ref, shc_ref, scc_ref, shl_ref, scl_ref, chan_ref, pctx_ref, plat_ref,
                 y_ref, ab_ctx, ab_lat, *, n_ctx, seq, group_dim):
    j = pl.program_id(1)
    ctx_tiles = n_ctx // TM
    d = x_ref.shape[-1]

    @pl.when(j == 0)
    def _():
        for r in range((n_ctx + seq) // TM):
            is_ctx = r < ctx_tiles
            sh, sc = (shc_ref, scc_ref) if is_ctx else (shl_ref, scl_ref)
            h = _norm_mod(x_ref[r * TM:(r + 1) * TM, :], ng_ref[...], sh[...], sc[...]).astype(BF16)
            dst, n, row0 = (ab_ctx, n_ctx, r * TM) if is_ctx else (ab_lat, seq, r * TM - n_ctx)
            for g in range(d // group_dim):
                cols = slice(g * group_dim, (g + 1) * group_dim)
                ab = _dot(h[:, cols], chan_ref[...])
                dst[row0:row0 + TM, cols] = ab[:, :group_dim].astype(BF16)
                dst[n + row0:n + row0 + TM, cols] = ab[:, group_dim:].astype(BF16)

    @pl.when(j < ctx_tiles)
    def _():
        rows = pl.ds(pl.multiple_of(j * TM, TM), TM)
        y = _dot(pctx_ref[rows, :], ab_ctx[...]) * (1.0 / math.sqrt(n_ctx * group_dim))
        y_ref[...] = y.astype(BF16)

    @pl.when(j >= ctx_tiles)
    def _():
        y = _dot(plat_ref[...], ab_lat[...]) * (1.0 / math.sqrt(seq * group_dim))
        y_ref[...] = y.astype(BF16)


def _fnet_mix(tok, x_all, mod, layer, norm_gain, group_dim):
    b, t, d = tok.batch, tok.t, x_all.shape[-1]
    chan, pos_ctx, pos_lat = _dft_tables(tok.n_ctx, tok.seq, group_dim)
    ct = tok.ctx_tiles

    def mod_spec(chunk, ctx):
        return pl.BlockSpec((None, 1, d), lambda i, j: ((layer * COND_ROWS + (0 if ctx else 1 + i)) * 6 + chunk, 0, 0))

    kern = functools.partial(_fnet_kernel, n_ctx=tok.n_ctx, seq=tok.seq, group_dim=group_dim)
    return pl.pallas_call(
        kern,
        grid=(b, tok.tiles),
        in_specs=[
            pl.BlockSpec((None, t, d), lambda i, j: (i, 0, 0)),
            pl.BlockSpec((1, d), lambda i, j: (0, 0)),
            mod_spec(0, True), mod_spec(1, True), mod_spec(0, False), mod_spec(1, False),
            _resident(chan.shape),
            _resident(pos_ctx.shape),
            pl.BlockSpec((TM, 2 * tok.seq), lambda i, j: (jnp.maximum(j - ct, 0), 0)),
        ],
        out_specs=pl.BlockSpec((TM, d), lambda i, j: (i * tok.tiles + j, 0)),
        out_shape=jax.ShapeDtypeStruct((b * t, d), BF16),
        scratch_shapes=[pltpu.VMEM((2 * tok.n_ctx, d), BF16), pltpu.VMEM((2 * tok.seq, d), BF16)],
        compiler_params=_cparams("parallel", "arbitrary"),
        name="fnet_mix",
    )(x_all, norm_gain.reshape(1, d), mod, mod, mod, mod, chan, pos_ctx, pos_lat)


def _rope_tables(tok, head_dim):
    rows = tok.seq // GRID_W
    row = jnp.repeat(jnp.arange(rows, dtype=F32), GRID_W)
    col = jnp.tile(jnp.arange(GRID_W, dtype=F32), rows)
    n_freq = head_dim // 4
    inv_freq = ROPE_THETA ** (-jnp.arange(n_freq, dtype=F32) / n_freq)
    ang = jnp.concatenate([row[:, None] * inv_freq, col[:, None] * inv_freq], axis=-1)
    cos = jnp.concatenate([jnp.ones((tok.n_ctx, head_dim // 2), F32), jnp.cos(ang)], axis=0)
    sin = jnp.concatenate([jnp.zeros((tok.n_ctx, head_dim // 2), F32), jnp.sin(ang)], axis=0)
    reps = LANES // head_dim
    cos = jnp.tile(jnp.concatenate([cos, cos], axis=-1), (1, reps))
    sin = jnp.tile(jnp.concatenate([-sin, sin], axis=-1), (1, reps))
    return cos, sin


def _group_mean_matrix(group):
    g = np.arange(LANES) // group
    return jnp.asarray((g[:, None] == g[None, :]).astype(np.float32) / group).astype(BF16)


def _head_norm_rope(y, gain, gmean, cos, sin, head_dim):
    ms = _dot((y * y).astype(BF16), gmean)
    yn = y * lax.rsqrt(ms + EPS) * gain
    half = head_dim // 2
    if head_dim == LANES:
        partner = pltpu.roll(yn, half, 1)
    else:
        lane = lax.broadcasted_iota(jnp.int32, yn.shape, 1)
        partner = jnp.where(lane % head_dim < half, pltpu.roll(yn, LANES - half, 1), pltpu.roll(yn, half, 1))
    return yn * cos + partner * sin


def _qkv_kernel(x_ref, ng_ref, sh_ref, sc_ref, w_ref, qg_ref, kg_ref, cos_ref, sin_ref, gm_ref,
                q_ref, k_ref, v_ref, *, head_dim):
    h = _norm_mod(x_ref[...], ng_ref[...], sh_ref[...], sc_ref[...]).astype(BF16)
    cos, sin, gm = cos_ref[...], sin_ref[...], gm_ref[...]
    nq, nk = q_ref.shape[-1], k_ref.shape[-1]
    for o_ref, g_ref, col0, n in ((q_ref, qg_ref, 0, nq), (k_ref, kg_ref, nq, nk)):
        y = _dot(h, w_ref[:, col0:col0 + n])
        for blk in range(n // LANES):
            cols = slice(blk * LANES, (blk + 1) * LANES)
            o_ref[:, cols] = _head_norm_rope(y[:, cols], g_ref[...], gm, cos, sin, head_dim).astype(BF16)
    v_ref[...] = _dot(h, w_ref[:, nq + nk:]).astype(BF16)


def _qkv_project(tok, x_all, mod, layer, norm_gain, w_in, q_gain, k_gain, head_dim, nq, nk):
    n_rows, d = x_all.shape
    nv = w_in.shape[1] - nq - nk
    cos, sin = _rope_tables(tok, head_dim)
    mod_spec = lambda chunk: tok.mod_spec(d, layer, chunk, lambda i: (i // tok.tiles, i % tok.tiles))
    row_spec = lambda n: pl.BlockSpec((TM, n), lambda i: (i, 0))
    tab_spec = pl.BlockSpec((TM, LANES), lambda i: (i % tok.tiles, 0))
    return pl.pallas_call(
        functools.partial(_qkv_kernel, head_dim=head_dim),
        grid=(n_rows // TM,),
        in_specs=[row_spec(d), _vec_spec(d), mod_spec(0), mod_spec(1), _resident(w_in.shape),
                  _vec_spec(LANES), _vec_spec(LANES), tab_spec, tab_spec, _resident((LANES, LANES))],
        out_specs=[row_spec(nq), row_spec(nk), row_spec(nv)],
        out_shape=[jax.ShapeDtypeStruct((n_rows, n), BF16) for n in (nq, nk, nv)],
        compiler_params=_cparams("parallel"),
        name=f"qkv_project_hd{head_dim}",
    )(x_all, norm_gain.reshape(1, d), mod, mod, w_in.astype(BF16), q_gain, k_gain, cos, sin,
      _group_mean_matrix(head_dim))


def _softmax_parts(q, k):
    s = _dot_nt(q, k)
    p = jnp.exp2(s - jnp.max(s, axis=-1, keepdims=True))
    return p, jnp.sum(p, axis=-1, keepdims=True)


def _softmax_parts_bounded(k, q, vt):
    p = jnp.exp2(_dot_nt(k, q))
    l = jnp.sum(p, axis=0, keepdims=True)
    return _dot(vt, p.astype(BF16)) * (1.0 / l)


def _logit_bound(qg_ref, kg_ref, head_dim):
    return BF16_SLACK * head_dim * jnp.max(jnp.abs(qg_ref[...])) * jnp.max(jnp.abs(kg_ref[...]))


def _diff_attn_kernel(lp_ref, sg_ref, qg_ref, kg_ref, q_ref, k_ref, v_ref, o_ref, vt, *, lam_init, n_ctx):
    lp = lp_ref[...]
    lam = (jnp.exp(jnp.sum(lp[0:1] * lp[1:2], axis=-1, keepdims=True))
           - jnp.exp(jnp.sum(lp[2:3] * lp[3:4], axis=-1, keepdims=True)) + lam_init)
    q = q_ref[...]
    lane = lax.broadcasted_iota(jnp.int32, q.shape, 1)
    zero = jnp.zeros_like(q)
    q1 = jnp.where(lane < DIFF_HEAD_DIM, q, zero)
    q2 = jnp.where(lane >= DIFF_HEAD_DIM, q, zero)
    bounded = _logit_bound(qg_ref, kg_ref, DIFF_HEAD_DIM) <= EXP2_SAFE_LOGIT

    @pl.when(bounded & (pl.program_id(2) == 0))
    def _():
        vt[...] = v_ref[...].T

    def attend_bounded(nk):
        o12 = _softmax_parts_bounded(k_ref[0:nk, :], jnp.concatenate([q1, q2], axis=0), vt[:, 0:nk])
        ot = o12[:, :TM] - lam * o12[:, TM:]
        ms = jnp.mean(ot * ot, axis=0, keepdims=True)
        o_ref[...] = ((ot * lax.rsqrt(ms + EPS)).T * sg_ref[...]).astype(BF16)

    def attend(nk):
        k, v = k_ref[0:nk, :], v_ref[0:nk, :]
        p1, l1 = _softmax_parts(q1, k)
        p2, l2 = _softmax_parts(q2, k)
        w = p1 * (1.0 / l1) - p2 * (lam / l2)
        o = _dot(w.astype(BF16), v)
        ms = jnp.mean(o * o, axis=-1, keepdims=True)
        o_ref[...] = (o * lax.rsqrt(ms + EPS) * sg_ref[...]).astype(BF16)

    is_ctx = pl.program_id(2) < n_ctx // TM
    for ctx_tile, nk in ((True, n_ctx), (False, k_ref.shape[0])):
        here = is_ctx if ctx_tile else jnp.logical_not(is_ctx)
        pl.when(here & bounded)(functools.partial(attend_bounded, nk))
        pl.when(here & jnp.logical_not(bounded))(functools.partial(attend, nk))


def _diff_attention(tok, q, k, v, q_gain, k_gain, lam_par, subln_gain, lam_init):
    n_rows, d = q.shape
    heads = d // LANES
    t = tok.t
    k3, v3 = k.reshape(tok.batch, t, d), v.reshape(tok.batch, t, d)
    kv_spec = pl.BlockSpec((None, t, LANES), lambda b, h, j: (b, 0, h))
    qo_spec = pl.BlockSpec((TM, LANES), lambda b, h, j: (b * tok.tiles + j, h))
    vec = pl.BlockSpec((1, LANES), lambda b, h, j: (0, 0))
    return pl.pallas_call(
        functools.partial(_diff_attn_kernel, lam_init=lam_init, n_ctx=tok.n_ctx),
        grid=(tok.batch, heads, tok.tiles),
        in_specs=[pl.BlockSpec(lam_par.shape, lambda b, h, j: (0, 0)), vec, vec, vec, qo_spec, kv_spec, kv_spec],
        out_specs=qo_spec,
        out_shape=jax.ShapeDtypeStruct((n_rows, d), BF16),
        scratch_shapes=[pltpu.VMEM((LANES, t), BF16)],
        compiler_params=_cparams("parallel", "parallel", "arbitrary"),
        name="diff_attention",
    )(lam_par, (subln_gain * (1.0 - lam_init)).reshape(1, LANES), q_gain, k_gain, q, k3, v3)


def _gqa_attn_kernel(qg_ref, kg_ref, q_ref, k_ref, v_ref, o_ref, vt, *, n_ctx, first_tile):
    bounded = _logit_bound(qg_ref, kg_ref, GQA_HEAD_DIM) <= EXP2_SAFE_LOGIT

    @pl.when(bounded & (pl.program_id(2) == 0))
    def _():
        vt[...] = v_ref[...].T

    def attend_bounded(nk):
        k, vt_k = k_ref[0:nk, :], vt[:, 0:nk]
        for g in range(0, GQA_GROUP, 2):
            q2h = jnp.concatenate([q_ref[:, g * LANES:(g + 1) * LANES], q_ref[:, (g + 1) * LANES:(g + 2) * LANES]], axis=0)
            o2h = _softmax_parts_bounded(k, q2h, vt_k)
            o_ref[:, g * LANES:(g + 1) * LANES] = o2h[:, :TM].T.astype(BF16)
            o_ref[:, (g + 1) * LANES:(g + 2) * LANES] = o2h[:, TM:].T.astype(BF16)

    def attend(nk):
        k, v = k_ref[0:nk, :], v_ref[0:nk, :]
        for g in range(GQA_GROUP):
            cols = slice(g * LANES, (g + 1) * LANES)
            p, l = _softmax_parts(q_ref[:, cols], k)
            o_ref[:, cols] = (_dot(p.astype(BF16), v) * (1.0 / l)).astype(BF16)

    is_ctx = pl.program_id(2) + first_tile < n_ctx // TM
    for ctx_tile, nk in ((True, n_ctx), (False, k_ref.shape[0])):
        if ctx_tile and first_tile >= n_ctx // TM:
            continue
        here = is_ctx if ctx_tile else jnp.logical_not(is_ctx)
        pl.when(here & bounded)(functools.partial(attend_bounded, nk))
        pl.when(here & jnp.logical_not(bounded))(functools.partial(attend, nk))


def _gqa_attention(tok, q, k, v, q_gain, k_gain, need_ctx):
    d = q.shape[1]
    kv_heads = k.shape[1] // LANES
    t = tok.t
    first_tile = 0 if need_ctx else tok.ctx_tiles
    per_b = tok.tiles - first_tile
    k3, v3 = k.reshape(tok.batch, t, k.shape[1]), v.reshape(tok.batch, t, v.shape[1])
    kv_spec = pl.BlockSpec((None, t, LANES), lambda b, h, j: (b, 0, h))
    q_spec = pl.BlockSpec((TM, GQA_GROUP * LANES), lambda b, h, j: (b * tok.tiles + first_tile + j, h))
    o_spec = pl.BlockSpec((TM, GQA_GROUP * LANES), lambda b, h, j: (b * per_b + j, h))
    vec = pl.BlockSpec((1, LANES), lambda b, h, j: (0, 0))
    return pl.pallas_call(
        functools.partial(_gqa_attn_kernel, n_ctx=tok.n_ctx, first_tile=first_tile),
        grid=(tok.batch, kv_heads, per_b),
        in_specs=[vec, vec, q_spec, kv_spec, kv_spec],
        out_specs=o_spec,
        out_shape=jax.ShapeDtypeStruct((tok.batch * per_b * TM, d), BF16),
        scratch_shapes=[pltpu.VMEM((LANES, t), BF16)],
        compiler_params=_cparams("parallel", "parallel", "arbitrary"),
        name="gqa_attention",
    )(q_gain, k_gain, q, k3, v3)


def _hgrn_proj_kernel(x_ref, ng_ref, sh_ref, sc_ref, w_ref, lb_ref, q_ref, lff_ref, lfb_ref, v_ref, g_ref,
                      *, layer_idx, depth):
    h = _norm_mod(x_ref[...], ng_ref[...], sh_ref[...], sc_ref[...]).astype(BF16)
    d = x_ref.shape[-1]
    q_ref[...] = _silu(_dot(h, w_ref[:, 0:d]))
    for direction, o_ref in enumerate((lff_ref, lfb_ref)):
        rows = [lb_ref[direction * depth + i:direction * depth + i + 1, :] for i in range(depth)]
        m = functools.reduce(jnp.maximum, rows)
        e = [jnp.exp(r - m) for r in rows]
        lb = sum(e[1:layer_idx + 1]) / sum(e) if layer_idx > 0 else jnp.zeros_like(m)
        z = _dot(h, w_ref[:, (1 + direction) * d:(2 + direction) * d])
        o_ref[...] = jnp.log(lb + (1.0 - lb) * jax.nn.sigmoid(z))
    v_ref[...] = _dot(h, w_ref[:, 3 * d:4 * d]).astype(BF16)
    g_ref[...] = _dot(h, w_ref[:, 4 * d:5 * d])


def _hgrn_project(tok, x_all, mod, layer, norm_gain, w_in, lower_bound):
    n_rows, d = x_all.shape
    depth = lower_bound.shape[1]
    mod_spec = lambda chunk: tok.mod_spec(d, layer, chunk, lambda i: (i // tok.tiles, i % tok.tiles))
    row_spec = pl.BlockSpec((TM, d), lambda i: (i, 0))
    return pl.pallas_call(
        functools.partial(_hgrn_proj_kernel, layer_idx=layer, depth=depth),
        grid=(n_rows // TM,),
        in_specs=[row_spec, _vec_spec(d), mod_spec(0), mod_spec(1), _resident(w_in.shape),
                  pl.BlockSpec((2 * depth, d), lambda i: (0, 0))],
        out_specs=[row_spec] * 5,
        out_shape=[jax.ShapeDtypeStruct((n_rows, d), dt) for dt in (F32, F32, F32, BF16, F32)],
        compiler_params=_cparams("parallel"),
        name="hgrn_project",
    )(x_all, norm_gain.reshape(1, d), mod, mod, w_in.astype(BF16), lower_bound.reshape(2 * depth, d))


def _split3(x):
    hi = x.astype(BF16)
    r1 = x - hi.astype(F32)
    mid = r1.astype(BF16)
    lo = (r1 - mid.astype(F32)).astype(BF16)
    return hi, mid, lo


def _hgrn_scan_kernel(q_ref, lff_ref, lfb_ref, v_ref, g_ref, ng_ref, o_ref,
                      acc, qs, xs, st, ds, cum, kk, *, n_ctx):
    t, dk = q_ref.shape
    c, hh = HGRN_CHUNK, HGRN_HALF
    n_chunks, ctx_chunks = t // c, n_ctx // c
    cpt = TM // c
    row = lax.broadcasted_iota(jnp.int32, (TM, TM), 0)
    col = lax.broadcasted_iota(jnp.int32, (TM, TM), 1)
    same_chunk = (row // c) == (col // c)
    same_half = (row // hh) == (col // hh)
    cross_half = same_chunk & jnp.logical_not(same_half)
    cum_mat = jnp.where(same_chunk & (col <= row), 1.0, 0.0).astype(BF16)
    in_chunk = lax.broadcasted_iota(jnp.int32, (cpt, c, dk), 1)
    row_c = lax.broadcasted_iota(jnp.int32, (c, c), 0)
    col_c = lax.broadcasted_iota(jnp.int32, (c, c), 1)

    def seg_min(ref):
        return jnp.min(ref[...].reshape(t // hh, hh, dk).sum(axis=1))
    unsafe = jnp.minimum(seg_min(lff_ref), seg_min(lfb_ref)) < HGRN_SAFE_LOG_DECAY

    def tile_step(ti, _):
        rows = pl.ds(pl.multiple_of(ti * TM, TM), TM)
        lf_f, lf_b = lff_ref[rows, :], lfb_ref[rows, :]
        hi, mid, lo = _split3(jnp.concatenate([lf_f, lf_b], axis=1))
        pre = _dot(cum_mat, hi) + _dot(cum_mat, mid) + _dot(cum_mat, lo)
        as4 = lambda a: a.reshape(cpt, c, dk)
        bc_f = as4(pre[:, :dk])
        pre_b = as4(pre[:, dk:])
        bc_b = pre_b[:, c - 1:c, :] - pre_b + as4(lf_b)
        q4, v = as4(q_ref[rows, :]), v_ref[rows, :]

        def direction(lf4, bc4, reverse):
            edge, last = (hh, 0) if reverse else (hh - 1, c - 1)
            second = (in_chunk < hh) if reverse else (in_chunk >= hh)
            k4 = 1.0 - jnp.exp(lf4)
            b_last, b_edge = bc4[:, last:last + 1, :], bc4[:, edge:edge + 1, :]
            q_s = q4 * jnp.exp(bc4)
            k_bar = k4 * jnp.exp(b_last - bc4)
            decay = jnp.exp(b_last)
            return k4, b_edge, second, q_s, k_bar, decay

        parts = [direction(as4(lf_f), bc_f, False), direction(as4(lf_b), bc_b, True)]
        flat = lambda a: a.reshape(TM, dk)
        qs[rows, :] = jnp.concatenate([flat(p[3]) for p in parts], axis=1).astype(BF16)
        k_bar2 = jnp.concatenate([flat(p[4]) for p in parts], axis=1).astype(BF16)
        decay2 = jnp.concatenate([p[5] for p in parts], axis=2)
        for cc in range(cpt):
            ci = ti * cpt + cc
            xs[ci] = _dot_tn(v[cc * c:(cc + 1) * c, :], k_bar2[cc * c:(cc + 1) * c, :])
            ds[ci] = jnp.broadcast_to(decay2[cc], ds.shape[1:])

        @pl.when(jnp.logical_not(unsafe))
        def _():
            a = jnp.zeros((TM, TM), F32)
            for (k4, b_edge, second, _, _, _), bc4, reverse in zip(parts, (bc_f, bc_b), (False, True)):
                tri = (col >= row) if reverse else (col <= row)
                as8 = lambda a_: a_.reshape(2 * cpt, hh, dk)
                bc8 = as8(bc4)
                ref = bc8[:, hh // 2:hh // 2 + 1, :]
                q_d = flat(as8(q4) * jnp.exp(bc8 - ref)).astype(BF16)
                k_d = flat(as8(k4) * jnp.exp(ref - bc8)).astype(BF16)
                e_o = jnp.exp(jnp.where(second, bc4 - b_edge, b_edge - bc4))
                q_o = flat(jnp.where(second, q4 * e_o, 0.0)).astype(BF16)
                k_o = flat(jnp.where(second, 0.0, k4 * e_o)).astype(BF16)
                a = a + jnp.where(same_half & tri, _dot_nt(q_d, k_d), jnp.where(cross_half, _dot_nt(q_o, k_o), 0.0))
            acc[rows, :] = _dot(a.astype(BF16), v)

        @pl.when(unsafe)
        def _():
            for cc in range(cpt):
                a = jnp.zeros((c, c), F32)
                q_c = q4[cc]
                for (k4, _, _, _, _, _), bc4, reverse in zip(parts, (bc_f, bc_b), (False, True)):
                    bc_c = bc4[cc]
                    cum[...] = bc_c
                    kk[...] = k4[cc]

                    def col_step(s, a_):
                        d_s = jnp.exp(jnp.minimum(bc_c - cum[pl.ds(s, 1), :], 0.0))
                        w = jnp.sum(q_c * kk[pl.ds(s, 1), :] * d_s, axis=-1, keepdims=True)
                        return jnp.where(col_c == s, w, a_)
                    a_dir = lax.fori_loop(0, c, col_step, jnp.zeros((c, c), F32))
                    a = a + jnp.where((col_c >= row_c) if reverse else (col_c <= row_c), a_dir, 0.0)
                crow = pl.ds(pl.multiple_of(ti * TM + cc * c, c), c)
                acc[crow, :] = _dot(a.astype(BF16), v[cc * c:(cc + 1) * c, :])
        return 0

    lax.fori_loop(0, t // TM, tile_step, 0)

    def state_step(i, carry):
        s_f, s_b = carry
        cf = i
        cb = jnp.where(i < ctx_chunks, ctx_chunks - 1 - i, n_chunks - 1 + ctx_chunks - i)
        st[cf, :, 0:dk] = s_f.astype(BF16)
        st[cb, :, dk:2 * dk] = s_b.astype(BF16)
        s_f = s_f * ds[cf, 0:1, 0:dk] + xs[cf, :, 0:dk]
        s_b = s_b * ds[cb, 0:1, dk:2 * dk] + xs[cb, :, dk:2 * dk]
        return s_f, s_b

    zero = jnp.zeros((dk, dk), F32)
    lax.fori_loop(0, n_chunks, state_step, (zero, zero))

    def finish_step(ti, _):
        rows = pl.ds(pl.multiple_of(ti * TM, TM), TM)
        inter = [_dot_nt(qs[pl.ds(pl.multiple_of(ti * TM + cc * c, c), c), :], st[ti * cpt + cc]) for cc in range(cpt)]
        o = acc[rows, :] + jnp.concatenate(inter, axis=0)
        ms = jnp.mean(o * o, axis=-1, keepdims=True)
        o_ref[rows, :] = (o * lax.rsqrt(ms + EPS) * ng_ref[...] * _silu(g_ref[rows, :])).astype(BF16)
        return 0

    lax.fori_loop(0, t // TM, finish_step, 0)


def _hgrn_scan(tok, q, lff, lfb, v, g, norm_gain):
    n_rows, d = q.shape
    heads = d // LANES
    t = tok.t
    n_chunks = t // HGRN_CHUNK
    spec = pl.BlockSpec((None, t, LANES), lambda b, h: (b, 0, h))
    as3 = lambda a: a.reshape(tok.batch, t, d)
    out = pl.pallas_call(
        functools.partial(_hgrn_scan_kernel, n_ctx=tok.n_ctx),
        grid=(tok.batch, heads),
        in_specs=[spec] * 5 + [pl.BlockSpec((1, LANES), lambda b, h: (0, 0))],
        out_specs=spec,
        out_shape=jax.ShapeDtypeStruct((tok.batch, t, d), BF16),
        scratch_shapes=[pltpu.VMEM((t, LANES), F32), pltpu.VMEM((t, 2 * LANES), BF16),
                        pltpu.VMEM((n_chunks, LANES, 2 * LANES), F32), pltpu.VMEM((n_chunks, LANES, 2 * LANES), BF16),
                        pltpu.VMEM((n_chunks, SUBLANES, 2 * LANES), F32),
                        pltpu.VMEM((HGRN_CHUNK, LANES), F32), pltpu.VMEM((HGRN_CHUNK, LANES), F32)],
        compiler_params=_cparams("parallel", "parallel"),
        name="hgrn_scan",
    )(as3(q), as3(lff), as3(lfb), as3(v), as3(g), norm_gain.reshape(1, LANES))
    return out.reshape(n_rows, d)


def _ffn_kernel(x_ref, o_ref, wo_ref, bo_ref, ng_ref, g1c, g1l, shc, shl, scc, scl, g2c, g2l, wi_ref, wf_ref, y_ref,
                *, tiles_per_batch, n_ctx):
    rows = x_ref.shape[0]
    if n_ctx > 0:
        row0 = (pl.program_id(0) % tiles_per_batch) * rows
        is_ctx = row0 + lax.broadcasted_iota(jnp.int32, (rows, 1), 0) < n_ctx
        pick = lambda c_ref, l_ref: jnp.where(is_ctx, c_ref[...], l_ref[...])
    else:
        pick = lambda c_ref, l_ref: l_ref[...]
    x1 = x_ref[...] + pick(g1c, g1l) * (_dot(o_ref[...], wo_ref[...]) + bo_ref[...])
    h = _norm_mod(x1, ng_ref[...], pick(shc, shl), pick(scc, scl)).astype(BF16)
    d_ff = wf_ref.shape[0]
    acc = jnp.zeros(x1.shape, F32)
    for c0 in range(0, d_ff, FF_CHUNK):
        gate = _dot(h, wi_ref[:, c0:c0 + FF_CHUNK])
        up = _dot(h, wi_ref[:, d_ff + c0:d_ff + c0 + FF_CHUNK])
        acc = acc + _dot((_silu(gate) * up).astype(BF16), wf_ref[c0:c0 + FF_CHUNK, :])
    y_ref[...] = x1 + pick(g2c, g2l) * acc


def _row_tile(n, cap):
    return max(r for r in range(TM, cap + 1, TM) if n % r == 0)


def _out_proj_ffn(tok, x_all, o_all, mod, layer, w_o, b_o, norm_gain, w_in, w_out, lat_only):
    n_rows, d = x_all.shape
    d_ff = w_out.shape[0]
    assert d_ff % FF_CHUNK == 0
    rows_b = tok.seq if lat_only else tok.t
    tm = _row_tile(math.gcd(rows_b, tok.n_ctx) if lat_only else rows_b, FFN_MAX_ROWS)
    per_b = rows_b // tm
    first = tok.n_ctx // tm if lat_only else 0
    stream_row = pl.BlockSpec((tm, d), lambda i: ((i // per_b) * (tok.t // tm) + first + i % per_b, 0))
    out_row = pl.BlockSpec((tm, d), lambda i: (i, 0))
    o_row = stream_row if o_all.shape[0] == n_rows else out_row

    def mod_specs(chunk):
        ctx = pl.BlockSpec((None, 1, d), lambda i: (layer * COND_ROWS * 6 + chunk, 0, 0))
        lat = pl.BlockSpec((None, 1, d), lambda i: ((layer * COND_ROWS + 1 + i // per_b) * 6 + chunk, 0, 0))
        return [ctx, lat]

    return pl.pallas_call(
        functools.partial(_ffn_kernel, tiles_per_batch=per_b, n_ctx=0 if lat_only else tok.n_ctx),
        grid=(tok.batch * per_b,),
        in_specs=[stream_row, o_row, _resident(w_o.shape), _vec_spec(d), _vec_spec(d)]
                 + mod_specs(2) + mod_specs(3) + mod_specs(4) + mod_specs(5)
                 + [_resident(w_in.shape), _resident(w_out.shape)],
        out_specs=out_row,
        out_shape=jax.ShapeDtypeStruct((tok.batch * rows_b, d), F32),
        compiler_params=_cparams("parallel"),
        name="out_proj_ffn",
    )(x_all, o_all, w_o.astype(BF16), b_o.reshape(1, d), norm_gain.reshape(1, d), *([mod] * 8),
      w_in.astype(BF16), w_out.astype(BF16))


def kernel(x, c, ctx, c_ctx, w_ada, b_ada, norm_gain, ffn_w_in, ffn_w_out, fnet_w_out, fnet_b_out,
           diff_w_in, diff_q_gain, diff_k_gain, diff_lambda, diff_subln_gain, diff_w_out,
           hgrn_w_in, hgrn_lower_bound, hgrn_norm_gain, hgrn_w_out,
           gqa_w_in, gqa_q_gain, gqa_k_gain, gqa_w_out):
    batch, seq, d = x.shape
    n_ctx = ctx.shape[1]
    depth = w_ada.shape[0]
    assert batch + 1 <= COND_ROWS and d % LANES == 0
    tok = _Tokens(batch, n_ctx, seq)

    cond = jnp.zeros((COND_ROWS, d), F32).at[0].set(c_ctx).at[1:1 + batch].set(c)
    mod = _ada_modulation(cond, w_ada, b_ada)
    x_all = jnp.concatenate([ctx, x], axis=1).reshape(batch * tok.t, d)
    zero_bias = jnp.zeros((d,), F32)

    for i in range(depth):
        m, j = i % N_MIXERS, i // N_MIXERS
        need_ctx = i < depth - 1
        if m == 0:
            group_dim = d // 8
            o = _fnet_mix(tok, x_all.reshape(batch, tok.t, d), mod, i, norm_gain[i, 0], group_dim)
            w_o, b_o = fnet_w_out[j], fnet_b_out[j]
        elif m == 1:
            lam_init = 0.8 - 0.6 * math.exp(-0.3 * i)
            qg = diff_q_gain[j].reshape(1, LANES) * (DIFF_HEAD_DIM ** -0.5 * LOG2_E)
            kg = diff_k_gain[j].reshape(1, LANES)
            q, k, v = _qkv_project(tok, x_all, mod, i, norm_gain[i, 0], diff_w_in[j], qg, kg,
                                   DIFF_HEAD_DIM, d, d)
            o = _diff_attention(tok, q, k, v, qg, kg, diff_lambda[j], diff_subln_gain[j], lam_init)
            w_o, b_o = diff_w_out[j], zero_bias
        elif m == 2:
            q, lff, lfb, v, g = _hgrn_project(tok, x_all, mod, i, norm_gain[i, 0], hgrn_w_in[j], hgrn_lower_bound)
            o = _hgrn_scan(tok, q, lff, lfb, v, g, hgrn_norm_gain[j])
            w_o, b_o = hgrn_w_out[j], zero_bias
        else:
            kv = (gqa_w_in.shape[-1] - d) // 2
            qg = gqa_q_gain[j].reshape(1, LANES) * (GQA_HEAD_DIM ** -0.5 * LOG2_E)
            kg = gqa_k_gain[j].reshape(1, LANES)
            q, k, v = _qkv_project(tok, x_all, mod, i, norm_gain[i, 0], gqa_w_in[j], qg, kg, GQA_HEAD_DIM, d, kv)
            o = _gqa_attention(tok, q, k, v, qg, kg, need_ctx)
            w_o, b_o = gqa_w_out[j], zero_bias
        x_all = _out_proj_ffn(tok, x_all, o, mod, i, w_o, b_o, norm_gain[i, 1], ffn_w_in[i], ffn_w_out[i],
                              lat_only=not need_ctx)
    return x_all.reshape(batch, seq, d)
```

```python
import functools
import math

import numpy as np
import jax
import jax.numpy as jnp
from jax import lax
from jax.experimental import pallas as pl
from jax.experimental.pallas import tpu as pltpu

F32 = jnp.float32
BF16 = jnp.bfloat16

EPS = 1e-6
GRID_W = 64
ROPE_THETA = 10000.0
N_MIXERS = 4

LANES = 128
SUBLANES = 8
TM = 256
COND_ROWS = 16
VMEM_LIMIT = 56 * 1024 * 1024

DIFF_HEAD_DIM = 64
GQA_HEAD_DIM = 128
GQA_GROUP = 4
HGRN_CHUNK = 64
HGRN_HALF = 32
HGRN_SAFE_LOG_DECAY = -80.0
FF_CHUNK = 256
FFN_MAX_ROWS = 768
LOG2_E = math.log2(math.e)
EXP2_SAFE_LOGIT = 100.0
BF16_SLACK = 1.02
ATTN_UNROLL = 2


def _cparams(*sem):
    return pltpu.CompilerParams(dimension_semantics=sem, vmem_limit_bytes=VMEM_LIMIT)


def _resident(shape):
    nd = len(shape)
    return pl.BlockSpec(shape, lambda *_: (0,) * nd, pipeline_mode=pl.Buffered(1))


def _silu(x):
    return x * jax.nn.sigmoid(x)


def _norm_mod(x, gain, shift, scale):
    ms = jnp.mean(x * x, axis=-1, keepdims=True)
    y = x * lax.rsqrt(ms + EPS) * gain
    return y * (1.0 + scale) + shift


def _dot(a, b):
    return jnp.dot(a, b, preferred_element_type=F32)


def _dot_nt(a, b):
    return lax.dot_general(a, b, (((1,), (1,)), ((), ())), preferred_element_type=F32)


def _dot_tn(a, b):
    return lax.dot_general(a, b, (((0,), (0,)), ((), ())), preferred_element_type=F32)


def _ada_kernel(cond_ref, w_ref, b_ref, o_ref):
    a = _silu(cond_ref[...]).astype(BF16)
    o_ref[...] = _dot(a, w_ref[...].astype(BF16)) + b_ref[...]


def _ada_modulation(cond, w_ada, b_ada):
    depth, d, n6 = w_ada.shape
    bn = n6 // 4
    out = pl.pallas_call(
        _ada_kernel,
        grid=(depth, n6 // bn),
        in_specs=[
            pl.BlockSpec((COND_ROWS, d), lambda i, j: (0, 0)),
            pl.BlockSpec((None, d, bn), lambda i, j: (i, 0, j)),
            pl.BlockSpec((None, 1, bn), lambda i, j: (i, 0, j)),
        ],
        out_specs=pl.BlockSpec((None, COND_ROWS, bn), lambda i, j: (i, 0, j)),
        out_shape=jax.ShapeDtypeStruct((depth, COND_ROWS, n6), F32),
        compiler_params=_cparams("parallel", "parallel"),
        name="ada_modulation",
    )(cond, w_ada, b_ada.reshape(depth, 1, n6))
    return out.reshape(depth * COND_ROWS * 6, 1, d)


class _Tokens:
    def __init__(self, batch, n_ctx, seq):
        assert n_ctx % TM == 0 and seq % TM == 0
        self.batch, self.n_ctx, self.seq = batch, n_ctx, seq
        self.t = n_ctx + seq
        self.ctx_tiles = n_ctx // TM
        self.tiles = self.t // TM

    def mod_spec(self, d, layer, chunk, tile_of):
        def index(i):
            b, r = tile_of(i)
            row = jnp.where(r < self.ctx_tiles, 0, 1 + b)
            return ((layer * COND_ROWS + row) * 6 + chunk, 0, 0)
        return pl.BlockSpec((None, 1, d), index)


def _vec_spec(n):
    return pl.BlockSpec((1, n), lambda *_: (0, 0))


def _dft_tables(n_ctx, seq, group_dim):
    def cs(n):
        k = np.arange(n, dtype=np.int64)
        ang = 2.0 * np.pi * ((k[:, None] * k[None, :]) % n).astype(np.float64) / n
        return np.cos(ang), np.sin(ang)
    cc, sc = cs(group_dim)
    chan = np.concatenate([cc, sc], axis=1)
    cl, sl = cs(seq)
    pos_lat = np.concatenate([cl, -sl], axis=1)
    cx, sx = cs(n_ctx)
    pos_ctx = np.concatenate([cx, -sx], axis=1)
    as_bf16 = lambda a: jnp.asarray(a.astype(np.float32)).astype(BF16)
    return as_bf16(chan), as_bf16(pos_ctx), as_bf16(pos_lat)


def _fnet_kernel(x_ref, ng_ref, shc_ref, scc_ref, shl_ref, scl_ref, chan_ref, pctx_ref, plat_ref,
                 y_ref, ab_ctx, ab_lat, *, n_ctx, seq, group_dim):
    j = pl.program_id(1)
    ctx_tiles = n_ctx // TM
    d = x_ref.shape[-1]

    @pl.when(j == 0)
    def _():
        for r in range((n_ctx + seq) // TM):
            is_ctx = r < ctx_tiles
            sh, sc = (shc_ref, scc_ref) if is_ctx else (shl_ref, scl_ref)
            h = _norm_mod(x_ref[r * TM:(r + 1) * TM, :], ng_ref[...], sh[...], sc[...]).astype(BF16)
            dst, n, row0 = (ab_ctx, n_ctx, r * TM) if is_ctx else (ab_lat, seq, r * TM - n_ctx)
            for g in range(d // group_dim):
                cols = slice(g * group_dim, (g + 1) * group_dim)
                ab = _dot(h[:, cols], chan_ref[...])
                dst[row0:row0 + TM, cols] = ab[:, :group_dim].astype(BF16)
                dst[n + row0:n + row0 + TM, cols] = ab[:, group_dim:].astype(BF16)

    @pl.when(j < ctx_tiles)
    def _():
        rows = pl.ds(pl.multiple_of(j * TM, TM), TM)
        y = _dot(pctx_ref[rows, :], ab_ctx[...]) * (1.0 / math.sqrt(n_ctx * group_dim))
        y_ref[...] = y.astype(BF16)

    @pl.when(j >= ctx_tiles)
    def _():
        y = _dot(plat_ref[...], ab_lat[...]) * (1.0 / math.sqrt(seq * group_dim))
        y_ref[...] = y.astype(BF16)


def _fnet_mix(tok, x_all, mod, layer, norm_gain, group_dim):
    b, t, d = tok.batch, tok.t, x_all.shape[-1]
    chan, pos_ctx, pos_lat = _dft_tables(tok.n_ctx, tok.seq, group_dim)
    ct = tok.ctx_tiles

    def mod_spec(chunk, ctx):
        return pl.BlockSpec((None, 1, d), lambda i, j: ((layer * COND_ROWS + (0 if ctx else 1 + i)) * 6 + chunk, 0, 0))

    kern = functools.partial(_fnet_kernel, n_ctx=tok.n_ctx, seq=tok.seq, group_dim=group_dim)
    return pl.pallas_call(
        kern,
        grid=(b, tok.tiles),
        in_specs=[
            pl.BlockSpec((None, t, d), lambda i, j: (i, 0, 0)),
            pl.BlockSpec((1, d), lambda i, j: (0, 0)),
            mod_spec(0, True), mod_spec(1, True), mod_spec(0, False), mod_spec(1, False),
            _resident(chan.shape),
            _resident(pos_ctx.shape),
            pl.BlockSpec((TM, 2 * tok.seq), lambda i, j: (jnp.maximum(j - ct, 0), 0)),
        ],
        out_specs=pl.BlockSpec((TM, d), lambda i, j: (i * tok.tiles + j, 0)),
        out_shape=jax.ShapeDtypeStruct((b * t, d), BF16),
        scratch_shapes=[pltpu.VMEM((2 * tok.n_ctx, d), BF16), pltpu.VMEM((2 * tok.seq, d), BF16)],
        compiler_params=_cparams("parallel", "arbitrary"),
        name="fnet_mix",
    )(x_all, norm_gain.reshape(1, d), mod, mod, mod, mod, chan, pos_ctx, pos_lat)


def _rope_tables(tok, head_dim):
    rows = tok.seq // GRID_W
    row = jnp.repeat(jnp.arange(rows, dtype=F32), GRID_W)
    col = jnp.tile(jnp.arange(GRID_W, dtype=F32), rows)
    n_freq = head_dim // 4
    inv_freq = ROPE_THETA ** (-jnp.arange(n_freq, dtype=F32) / n_freq)
    ang = jnp.concatenate([row[:, None] * inv_freq, col[:, None] * inv_freq], axis=-1)
    cos = jnp.concatenate([jnp.ones((tok.n_ctx, head_dim // 2), F32), jnp.cos(ang)], axis=0)
    sin = jnp.concatenate([jnp.zeros((tok.n_ctx, head_dim // 2), F32), jnp.sin(ang)], axis=0)
    reps = LANES // head_dim
    cos = jnp.tile(jnp.concatenate([cos, cos], axis=-1), (1, reps))
    sin = jnp.tile(jnp.concatenate([-sin, sin], axis=-1), (1, reps))
    return cos, sin


def _group_mean_matrix(group):
    g = np.arange(LANES) // group
    return jnp.asarray((g[:, None] == g[None, :]).astype(np.float32) / group).astype(BF16)


def _head_norm_rope(y, gain, gmean, cos, sin, head_dim):
    ms = _dot((y * y).astype(BF16), gmean)
    yn = y * lax.rsqrt(ms + EPS) * gain
    half = head_dim // 2
    if head_dim == LANES:
        partner = pltpu.roll(yn, half, 1)
    else:
        lane = lax.broadcasted_iota(jnp.int32, yn.shape, 1)
        partner = jnp.where(lane % head_dim < half, pltpu.roll(yn, LANES - half, 1), pltpu.roll(yn, half, 1))
    return yn * cos + partner * sin


def _qkv_kernel(x_ref, ng_ref, sh_ref, sc_ref, w_ref, qg_ref, kg_ref, cos_ref, sin_ref, gm_ref,
                q_ref, k_ref, v_ref, *, head_dim):
    h = _norm_mod(x_ref[...], ng_ref[...], sh_ref[...], sc_ref[...]).astype(BF16)
    cos, sin, gm = cos_ref[...], sin_ref[...], gm_ref[...]
    nq, nk = q_ref.shape[-1], k_ref.shape[-1]
    for o_ref, g_ref, col0, n in ((q_ref, qg_ref, 0, nq), (k_ref, kg_ref, nq, nk)):
        y = _dot(h, w_ref[:, col0:col0 + n])
        for blk in range(n // LANES):
            cols = slice(blk * LANES, (blk + 1) * LANES)
            o_ref[:, cols] = _head_norm_rope(y[:, cols], g_ref[...], gm, cos, sin, head_dim).astype(BF16)
    v_ref[...] = _dot(h, w_ref[:, nq + nk:]).astype(BF16)


def _qkv_project(tok, x_all, mod, layer, norm_gain, w_in, q_gain, k_gain, head_dim, nq, nk):
    n_rows, d = x_all.shape
    nv = w_in.shape[1] - nq - nk
    cos, sin = _rope_tables(tok, head_dim)
    mod_spec = lambda chunk: tok.mod_spec(d, layer, chunk, lambda i: (i // tok.tiles, i % tok.tiles))
    row_spec = lambda n: pl.BlockSpec((TM, n), lambda i: (i, 0))
    tab_spec = pl.BlockSpec((TM, LANES), lambda i: (i % tok.tiles, 0))
    return pl.pallas_call(
        functools.partial(_qkv_kernel, head_dim=head_dim),
        grid=(n_rows // TM,),
        in_specs=[row_spec(d), _vec_spec(d), mod_spec(0), mod_spec(1), _resident(w_in.shape),
                  _vec_spec(LANES), _vec_spec(LANES), tab_spec, tab_spec, _resident((LANES, LANES))],
        out_specs=[row_spec(nq), row_spec(nk), row_spec(nv)],
        out_shape=[jax.ShapeDtypeStruct((n_rows, n), BF16) for n in (nq, nk, nv)],
        compiler_params=_cparams("parallel"),
        name=f"qkv_project_hd{head_dim}",
    )(x_all, norm_gain.reshape(1, d), mod, mod, w_in.astype(BF16), q_gain, k_gain, cos, sin,
      _group_mean_matrix(head_dim))


def _softmax_parts(q, k):
    s = _dot_nt(q, k)
    p = jnp.exp2(s - jnp.max(s, axis=-1, keepdims=True))
    return p, jnp.sum(p, axis=-1, keepdims=True)


def _softmax_parts_bounded(k, q, vt):
    p = jnp.exp2(_dot_nt(k, q))
    l = jnp.sum(p, axis=0, keepdims=True)
    return _dot(vt, p.astype(BF16)) * (1.0 / l)


def _logit_bound(qg_ref, kg_ref, head_dim):
    return BF16_SLACK * head_dim * jnp.max(jnp.abs(qg_ref[...])) * jnp.max(jnp.abs(kg_ref[...]))


def _diff_attn_kernel(lp_ref, sg_ref, qg_ref, kg_ref, q_ref, k_ref, v_ref, o_ref, vt, *, lam_init, n_ctx):
    lp = lp_ref[...]
    lam = (jnp.exp(jnp.sum(lp[0:1] * lp[1:2], axis=-1, keepdims=True))
           - jnp.exp(jnp.sum(lp[2:3] * lp[3:4], axis=-1, keepdims=True)) + lam_init)
    t = k_ref.shape[0]
    lane = lax.broadcasted_iota(jnp.int32, (TM, LANES), 1)
    bounded = _logit_bound(qg_ref, kg_ref, DIFF_HEAD_DIM) <= EXP2_SAFE_LOGIT

    def split_q(rows):
        q = q_ref[rows, :]
        zero = jnp.zeros_like(q)
        return jnp.where(lane < DIFF_HEAD_DIM, q, zero), jnp.where(lane >= DIFF_HEAD_DIM, q, zero)

    def tile_bounded(rows, nk):
        o12 = _softmax_parts_bounded(k_ref[0:nk, :], jnp.concatenate(split_q(rows), axis=0), vt[:, 0:nk])
        ot = o12[:, :TM] - lam * o12[:, TM:]
        ms = jnp.mean(ot * ot, axis=0, keepdims=True)
        o_ref[rows, :] = ((ot * lax.rsqrt(ms + EPS)).T * sg_ref[...]).astype(BF16)

    def tile_exact(rows, nk):
        k, v = k_ref[0:nk, :], v_ref[0:nk, :]
        q1, q2 = split_q(rows)
        p1, l1 = _softmax_parts(q1, k)
        p2, l2 = _softmax_parts(q2, k)
        w = p1 * (1.0 / l1) - p2 * (lam / l2)
        o = _dot(w.astype(BF16), v)
        ms = jnp.mean(o * o, axis=-1, keepdims=True)
        o_ref[rows, :] = (o * lax.rsqrt(ms + EPS) * sg_ref[...]).astype(BF16)

    @pl.when(bounded)
    def _():
        vt[...] = v_ref[...].T
        _for_query_tiles(lambda rows, nk: tile_bounded(rows, nk), n_ctx, t, 0, ATTN_UNROLL)

    @pl.when(jnp.logical_not(bounded))
    def _():
        _for_query_tiles(lambda rows, nk: tile_exact(rows, nk), n_ctx, t, 0, 1)


def _for_query_tiles(tile, n_ctx, t, first_row, unroll):
    for r0 in range(first_row, n_ctx, TM):
        tile(pl.ds(r0, TM), n_ctx)
    lat0 = max(first_row, n_ctx)
    n_lat = (t - lat0) // TM
    unroll = unroll if n_lat % unroll == 0 else 1

    def group(i, _):
        for u in range(unroll):
            tile(pl.ds(pl.multiple_of(lat0 + (i * unroll + u) * TM, TM), TM), t)
        return 0

    lax.fori_loop(0, n_lat // unroll, group, 0)


def _diff_attention(tok, q, k, v, q_gain, k_gain, lam_par, subln_gain, lam_init):
    n_rows, d = q.shape
    heads = d // LANES
    t = tok.t
    as3 = lambda a: a.reshape(tok.batch, t, d)
    spec = pl.BlockSpec((None, t, LANES), lambda b, h: (b, 0, h))
    vec = pl.BlockSpec((1, LANES), lambda b, h: (0, 0))
    out = pl.pallas_call(
        functools.partial(_diff_attn_kernel, lam_init=lam_init, n_ctx=tok.n_ctx),
        grid=(tok.batch, heads),
        in_specs=[pl.BlockSpec(lam_par.shape, lambda b, h: (0, 0)), vec, vec, vec, spec, spec, spec],
        out_specs=spec,
        out_shape=jax.ShapeDtypeStruct((tok.batch, t, d), BF16),
        scratch_shapes=[pltpu.VMEM((LANES, t), BF16)],
        compiler_params=_cparams("parallel", "parallel"),
        name="diff_attention",
    )(lam_par, (subln_gain * (1.0 - lam_init)).reshape(1, LANES), q_gain, k_gain, as3(q), as3(k), as3(v))
    return out.reshape(n_rows, d)


def _gqa_attn_kernel(qg_ref, kg_ref, q_ref, k_ref, v_ref, o_ref, vt, *, n_ctx, first_row):
    t = k_ref.shape[0]
    bounded = _logit_bound(qg_ref, kg_ref, GQA_HEAD_DIM) <= EXP2_SAFE_LOGIT

    def out_rows(rows):
        if first_row == 0:
            return rows
        start = rows.start - first_row
        return pl.ds(start if isinstance(start, int) else pl.multiple_of(start, TM), TM)

    def tile_bounded(rows, nk):
        k, vt_k = k_ref[0:nk, :], vt[:, 0:nk]
        for g in range(0, GQA_GROUP, 2):
            q2h = jnp.concatenate([q_ref[rows, g * LANES:(g + 1) * LANES],
                                   q_ref[rows, (g + 1) * LANES:(g + 2) * LANES]], axis=0)
            o2h = _softmax_parts_bounded(k, q2h, vt_k)
            o_ref[out_rows(rows), g * LANES:(g + 1) * LANES] = o2h[:, :TM].T.astype(BF16)
            o_ref[out_rows(rows), (g + 1) * LANES:(g + 2) * LANES] = o2h[:, TM:].T.astype(BF16)

    def tile_exact(rows, nk):
        k, v = k_ref[0:nk, :], v_ref[0:nk, :]
        for g in range(GQA_GROUP):
            cols = slice(g * LANES, (g + 1) * LANES)
            p, l = _softmax_parts(q_ref[rows, cols], k)
            o_ref[out_rows(rows), cols] = (_dot(p.astype(BF16), v) * (1.0 / l)).astype(BF16)

    @pl.when(bounded)
    def _():
        vt[...] = v_ref[...].T
        _for_query_tiles(tile_bounded, n_ctx, t, first_row, 1)

    @pl.when(jnp.logical_not(bounded))
    def _():
        _for_query_tiles(tile_exact, n_ctx, t, first_row, 1)


def _gqa_attention(tok, q, k, v, q_gain, k_gain, need_ctx):
    d = q.shape[1]
    kv_heads = k.shape[1] // LANES
    t = tok.t
    first_row = 0 if need_ctx else tok.n_ctx
    as3 = lambda a: a.reshape(tok.batch, t, a.shape[1])
    kv_spec = pl.BlockSpec((None, t, LANES), lambda b, h: (b, 0, h))
    q_spec = pl.BlockSpec((None, t, GQA_GROUP * LANES), lambda b, h: (b, 0, h))
    o_spec = pl.BlockSpec((None, t - first_row, GQA_GROUP * LANES), lambda b, h: (b, 0, h))
    vec = pl.BlockSpec((1, LANES), lambda b, h: (0, 0))
    out = pl.pallas_call(
        functools.partial(_gqa_attn_kernel, n_ctx=tok.n_ctx, first_row=first_row),
        grid=(tok.batch, kv_heads),
        in_specs=[vec, vec, q_spec, kv_spec, kv_spec],
        out_specs=o_spec,
        out_shape=jax.ShapeDtypeStruct((tok.batch, t - first_row, d), BF16),
        scratch_shapes=[pltpu.VMEM((LANES, t), BF16)],
        compiler_params=_cparams("parallel", "parallel"),
        name="gqa_attention",
    )(q_gain, k_gain, as3(q), as3(k), as3(v))
    return out.reshape(tok.batch * (t - first_row), d)


def _hgrn_proj_kernel(x_ref, ng_ref, sh_ref, sc_ref, w_ref, lb_ref, q_ref, lff_ref, lfb_ref, v_ref, g_ref,
                      *, layer_idx, depth):
    h = _norm_mod(x_ref[...], ng_ref[...], sh_ref[...], sc_ref[...]).astype(BF16)
    d = x_ref.shape[-1]
    q_ref[...] = _silu(_dot(h, w_ref[:, 0:d]))
    for direction, o_ref in enumerate((lff_ref, lfb_ref)):
        rows = [lb_ref[direction * depth + i:direction * depth + i + 1, :] for i in range(depth)]
        m = functools.reduce(jnp.maximum, rows)
        e = [jnp.exp(r - m) for r in rows]
        lb = sum(e[1:layer_idx + 1]) / sum(e) if layer_idx > 0 else jnp.zeros_like(m)
        z = _dot(h, w_ref[:, (1 + direction) * d:(2 + direction) * d])
        o_ref[...] = jnp.log(lb + (1.0 - lb) * jax.nn.sigmoid(z))
    v_ref[...] = _dot(h, w_ref[:, 3 * d:4 * d]).astype(BF16)
    g_ref[...] = _dot(h, w_ref[:, 4 * d:5 * d])


def _hgrn_project(tok, x_all, mod, layer, norm_gain, w_in, lower_bound):
    n_rows, d = x_all.shape
    depth = lower_bound.shape[1]
    mod_spec = lambda chunk: tok.mod_spec(d, layer, chunk, lambda i: (i // tok.tiles, i % tok.tiles))
    row_spec = pl.BlockSpec((TM, d), lambda i: (i, 0))
    return pl.pallas_call(
        functools.partial(_hgrn_proj_kernel, layer_idx=layer, depth=depth),
        grid=(n_rows // TM,),
        in_specs=[row_spec, _vec_spec(d), mod_spec(0), mod_spec(1), _resident(w_in.shape),
                  pl.BlockSpec((2 * depth, d), lambda i: (0, 0))],
        out_specs=[row_spec] * 5,
        out_shape=[jax.ShapeDtypeStruct((n_rows, d), dt) for dt in (F32, F32, F32, BF16, F32)],
        compiler_params=_cparams("parallel"),
        name="hgrn_project",
    )(x_all, norm_gain.reshape(1, d), mod, mod, w_in.astype(BF16), lower_bound.reshape(2 * depth, d))


def _split3(x):
    hi = x.astype(BF16)
    r1 = x - hi.astype(F32)
    mid = r1.astype(BF16)
    lo = (r1 - mid.astype(F32)).astype(BF16)
    return hi, mid, lo


def _hgrn_scan_kernel(q_ref, lff_ref, lfb_ref, v_ref, g_ref, ng_ref, o_ref,
                      acc, qs, xs, st, ds, cum, kk, *, n_ctx):
    t, dk = q_ref.shape
    c, hh = HGRN_CHUNK, HGRN_HALF
    n_chunks, ctx_chunks = t // c, n_ctx // c
    cpt = TM // c
    row = lax.broadcasted_iota(jnp.int32, (TM, TM), 0)
    col = lax.broadcasted_iota(jnp.int32, (TM, TM), 1)
    same_chunk = (row // c) == (col // c)
    same_half = (row // hh) == (col // hh)
    cross_half = same_chunk & jnp.logical_not(same_half)
    cum_mat = jnp.where(same_chunk & (col <= row), 1.0, 0.0).astype(BF16)
    in_chunk = lax.broadcasted_iota(jnp.int32, (cpt, c, dk), 1)
    row_c = lax.broadcasted_iota(jnp.int32, (c, c), 0)
    col_c = lax.broadcasted_iota(jnp.int32, (c, c), 1)

    def seg_min(ref):
        return jnp.min(ref[...].reshape(t // hh, hh, dk).sum(axis=1))
    unsafe = jnp.minimum(seg_min(lff_ref), seg_min(lfb_ref)) < HGRN_SAFE_LOG_DECAY

    as4 = lambda a: a.reshape(cpt, c, dk)
    flat = lambda a: a.reshape(TM, dk)
    n_tiles = t // TM
    unroll = next(u for u in (3, 2, 1) if n_tiles % u == 0)

    def cumulative(ti):
        rows = pl.ds(pl.multiple_of(ti * TM, TM), TM)
        lf_f, lf_b = lff_ref[rows, :], lfb_ref[rows, :]
        hi, mid, lo = _split3(jnp.concatenate([lf_f, lf_b], axis=1))
        pre = _dot(cum_mat, hi) + _dot(cum_mat, mid) + _dot(cum_mat, lo)
        pre_b = as4(pre[:, dk:])
        bc_b = pre_b[:, c - 1:c, :] - pre_b + as4(lf_b)
        return rows, (as4(lf_f), as4(lf_b)), (as4(pre[:, :dk]), bc_b)

    def tile_step(ti):
        rows, lfs, bcs = cumulative(ti)
        q4, v = as4(q_ref[rows, :]), v_ref[rows, :]
        a = jnp.zeros((TM, TM), F32)
        q_s, k_bar, decay = [], [], []
        for lf4, bc4, reverse in zip(lfs, bcs, (False, True)):
            edge, last = (hh, 0) if reverse else (hh - 1, c - 1)
            second = (in_chunk < hh) if reverse else (in_chunk >= hh)
            tri = (col >= row) if reverse else (col <= row)
            k4 = 1.0 - jnp.exp(lf4)
            b_last, b_edge = bc4[:, last:last + 1, :], bc4[:, edge:edge + 1, :]
            q_s.append(flat(q4 * jnp.exp(bc4)))
            k_bar.append(flat(k4 * jnp.exp(b_last - bc4)))
            decay.append(jnp.exp(b_last))
            as8 = lambda a_: a_.reshape(2 * cpt, hh, dk)
            bc8 = as8(bc4)
            ref = bc8[:, hh // 2:hh // 2 + 1, :]
            q_d = flat(as8(q4) * jnp.exp(bc8 - ref)).astype(BF16)
            k_d = flat(as8(k4) * jnp.exp(ref - bc8)).astype(BF16)
            e_o = jnp.exp(jnp.where(second, bc4 - b_edge, b_edge - bc4))
            q_o = flat(jnp.where(second, q4 * e_o, 0.0)).astype(BF16)
            k_o = flat(jnp.where(second, 0.0, k4 * e_o)).astype(BF16)
            a = a + jnp.where(same_half & tri, _dot_nt(q_d, k_d), jnp.where(cross_half, _dot_nt(q_o, k_o), 0.0))
        acc[rows, :] = _dot(a.astype(BF16), v)
        qs[rows, :] = jnp.concatenate(q_s, axis=1).astype(BF16)
        k_bar2 = jnp.concatenate(k_bar, axis=1).astype(BF16)
        decay2 = jnp.concatenate(decay, axis=2)
        for cc in range(cpt):
            ci = ti * cpt + cc
            xs[ci] = _dot_tn(v[cc * c:(cc + 1) * c, :], k_bar2[cc * c:(cc + 1) * c, :])
            ds[ci] = jnp.broadcast_to(decay2[cc], ds.shape[1:])

    def tile_exact(ti, _):
        rows, lfs, bcs = cumulative(ti)
        q4, v = as4(q_ref[rows, :]), v_ref[rows, :]
        for cc in range(cpt):
            a = jnp.zeros((c, c), F32)
            q_c = q4[cc]
            for lf4, bc4, reverse in zip(lfs, bcs, (False, True)):
                bc_c = bc4[cc]
                cum[...] = bc_c
                kk[...] = 1.0 - jnp.exp(lf4[cc])

                def col_step(s, a_):
                    d_s = jnp.exp(jnp.minimum(bc_c - cum[pl.ds(s, 1), :], 0.0))
                    w = jnp.sum(q_c * kk[pl.ds(s, 1), :] * d_s, axis=-1, keepdims=True)
                    return jnp.where(col_c == s, w, a_)
                a_dir = lax.fori_loop(0, c, col_step, jnp.zeros((c, c), F32))
                a = a + jnp.where((col_c >= row_c) if reverse else (col_c <= row_c), a_dir, 0.0)
            crow = pl.ds(pl.multiple_of(ti * TM + cc * c, c), c)
            acc[crow, :] = _dot(a.astype(BF16), v[cc * c:(cc + 1) * c, :])
        return 0

    def tile_group(i, _):
        for u in range(unroll):
            tile_step(i * unroll + u)
        return 0

    lax.fori_loop(0, n_tiles // unroll, tile_group, 0)

    @pl.when(unsafe)
    def _():
        lax.fori_loop(0, n_tiles, tile_exact, 0)

    def state_step(i, carry):
        s_f, s_b = carry
        cf = i
        cb = jnp.where(i < ctx_chunks, ctx_chunks - 1 - i, n_chunks - 1 + ctx_chunks - i)
        st[cf, :, 0:dk] = s_f.astype(BF16)
        st[cb, :, dk:2 * dk] = s_b.astype(BF16)
        s_f = s_f * ds[cf, 0:1, 0:dk] + xs[cf, :, 0:dk]
        s_b = s_b * ds[cb, 0:1, dk:2 * dk] + xs[cb, :, dk:2 * dk]
        return s_f, s_b

    zero = jnp.zeros((dk, dk), F32)
    lax.fori_loop(0, n_chunks, state_step, (zero, zero))

    def finish_step(ti):
        rows = pl.ds(pl.multiple_of(ti * TM, TM), TM)
        inter = [_dot_nt(qs[pl.ds(pl.multiple_of(ti * TM + cc * c, c), c), :], st[ti * cpt + cc]) for cc in range(cpt)]
        o = acc[rows, :] + jnp.concatenate(inter, axis=0)
        ms = jnp.mean(o * o, axis=-1, keepdims=True)
        o_ref[rows, :] = (o * lax.rsqrt(ms + EPS) * ng_ref[...] * _silu(g_ref[rows, :])).astype(BF16)

    def finish_group(i, _):
        for u in range(unroll):
            finish_step(i * unroll + u)
        return 0

    lax.fori_loop(0, n_tiles // unroll, finish_group, 0)


def _hgrn_scan(tok, q, lff, lfb, v, g, norm_gain):
    n_rows, d = q.shape
    heads = d // LANES
    t = tok.t
    n_chunks = t // HGRN_CHUNK
    spec = pl.BlockSpec((None, t, LANES), lambda b, h: (b, 0, h))
    as3 = lambda a: a.reshape(tok.batch, t, d)
    out = pl.pallas_call(
        functools.partial(_hgrn_scan_kernel, n_ctx=tok.n_ctx),
        grid=(tok.batch, heads),
        in_specs=[spec] * 5 + [pl.BlockSpec((1, LANES), lambda b, h: (0, 0))],
        out_specs=spec,
        out_shape=jax.ShapeDtypeStruct((tok.batch, t, d), BF16),
        scratch_shapes=[pltpu.VMEM((t, LANES), F32), pltpu.VMEM((t, 2 * LANES), BF16),
                        pltpu.VMEM((n_chunks, LANES, 2 * LANES), F32), pltpu.VMEM((n_chunks, LANES, 2 * LANES), BF16),
                        pltpu.VMEM((n_chunks, SUBLANES, 2 * LANES), F32),
                        pltpu.VMEM((HGRN_CHUNK, LANES), F32), pltpu.VMEM((HGRN_CHUNK, LANES), F32)],
        compiler_params=_cparams("parallel", "parallel"),
        name="hgrn_scan",
    )(as3(q), as3(lff), as3(lfb), as3(v), as3(g), norm_gain.reshape(1, LANES))
    return out.reshape(n_rows, d)


def _ffn_kernel(x_ref, o_ref, wo_ref, bo_ref, ng_ref, g1c, g1l, shc, shl, scc, scl, g2c, g2l, wi_ref, wf_ref, y_ref,
                *, tiles_per_batch, n_ctx):
    rows = x_ref.shape[0]
    if n_ctx > 0:
        row0 = (pl.program_id(0) % tiles_per_batch) * rows
        is_ctx = row0 + lax.broadcasted_iota(jnp.int32, (rows, 1), 0) < n_ctx
        pick = lambda c_ref, l_ref: jnp.where(is_ctx, c_ref[...], l_ref[...])
    else:
        pick = lambda c_ref, l_ref: l_ref[...]
    x1 = x_ref[...] + pick(g1c, g1l) * (_dot(o_ref[...], wo_ref[...]) + bo_ref[...])
    h = _norm_mod(x1, ng_ref[...], pick(shc, shl), pick(scc, scl)).astype(BF16)
    d_ff = wf_ref.shape[0]
    acc = jnp.zeros(x1.shape, F32)
    for c0 in range(0, d_ff, FF_CHUNK):
        gate = _dot(h, wi_ref[:, c0:c0 + FF_CHUNK])
        up = _dot(h, wi_ref[:, d_ff + c0:d_ff + c0 + FF_CHUNK])
        acc = acc + _dot((_silu(gate) * up).astype(BF16), wf_ref[c0:c0 + FF_CHUNK, :])
    y_ref[...] = x1 + pick(g2c, g2l) * acc


def _row_tile(n, cap):
    return max(r for r in range(TM, cap + 1, TM) if n % r == 0)


def _out_proj_ffn(tok, x_all, o_all, mod, layer, w_o, b_o, norm_gain, w_in, w_out, lat_only):
    n_rows, d = x_all.shape
    d_ff = w_out.shape[0]
    assert d_ff % FF_CHUNK == 0
    rows_b = tok.seq if lat_only else tok.t
    tm = _row_tile(math.gcd(rows_b, tok.n_ctx) if lat_only else rows_b, FFN_MAX_ROWS)
    per_b = rows_b // tm
    first = tok.n_ctx // tm if lat_only else 0
    stream_row = pl.BlockSpec((tm, d), lambda i: ((i // per_b) * (tok.t // tm) + first + i % per_b, 0))
    out_row = pl.BlockSpec((tm, d), lambda i: (i, 0))
    o_row = stream_row if o_all.shape[0] == n_rows else out_row

    def mod_specs(chunk):
        ctx = pl.BlockSpec((None, 1, d), lambda i: (layer * COND_ROWS * 6 + chunk, 0, 0))
        lat = pl.BlockSpec((None, 1, d), lambda i: ((layer * COND_ROWS + 1 + i // per_b) * 6 + chunk, 0, 0))
        return [ctx, lat]

    return pl.pallas_call(
        functools.partial(_ffn_kernel, tiles_per_batch=per_b, n_ctx=0 if lat_only else tok.n_ctx),
        grid=(tok.batch * per_b,),
        in_specs=[stream_row, o_row, _resident(w_o.shape), _vec_spec(d), _vec_spec(d)]
                 + mod_specs(2) + mod_specs(3) + mod_specs(4) + mod_specs(5)
                 + [_resident(w_in.shape), _resident(w_out.shape)],
        out_specs=out_row,
        out_shape=jax.ShapeDtypeStruct((tok.batch * rows_b, d), F32),
        compiler_params=_cparams("parallel"),
        name="out_proj_ffn",
    )(x_all, o_all, w_o.astype(BF16), b_o.reshape(1, d), norm_gain.reshape(1, d), *([mod] * 8),
      w_in.astype(BF16), w_out.astype(BF16))


def kernel(x, c, ctx, c_ctx, w_ada, b_ada, norm_gain, ffn_w_in, ffn_w_out, fnet_w_out, fnet_b_out,
           diff_w_in, diff_q_gain, diff_k_gain, diff_lambda, diff_subln_gain, diff_w_out,
           hgrn_w_in, hgrn_lower_bound, hgrn_norm_gain, hgrn_w_out,
           gqa_w_in, gqa_q_gain, gqa_k_gain, gqa_w_out):
    batch, seq, d = x.shape
    n_ctx = ctx.shape[1]
    depth = w_ada.shape[0]
    assert batch + 1 <= COND_ROWS and d % LANES == 0
    tok = _Tokens(batch, n_ctx, seq)

    cond = jnp.zeros((COND_ROWS, d), F32).at[0].set(c_ctx).at[1:1 + batch].set(c)
    mod = _ada_modulation(cond, w_ada, b_ada)
    x_all = jnp.concatenate([ctx, x], axis=1).reshape(batch * tok.t, d)
    zero_bias = jnp.zeros((d,), F32)

    for i in range(depth):
        m, j = i % N_MIXERS, i // N_MIXERS
        need_ctx = i < depth - 1
        if m == 0:
            group_dim = d // 8
            o = _fnet_mix(tok, x_all.reshape(batch, tok.t, d), mod, i, norm_gain[i, 0], group_dim)
            w_o, b_o = fnet_w_out[j], fnet_b_out[j]
        elif m == 1:
            lam_init = 0.8 - 0.6 * math.exp(-0.3 * i)
            qg = diff_q_gain[j].reshape(1, LANES) * (DIFF_HEAD_DIM ** -0.5 * LOG2_E)
            kg = diff_k_gain[j].reshape(1, LANES)
            q, k, v = _qkv_project(tok, x_all, mod, i, norm_gain[i, 0], diff_w_in[j], qg, kg,
                                   DIFF_HEAD_DIM, d, d)
            o = _diff_attention(tok, q, k, v, qg, kg, diff_lambda[j], diff_subln_gain[j], lam_init)
            w_o, b_o = diff_w_out[j], zero_bias
        elif m == 2:
            q, lff, lfb, v, g = _hgrn_project(tok, x_all, mod, i, norm_gain[i, 0], hgrn_w_in[j], hgrn_lower_bound)
            o = _hgrn_scan(tok, q, lff, lfb, v, g, hgrn_norm_gain[j])
            w_o, b_o = hgrn_w_out[j], zero_bias
        else:
            kv = (gqa_w_in.shape[-1] - d) // 2
            qg = gqa_q_gain[j].reshape(1, LANES) * (GQA_HEAD_DIM ** -0.5 * LOG2_E)
            kg = gqa_k_gain[j].reshape(1, LANES)
            q, k, v = _qkv_project(tok, x_all, mod, i, norm_gain[i, 0], gqa_w_in[j], qg, kg, GQA_HEAD_DIM, d, kv)
            o = _gqa_attention(tok, q, k, v, qg, kg, need_ctx)
            w_o, b_o = gqa_w_out[j], zero_bias
        x_all = _out_proj_ffn(tok, x_all, o, mod, i, w_o, b_o, norm_gain[i, 1], ffn_w_in[i], ffn_w_out[i],
                              lat_only=not need_ctx)
    return x_all.reshape(batch, seq, d)
```

```python
import functools
import math

import numpy as np
import jax
import jax.numpy as jnp
from jax import lax
from jax.experimental import pallas as pl
from jax.experimental.pallas import tpu as pltpu

F32 = jnp.float32
BF16 = jnp.bfloat16

EPS = 1e-6
GRID_W = 64
ROPE_THETA = 10000.0
N_MIXERS = 4

LANES = 128
SUBLANES = 8
TM = 256
COND_ROWS = 16
VMEM_LIMIT = 56 * 1024 * 1024

DIFF_HEAD_DIM = 64
GQA_HEAD_DIM = 128
GQA_GROUP = 4
HGRN_CHUNK = 64
HGRN_HALF = 32
HGRN_SAFE_LOG2_DECAY = -115.0
FF_CHUNK = 256
FFN_MAX_ROWS = 768
PROJ_MAX_ROWS = 768
HGRN_PROJ_MAX_ROWS = 256
MXU_DIM = 256
LOG2_E = math.log2(math.e)
EXP2_SAFE_LOGIT = 100.0
BF16_SLACK = 1.02
ATTN_UNROLL = 2


def _cparams(*sem):
    return pltpu.CompilerParams(dimension_semantics=sem, vmem_limit_bytes=VMEM_LIMIT)


def _resident(shape):
    nd = len(shape)
    return pl.BlockSpec(shape, lambda *_: (0,) * nd, pipeline_mode=pl.Buffered(1))


def _silu(x):
    return x * jax.nn.sigmoid(x)


def _norm_mod(x, gain, shift, scale):
    ms = jnp.mean(x * x, axis=-1, keepdims=True)
    y = x * lax.rsqrt(ms + EPS) * gain
    return y * (1.0 + scale) + shift


def _dot(a, b):
    return jnp.dot(a, b, preferred_element_type=F32)


def _dot_nt(a, b):
    return lax.dot_general(a, b, (((1,), (1,)), ((), ())), preferred_element_type=F32)


def _dot_tn(a, b):
    return lax.dot_general(a, b, (((0,), (0,)), ((), ())), preferred_element_type=F32)


def _ada_kernel(cond_ref, w_ref, b_ref, o_ref):
    a = _silu(cond_ref[...]).astype(BF16)
    o_ref[...] = _dot(a, w_ref[...].astype(BF16)) + b_ref[...]


def _ada_modulation(cond, w_ada, b_ada):
    depth, d, n6 = w_ada.shape
    bn = n6 // 4
    out = pl.pallas_call(
        _ada_kernel,
        grid=(depth, n6 // bn),
        in_specs=[
            pl.BlockSpec((COND_ROWS, d), lambda i, j: (0, 0)),
            pl.BlockSpec((None, d, bn), lambda i, j: (i, 0, j)),
            pl.BlockSpec((None, 1, bn), lambda i, j: (i, 0, j)),
        ],
        out_specs=pl.BlockSpec((None, COND_ROWS, bn), lambda i, j: (i, 0, j)),
        out_shape=jax.ShapeDtypeStruct((depth, COND_ROWS, n6), F32),
        compiler_params=_cparams("parallel", "parallel"),
        name="ada_modulation",
    )(cond, w_ada, b_ada.reshape(depth, 1, n6))
    return out.reshape(depth * COND_ROWS * 6, 1, d)


class _Tokens:
    def __init__(self, batch, n_ctx, seq):
        assert n_ctx % TM == 0 and seq % TM == 0
        self.batch, self.n_ctx, self.seq = batch, n_ctx, seq
        self.t = n_ctx + seq
        self.ctx_tiles = n_ctx // TM
        self.tiles = self.t // TM

    def mod_specs(self, d, layer, chunk, tiles_per_batch):
        ctx = pl.BlockSpec((None, 1, d), lambda i: (layer * COND_ROWS * 6 + chunk, 0, 0))
        lat = pl.BlockSpec((None, 1, d), lambda i: ((layer * COND_ROWS + 1 + i // tiles_per_batch) * 6 + chunk, 0, 0))
        return [ctx, lat]


def _ctx_picker(rows, tiles_per_batch, n_ctx):
    if n_ctx == 0:
        return lambda c_ref, l_ref: l_ref[...]
    row0 = (pl.program_id(0) % tiles_per_batch) * rows
    is_ctx = row0 + lax.broadcasted_iota(jnp.int32, (rows, 1), 0) < n_ctx
    return lambda c_ref, l_ref: jnp.where(is_ctx, c_ref[...], l_ref[...])


def _row_tile(n, cap, step=TM):
    return max(r for r in range(step, cap + 1, step) if n % r == 0)


def _vec_spec(n):
    return pl.BlockSpec((1, n), lambda *_: (0, 0))


def _dft_tables(n_ctx, seq, group_dim):
    def cs(n):
        k = np.arange(n, dtype=np.int64)
        ang = 2.0 * np.pi * ((k[:, None] * k[None, :]) % n).astype(np.float64) / n
        return np.cos(ang), np.sin(ang)
    cc, sc = cs(group_dim)
    chan = np.concatenate([cc, sc], axis=1)
    cl, sl = cs(seq)
    pos_lat = np.concatenate([cl, -sl], axis=1)
    cx, sx = cs(n_ctx)
    pos_ctx = np.concatenate([cx, -sx], axis=1)
    as_bf16 = lambda a: jnp.asarray(a.astype(np.float32)).astype(BF16)
    return as_bf16(chan), as_bf16(pos_ctx), as_bf16(pos_lat)


def _fnet_kernel(x_ref, ng_ref, shc_ref, scc_ref, shl_ref, scl_ref, chan_ref, pctx_ref, plat_ref,
                 y_ref, ab_ctx, ab_lat, *, n_ctx, seq, group_dim):
    j = pl.program_id(1)
    ctx_tiles = n_ctx // TM
    d = x_ref.shape[-1]

    @pl.when(j == 0)
    def _():
        for r in range((n_ctx + seq) // TM):
            is_ctx = r < ctx_tiles
            sh, sc = (shc_ref, scc_ref) if is_ctx else (shl_ref, scl_ref)
            h = _norm_mod(x_ref[r * TM:(r + 1) * TM, :], ng_ref[...], sh[...], sc[...]).astype(BF16)
            dst, n, row0 = (ab_ctx, n_ctx, r * TM) if is_ctx else (ab_lat, seq, r * TM - n_ctx)
            for g in range(d // group_dim):
                cols = slice(g * group_dim, (g + 1) * group_dim)
                ab = _dot(h[:, cols], chan_ref[...])
                dst[row0:row0 + TM, cols] = ab[:, :group_dim].astype(BF16)
                dst[n + row0:n + row0 + TM, cols] = ab[:, group_dim:].astype(BF16)

    @pl.when(j < ctx_tiles)
    def _():
        rows = pl.ds(pl.multiple_of(j * TM, TM), TM)
        y = _dot(pctx_ref[rows, :], ab_ctx[...]) * (1.0 / math.sqrt(n_ctx * group_dim))
        y_ref[...] = y.astype(BF16)

    @pl.when(j >= ctx_tiles)
    def _():
        y = _dot(plat_ref[...], ab_lat[...]) * (1.0 / math.sqrt(seq * group_dim))
        y_ref[...] = y.astype(BF16)


def _fnet_mix(tok, x_all, mod, layer, norm_gain, group_dim):
    b, t, d = tok.batch, tok.t, x_all.shape[-1]
    chan, pos_ctx, pos_lat = _dft_tables(tok.n_ctx, tok.seq, group_dim)
    ct = tok.ctx_tiles

    def mod_spec(chunk, ctx):
        return pl.BlockSpec((None, 1, d), lambda i, j: ((layer * COND_ROWS + (0 if ctx else 1 + i)) * 6 + chunk, 0, 0))

    kern = functools.partial(_fnet_kernel, n_ctx=tok.n_ctx, seq=tok.seq, group_dim=group_dim)
    return pl.pallas_call(
        kern,
        grid=(b, tok.tiles),
        in_specs=[
            pl.BlockSpec((None, t, d), lambda i, j: (i, 0, 0)),
            pl.BlockSpec((1, d), lambda i, j: (0, 0)),
            mod_spec(0, True), mod_spec(1, True), mod_spec(0, False), mod_spec(1, False),
            _resident(chan.shape),
            _resident(pos_ctx.shape),
            pl.BlockSpec((TM, 2 * tok.seq), lambda i, j: (jnp.maximum(j - ct, 0), 0)),
        ],
        out_specs=pl.BlockSpec((TM, d), lambda i, j: (i * tok.tiles + j, 0)),
        out_shape=jax.ShapeDtypeStruct((b * t, d), BF16),
        scratch_shapes=[pltpu.VMEM((2 * tok.n_ctx, d), BF16), pltpu.VMEM((2 * tok.seq, d), BF16)],
        compiler_params=_cparams("parallel", "arbitrary"),
        name="fnet_mix",
    )(x_all, norm_gain.reshape(1, d), mod, mod, mod, mod, chan, pos_ctx, pos_lat)


def _rope_tables(tok, head_dim):
    rows = tok.seq // GRID_W
    row = jnp.repeat(jnp.arange(rows, dtype=F32), GRID_W)
    col = jnp.tile(jnp.arange(GRID_W, dtype=F32), rows)
    n_freq = head_dim // 4
    inv_freq = ROPE_THETA ** (-jnp.arange(n_freq, dtype=F32) / n_freq)
    ang = jnp.concatenate([row[:, None] * inv_freq, col[:, None] * inv_freq], axis=-1)
    cos = jnp.concatenate([jnp.ones((tok.n_ctx, head_dim // 2), F32), jnp.cos(ang)], axis=0)
    sin = jnp.concatenate([jnp.zeros((tok.n_ctx, head_dim // 2), F32), jnp.sin(ang)], axis=0)
    reps = LANES // head_dim
    cos = jnp.tile(jnp.concatenate([cos, cos], axis=-1), (1, reps))
    sin = jnp.tile(jnp.concatenate([-sin, sin], axis=-1), (1, reps))
    return cos, sin


def _group_mean_matrix(group, width):
    g = np.arange(width) // group
    return jnp.asarray((g[:, None] == g[None, :]).astype(np.float32) / group).astype(BF16)


def _rope(yn, cos, sin, head_dim):
    half = head_dim // 2
    if head_dim == LANES:
        partner = pltpu.roll(yn, half, 1)
    else:
        lane = lax.broadcasted_iota(jnp.int32, yn.shape, 1)
        partner = jnp.where(lane % head_dim < half, pltpu.roll(yn, LANES - half, 1), pltpu.roll(yn, half, 1))
    return yn * cos + partner * sin


def _qkv_kernel(x_ref, ng_ref, shc, shl, scc, scl, w_ref, qg_ref, kg_ref, cos_ref, sin_ref, gm_ref,
                q_ref, k_ref, v_ref, *, head_dim, tiles_per_batch, n_ctx):
    pick = _ctx_picker(x_ref.shape[0], tiles_per_batch, n_ctx)
    h = _norm_mod(x_ref[...], ng_ref[...], pick(shc, shl), pick(scc, scl)).astype(BF16)
    cos, sin, gm = cos_ref[...], sin_ref[...], gm_ref[...]
    nq, nk = q_ref.shape[-1], k_ref.shape[-1]
    for o_ref, g_ref, col0, n in ((q_ref, qg_ref, 0, nq), (k_ref, kg_ref, nq, nk)):
        y = _dot(h, w_ref[:, col0:col0 + n])
        for c0 in range(0, n, MXU_DIM):
            yb = y[:, c0:c0 + MXU_DIM]
            yn = yb * lax.rsqrt(_dot((yb * yb).astype(BF16), gm) + EPS) * g_ref[...]
            for l0 in range(0, MXU_DIM, LANES):
                o_ref[:, c0 + l0:c0 + l0 + LANES] = _rope(yn[:, l0:l0 + LANES], cos, sin, head_dim).astype(BF16)
    v_ref[...] = _dot(h, w_ref[:, nq + nk:]).astype(BF16)


def _qkv_project(tok, x_all, mod, layer, norm_gain, w_in, q_gain, k_gain, head_dim, nq, nk):
    n_rows, d = x_all.shape
    nv = w_in.shape[1] - nq - nk
    assert nq % MXU_DIM == 0 and nk % MXU_DIM == 0
    cos, sin = _rope_tables(tok, head_dim)
    tm = _row_tile(tok.t, PROJ_MAX_ROWS)
    per_b = tok.t // tm
    row_spec = lambda n: pl.BlockSpec((tm, n), lambda i: (i, 0))
    tab_spec = pl.BlockSpec((tm, LANES), lambda i: (i % per_b, 0))
    gain2 = lambda g: jnp.tile(g, (1, MXU_DIM // LANES))
    return pl.pallas_call(
        functools.partial(_qkv_kernel, head_dim=head_dim, tiles_per_batch=per_b, n_ctx=tok.n_ctx),
        grid=(n_rows // tm,),
        in_specs=[row_spec(d), _vec_spec(d)] + tok.mod_specs(d, layer, 0, per_b) + tok.mod_specs(d, layer, 1, per_b)
                 + [_resident(w_in.shape), _vec_spec(MXU_DIM), _vec_spec(MXU_DIM), tab_spec, tab_spec,
                    _resident((MXU_DIM, MXU_DIM))],
        out_specs=[row_spec(nq), row_spec(nk), row_spec(nv)],
        out_shape=[jax.ShapeDtypeStruct((n_rows, n), BF16) for n in (nq, nk, nv)],
        compiler_params=_cparams("parallel"),
        name=f"qkv_project_hd{head_dim}",
    )(x_all, norm_gain.reshape(1, d), mod, mod, mod, mod, w_in.astype(BF16), gain2(q_gain), gain2(k_gain),
      cos, sin, _group_mean_matrix(head_dim, MXU_DIM))


def _softmax_parts(q, k):
    s = _dot_nt(q, k)
    p = jnp.exp2(s - jnp.max(s, axis=-1, keepdims=True))
    return p, jnp.sum(p, axis=-1, keepdims=True)


def _softmax_parts_bounded(k, q, vt):
    p = jnp.exp2(_dot_nt(k, q))
    l = jnp.sum(p, axis=0, keepdims=True)
    return _dot(vt, p.astype(BF16)) * (1.0 / l)


def _logit_bound(qg_ref, kg_ref, head_dim):
    return BF16_SLACK * head_dim * jnp.max(jnp.abs(qg_ref[...])) * jnp.max(jnp.abs(kg_ref[...]))


def _diff_attn_kernel(lp_ref, sg_ref, qg_ref, kg_ref, q_ref, k_ref, v_ref, o_ref, vt, *, lam_init, n_ctx):
    lp = lp_ref[...]
    lam = (jnp.exp(jnp.sum(lp[0:1] * lp[1:2], axis=-1, keepdims=True))
           - jnp.exp(jnp.sum(lp[2:3] * lp[3:4], axis=-1, keepdims=True)) + lam_init)
    t = k_ref.shape[0]
    lane = lax.broadcasted_iota(jnp.int32, (TM, LANES), 1)
    bounded = _logit_bound(qg_ref, kg_ref, DIFF_HEAD_DIM) <= EXP2_SAFE_LOGIT

    def split_q(rows):
        q = q_ref[rows, :]
        zero = jnp.zeros_like(q)
        return jnp.where(lane < DIFF_HEAD_DIM, q, zero), jnp.where(lane >= DIFF_HEAD_DIM, q, zero)

    def tile_bounded(rows, nk):
        o12 = _softmax_parts_bounded(k_ref[0:nk, :], jnp.concatenate(split_q(rows), axis=0), vt[:, 0:nk])
        ot = o12[:, :TM] - lam * o12[:, TM:]
        ms = jnp.mean(ot * ot, axis=0, keepdims=True)
        o_ref[rows, :] = ((ot * lax.rsqrt(ms + EPS)).T * sg_ref[...]).astype(BF16)

    def tile_exact(rows, nk):
        k, v = k_ref[0:nk, :], v_ref[0:nk, :]
        q1, q2 = split_q(rows)
        p1, l1 = _softmax_parts(q1, k)
        p2, l2 = _softmax_parts(q2, k)
        w = p1 * (1.0 / l1) - p2 * (lam / l2)
        o = _dot(w.astype(BF16), v)
        ms = jnp.mean(o * o, axis=-1, keepdims=True)
        o_ref[rows, :] = (o * lax.rsqrt(ms + EPS) * sg_ref[...]).astype(BF16)

    @pl.when(bounded)
    def _():
        vt[...] = v_ref[...].T
        _for_query_tiles(lambda rows, nk: tile_bounded(rows, nk), n_ctx, t, 0, ATTN_UNROLL)

    @pl.when(jnp.logical_not(bounded))
    def _():
        _for_query_tiles(lambda rows, nk: tile_exact(rows, nk), n_ctx, t, 0, 1)


def _for_query_tiles(tile, n_ctx, t, first_row, unroll):
    for r0 in range(first_row, n_ctx, TM):
        tile(pl.ds(r0, TM), n_ctx)
    lat0 = max(first_row, n_ctx)
    n_lat = (t - lat0) // TM
    unroll = unroll if n_lat % unroll == 0 else 1

    def group(i, _):
        for u in range(unroll):
            tile(pl.ds(pl.multiple_of(lat0 + (i * unroll + u) * TM, TM), TM), t)
        return 0

    lax.fori_loop(0, n_lat // unroll, group, 0)


def _diff_attention(tok, q, k, v, q_gain, k_gain, lam_par, subln_gain, lam_init):
    n_rows, d = q.shape
    heads = d // LANES
    t = tok.t
    as3 = lambda a: a.reshape(tok.batch, t, d)
    spec = pl.BlockSpec((None, t, LANES), lambda b, h: (b, 0, h))
    vec = pl.BlockSpec((1, LANES), lambda b, h: (0, 0))
    out = pl.pallas_call(
        functools.partial(_diff_attn_kernel, lam_init=lam_init, n_ctx=tok.n_ctx),
        grid=(tok.batch, heads),
        in_specs=[pl.BlockSpec(lam_par.shape, lambda b, h: (0, 0)), vec, vec, vec, spec, spec, spec],
        out_specs=spec,
        out_shape=jax.ShapeDtypeStruct((tok.batch, t, d), BF16),
        scratch_shapes=[pltpu.VMEM((LANES, t), BF16)],
        compiler_params=_cparams("parallel", "parallel"),
        name="diff_attention",
    )(lam_par, (subln_gain * (1.0 - lam_init)).reshape(1, LANES), q_gain, k_gain, as3(q), as3(k), as3(v))
    return out.reshape(n_rows, d)


def _gqa_attn_kernel(qg_ref, kg_ref, q_ref, k_ref, v_ref, o_ref, vt, *, n_ctx, first_row):
    t = k_ref.shape[0]
    bounded = _logit_bound(qg_ref, kg_ref, GQA_HEAD_DIM) <= EXP2_SAFE_LOGIT

    def out_rows(rows):
        if first_row == 0:
            return rows
        start = rows.start - first_row
        return pl.ds(start if isinstance(start, int) else pl.multiple_of(start, TM), TM)

    def tile_bounded(rows, nk):
        k, vt_k = k_ref[0:nk, :], vt[:, 0:nk]
        for g in range(0, GQA_GROUP, 2):
            q2h = jnp.concatenate([q_ref[rows, g * LANES:(g + 1) * LANES],
                                   q_ref[rows, (g + 1) * LANES:(g + 2) * LANES]], axis=0)
            o2h = _softmax_parts_bounded(k, q2h, vt_k)
            o_ref[out_rows(rows), g * LANES:(g + 1) * LANES] = o2h[:, :TM].T.astype(BF16)
            o_ref[out_rows(rows), (g + 1) * LANES:(g + 2) * LANES] = o2h[:, TM:].T.astype(BF16)

    def tile_exact(rows, nk):
        k, v = k_ref[0:nk, :], v_ref[0:nk, :]
        for g in range(GQA_GROUP):
            cols = slice(g * LANES, (g + 1) * LANES)
            p, l = _softmax_parts(q_ref[rows, cols], k)
            o_ref[out_rows(rows), cols] = (_dot(p.astype(BF16), v) * (1.0 / l)).astype(BF16)

    @pl.when(bounded)
    def _():
        vt[...] = v_ref[...].T
        _for_query_tiles(tile_bounded, n_ctx, t, first_row, 1)

    @pl.when(jnp.logical_not(bounded))
    def _():
        _for_query_tiles(tile_exact, n_ctx, t, first_row, 1)


def _gqa_attention(tok, q, k, v, q_gain, k_gain, need_ctx):
    d = q.shape[1]
    kv_heads = k.shape[1] // LANES
    t = tok.t
    first_row = 0 if need_ctx else tok.n_ctx
    as3 = lambda a: a.reshape(tok.batch, t, a.shape[1])
    kv_spec = pl.BlockSpec((None, t, LANES), lambda b, h: (b, 0, h))
    q_spec = pl.BlockSpec((None, t, GQA_GROUP * LANES), lambda b, h: (b, 0, h))
    o_spec = pl.BlockSpec((None, t - first_row, GQA_GROUP * LANES), lambda b, h: (b, 0, h))
    vec = pl.BlockSpec((1, LANES), lambda b, h: (0, 0))
    out = pl.pallas_call(
        functools.partial(_gqa_attn_kernel, n_ctx=tok.n_ctx, first_row=first_row),
        grid=(tok.batch, kv_heads),
        in_specs=[vec, vec, q_spec, kv_spec, kv_spec],
        out_specs=o_spec,
        out_shape=jax.ShapeDtypeStruct((tok.batch, t - first_row, d), BF16),
        scratch_shapes=[pltpu.VMEM((LANES, t), BF16)],
        compiler_params=_cparams("parallel", "parallel"),
        name="gqa_attention",
    )(q_gain, k_gain, as3(q), as3(k), as3(v))
    return out.reshape(tok.batch * (t - first_row), d)


def _hgrn_proj_kernel(x_ref, ng_ref, shc, shl, scc, scl, w_ref, lb_ref, q_ref, lff_ref, lfb_ref, v_ref, g_ref,
                      *, layer_idx, depth, tiles_per_batch, n_ctx):
    pick = _ctx_picker(x_ref.shape[0], tiles_per_batch, n_ctx)
    h = _norm_mod(x_ref[...], ng_ref[...], pick(shc, shl), pick(scc, scl)).astype(BF16)
    d = x_ref.shape[-1]
    q_ref[...] = _silu(_dot(h, w_ref[:, 0:d]))
    for direction, o_ref in enumerate((lff_ref, lfb_ref)):
        rows = [lb_ref[direction * depth + i:direction * depth + i + 1, :] for i in range(depth)]
        m = functools.reduce(jnp.maximum, rows)
        e = [jnp.exp(r - m) for r in rows]
        lb = sum(e[1:layer_idx + 1]) / sum(e) if layer_idx > 0 else jnp.zeros_like(m)
        z = _dot(h, w_ref[:, (1 + direction) * d:(2 + direction) * d])
        o_ref[...] = jnp.log2(lb + (1.0 - lb) * jax.nn.sigmoid(z))
    v_ref[...] = _dot(h, w_ref[:, 3 * d:4 * d]).astype(BF16)
    g_ref[...] = _dot(h, w_ref[:, 4 * d:5 * d])


def _hgrn_project(tok, x_all, mod, layer, norm_gain, w_in, lower_bound):
    n_rows, d = x_all.shape
    depth = lower_bound.shape[1]
    tm = _row_tile(tok.t, HGRN_PROJ_MAX_ROWS, step=HGRN_CHUNK)
    per_b = tok.t // tm
    row_spec = pl.BlockSpec((tm, d), lambda i: (i, 0))
    return pl.pallas_call(
        functools.partial(_hgrn_proj_kernel, layer_idx=layer, depth=depth, tiles_per_batch=per_b, n_ctx=tok.n_ctx),
        grid=(n_rows // tm,),
        in_specs=[row_spec, _vec_spec(d)] + tok.mod_specs(d, layer, 0, per_b) + tok.mod_specs(d, layer, 1, per_b)
                 + [_resident(w_in.shape), pl.BlockSpec((2 * depth, d), lambda i: (0, 0))],
        out_specs=[row_spec] * 5,
        out_shape=[jax.ShapeDtypeStruct((n_rows, d), dt) for dt in (F32, F32, F32, BF16, F32)],
        compiler_params=_cparams("parallel"),
        name="hgrn_project",
    )(x_all, norm_gain.reshape(1, d), mod, mod, mod, mod, w_in.astype(BF16), lower_bound.reshape(2 * depth, d))


def _split2(x):
    hi = x.astype(BF16)
    return hi, (x - hi.astype(F32)).astype(BF16)


def _hgrn_scan_kernel(q_ref, lff_ref, lfb_ref, v_ref, g_ref, ng_ref, o_ref,
                      acc, qs, xs, st, ds, cum, kk, *, n_ctx):
    t, dk = q_ref.shape
    c, hh = HGRN_CHUNK, HGRN_HALF
    n_chunks, ctx_chunks = t // c, n_ctx // c
    cpt = TM // c
    row = lax.broadcasted_iota(jnp.int32, (TM, TM), 0)
    col = lax.broadcasted_iota(jnp.int32, (TM, TM), 1)
    same_chunk = (row // c) == (col // c)
    same_half = (row // hh) == (col // hh)
    cross_half = same_chunk & jnp.logical_not(same_half)
    cum_mat = jnp.where(same_chunk & (col <= row), 1.0, 0.0).astype(BF16)
    in_chunk = lax.broadcasted_iota(jnp.int32, (cpt, c, dk), 1)
    row_c = lax.broadcasted_iota(jnp.int32, (c, c), 0)
    col_c = lax.broadcasted_iota(jnp.int32, (c, c), 1)

    def seg_min(ref):
        return jnp.min(ref[...].reshape(t // hh, hh, dk).sum(axis=1))
    unsafe = jnp.minimum(seg_min(lff_ref), seg_min(lfb_ref)) < HGRN_SAFE_LOG2_DECAY

    as4 = lambda a: a.reshape(cpt, c, dk)
    flat = lambda a: a.reshape(TM, dk)
    n_tiles = t // TM
    unroll = next(u for u in (3, 2, 1) if n_tiles % u == 0)

    def cumulative(ti):
        rows = pl.ds(pl.multiple_of(ti * TM, TM), TM)
        lf_f, lf_b = lff_ref[rows, :], lfb_ref[rows, :]
        hi, lo = _split2(jnp.concatenate([lf_f, lf_b], axis=1))
        pre = _dot(cum_mat, hi) + _dot(cum_mat, lo)
        pre_b = as4(pre[:, dk:])
        bc_b = pre_b[:, c - 1:c, :] - pre_b + as4(lf_b)
        return rows, (as4(lf_f), as4(lf_b)), (as4(pre[:, :dk]), bc_b)

    def tile_step(ti):
        rows, lfs, bcs = cumulative(ti)
        q4, v = as4(q_ref[rows, :]), v_ref[rows, :]
        a_same = jnp.zeros((TM, TM), F32)
        q_s, k_bar, decay, q_o, k_o = [], [], [], [], []
        for lf4, bc4, reverse in zip(lfs, bcs, (False, True)):
            edge, last = (hh, 0) if reverse else (hh - 1, c - 1)
            second = (in_chunk < hh) if reverse else (in_chunk >= hh)
            tri = (col >= row) if reverse else (col <= row)
            k4 = 1.0 - jnp.exp2(lf4)
            b_last, b_edge = bc4[:, last:last + 1, :], bc4[:, edge:edge + 1, :]
            q_s.append(flat(q4 * jnp.exp2(bc4)))
            k_bar.append(flat(k4 * jnp.exp2(b_last - bc4)))
            decay.append(jnp.exp2(b_last))
            as8 = lambda a_: a_.reshape(2 * cpt, hh, dk)
            bc8 = as8(bc4)
            ref = bc8[:, hh // 2:hh // 2 + 1, :]
            q_d = flat(as8(q4) * jnp.exp2(bc8 - ref)).astype(BF16)
            k_d = flat(as8(k4) * jnp.exp2(ref - bc8)).astype(BF16)
            a_same = a_same + jnp.where(tri, _dot_nt(q_d, k_d), 0.0)
            e_o = jnp.exp2(jnp.where(second, bc4 - b_edge, b_edge - bc4))
            q_o.append(flat(jnp.where(second, q4 * e_o, 0.0)))
            k_o.append(flat(jnp.where(second, 0.0, k4 * e_o)))
        a_cross = _dot_nt(jnp.concatenate(q_o, axis=1).astype(BF16), jnp.concatenate(k_o, axis=1).astype(BF16))
        a = jnp.where(same_half, a_same, jnp.where(cross_half, a_cross, 0.0))
        acc[rows, :] = _dot(a.astype(BF16), v)
        qs[rows, :] = jnp.concatenate(q_s, axis=1).astype(BF16)
        k_bar2 = jnp.concatenate(k_bar, axis=1).astype(BF16)
        decay2 = jnp.concatenate(decay, axis=2)
        for cc in range(cpt):
            ci = ti * cpt + cc
            xs[ci] = _dot_tn(v[cc * c:(cc + 1) * c, :], k_bar2[cc * c:(cc + 1) * c, :])
            ds[ci] = jnp.broadcast_to(decay2[cc], ds.shape[1:])

    def tile_exact(ti, _):
        rows, lfs, bcs = cumulative(ti)
        q4, v = as4(q_ref[rows, :]), v_ref[rows, :]
        for cc in range(cpt):
            a = jnp.zeros((c, c), F32)
            q_c = q4[cc]
            for lf4, bc4, reverse in zip(lfs, bcs, (False, True)):
                bc_c = bc4[cc]
                cum[...] = bc_c
                kk[...] = 1.0 - jnp.exp2(lf4[cc])

                def col_step(s, a_):
                    d_s = jnp.exp2(jnp.minimum(bc_c - cum[pl.ds(s, 1), :], 0.0))
                    w = jnp.sum(q_c * kk[pl.ds(s, 1), :] * d_s, axis=-1, keepdims=True)
                    return jnp.where(col_c == s, w, a_)
                a_dir = lax.fori_loop(0, c, col_step, jnp.zeros((c, c), F32))
                a = a + jnp.where((col_c >= row_c) if reverse else (col_c <= row_c), a_dir, 0.0)
            crow = pl.ds(pl.multiple_of(ti * TM + cc * c, c), c)
            acc[crow, :] = _dot(a.astype(BF16), v[cc * c:(cc + 1) * c, :])
        return 0

    def tile_group(i, _):
        for u in range(unroll):
            tile_step(i * unroll + u)
        return 0

    lax.fori_loop(0, n_tiles // unroll, tile_group, 0)

    @pl.when(unsafe)
    def _():
        lax.fori_loop(0, n_tiles, tile_exact, 0)

    def state_step(i, carry):
        s_f, s_b = carry
        cf = i
        cb = jnp.where(i < ctx_chunks, ctx_chunks - 1 - i, n_chunks - 1 + ctx_chunks - i)
        st[cf, :, 0:dk] = s_f.astype(BF16)
        st[cb, :, dk:2 * dk] = s_b.astype(BF16)
        s_f = s_f * ds[cf, 0:1, 0:dk] + xs[cf, :, 0:dk]
        s_b = s_b * ds[cb, 0:1, dk:2 * dk] + xs[cb, :, dk:2 * dk]
        return s_f, s_b

    zero = jnp.zeros((dk, dk), F32)
    lax.fori_loop(0, n_chunks, state_step, (zero, zero))

    def finish_step(ti):
        rows = pl.ds(pl.multiple_of(ti * TM, TM), TM)
        inter = [_dot_nt(qs[pl.ds(pl.multiple_of(ti * TM + cc * c, c), c), :], st[ti * cpt + cc]) for cc in range(cpt)]
        o = acc[rows, :] + jnp.concatenate(inter, axis=0)
        ms = jnp.mean(o * o, axis=-1, keepdims=True)
        o_ref[rows, :] = (o * lax.rsqrt(ms + EPS) * ng_ref[...] * _silu(g_ref[rows, :])).astype(BF16)

    def finish_group(i, _):
        for u in range(unroll):
            finish_step(i * unroll + u)
        return 0

    lax.fori_loop(0, n_tiles // unroll, finish_group, 0)


def _hgrn_scan(tok, q, lff, lfb, v, g, norm_gain):
    n_rows, d = q.shape
    heads = d // LANES
    t = tok.t
    n_chunks = t // HGRN_CHUNK
    spec = pl.BlockSpec((None, t, LANES), lambda b, h: (b, 0, h))
    as3 = lambda a: a.reshape(tok.batch, t, d)
    out = pl.pallas_call(
        functools.partial(_hgrn_scan_kernel, n_ctx=tok.n_ctx),
        grid=(tok.batch, heads),
        in_specs=[spec] * 5 + [pl.BlockSpec((1, LANES), lambda b, h: (0, 0))],
        out_specs=spec,
        out_shape=jax.ShapeDtypeStruct((tok.batch, t, d), BF16),
        scratch_shapes=[pltpu.VMEM((t, LANES), F32), pltpu.VMEM((t, 2 * LANES), BF16),
                        pltpu.VMEM((n_chunks, LANES, 2 * LANES), F32), pltpu.VMEM((n_chunks, LANES, 2 * LANES), BF16),
                        pltpu.VMEM((n_chunks, SUBLANES, 2 * LANES), F32),
                        pltpu.VMEM((HGRN_CHUNK, LANES), F32), pltpu.VMEM((HGRN_CHUNK, LANES), F32)],
        compiler_params=_cparams("parallel", "parallel"),
        name="hgrn_scan",
    )(as3(q), as3(lff), as3(lfb), as3(v), as3(g), norm_gain.reshape(1, LANES))
    return out.reshape(n_rows, d)


def _ffn_kernel(x_ref, o_ref, wo_ref, bo_ref, ng_ref, g1c, g1l, shc, shl, scc, scl, g2c, g2l, wi_ref, wf_ref, y_ref,
                *, tiles_per_batch, n_ctx):
    pick = _ctx_picker(x_ref.shape[0], tiles_per_batch, n_ctx)
    x1 = x_ref[...] + pick(g1c, g1l) * (_dot(o_ref[...], wo_ref[...]) + bo_ref[...])
    h = _norm_mod(x1, ng_ref[...], pick(shc, shl), pick(scc, scl)).astype(BF16)
    d_ff = wf_ref.shape[0]
    acc = jnp.zeros(x1.shape, F32)
    for c0 in range(0, d_ff, FF_CHUNK):
        gate = _dot(h, wi_ref[:, c0:c0 + FF_CHUNK])
        up = _dot(h, wi_ref[:, d_ff + c0:d_ff + c0 + FF_CHUNK])
        acc = acc + _dot((_silu(gate) * up).astype(BF16), wf_ref[c0:c0 + FF_CHUNK, :])
    y_ref[...] = x1 + pick(g2c, g2l) * acc


def _resident_layer(stacked, layer):
    return pl.BlockSpec((None,) + stacked.shape[1:], lambda *_: (layer, 0, 0), pipeline_mode=pl.Buffered(1))


def _out_proj_ffn(tok, x_all, o_all, mod, layer, w_o, b_o, norm_gain, w_in_all, w_out_all, lat_only):
    n_rows, d = x_all.shape
    d_ff = w_out_all.shape[1]
    assert d_ff % FF_CHUNK == 0
    rows_b = tok.seq if lat_only else tok.t
    tm = _row_tile(math.gcd(rows_b, tok.n_ctx) if lat_only else rows_b, FFN_MAX_ROWS)
    per_b = rows_b // tm
    first = tok.n_ctx // tm if lat_only else 0
    stream_row = pl.BlockSpec((tm, d), lambda i: ((i // per_b) * (tok.t // tm) + first + i % per_b, 0))
    out_row = pl.BlockSpec((tm, d), lambda i: (i, 0))
    o_row = stream_row if o_all.shape[0] == n_rows else out_row
    mods = [s for chunk in (2, 3, 4, 5) for s in tok.mod_specs(d, layer, chunk, per_b)]
    return pl.pallas_call(
        functools.partial(_ffn_kernel, tiles_per_batch=per_b, n_ctx=0 if lat_only else tok.n_ctx),
        grid=(tok.batch * per_b,),
        in_specs=[stream_row, o_row, _resident(w_o.shape), _vec_spec(d), _vec_spec(d)] + mods
                 + [_resident_layer(w_in_all, layer), _resident_layer(w_out_all, layer)],
        out_specs=out_row,
        out_shape=jax.ShapeDtypeStruct((tok.batch * rows_b, d), F32),
        compiler_params=_cparams("parallel"),
        name="out_proj_ffn",
    )(x_all, o_all, w_o.astype(BF16), b_o.reshape(1, d), norm_gain.reshape(1, d), *([mod] * 8),
      w_in_all, w_out_all)


def kernel(x, c, ctx, c_ctx, w_ada, b_ada, norm_gain, ffn_w_in, ffn_w_out, fnet_w_out, fnet_b_out,
           diff_w_in, diff_q_gain, diff_k_gain, diff_lambda, diff_subln_gain, diff_w_out,
           hgrn_w_in, hgrn_lower_bound, hgrn_norm_gain, hgrn_w_out,
           gqa_w_in, gqa_q_gain, gqa_k_gain, gqa_w_out):
    batch, seq, d = x.shape
    n_ctx = ctx.shape[1]
    depth = w_ada.shape[0]
    assert batch + 1 <= COND_ROWS and d % LANES == 0
    tok = _Tokens(batch, n_ctx, seq)

    cond = jnp.zeros((COND_ROWS, d), F32).at[0].set(c_ctx).at[1:1 + batch].set(c)
    mod = _ada_modulation(cond, w_ada, b_ada)
    x_all = jnp.concatenate([ctx, x], axis=1).reshape(batch * tok.t, d)
    zero_bias = jnp.zeros((d,), F32)
    ffn_w_in_bf16, ffn_w_out_bf16 = ffn_w_in.astype(BF16), ffn_w_out.astype(BF16)

    for i in range(depth):
        m, j = i % N_MIXERS, i // N_MIXERS
        need_ctx = i < depth - 1
        if m == 0:
            group_dim = d // 8
            o = _fnet_mix(tok, x_all.reshape(batch, tok.t, d), mod, i, norm_gain[i, 0], group_dim)
            w_o, b_o = fnet_w_out[j], fnet_b_out[j]
        elif m == 1:
            lam_init = 0.8 - 0.6 * math.exp(-0.3 * i)
            qg = diff_q_gain[j].reshape(1, LANES) * (DIFF_HEAD_DIM ** -0.5 * LOG2_E)
            kg = diff_k_gain[j].reshape(1, LANES)
            q, k, v = _qkv_project(tok, x_all, mod, i, norm_gain[i, 0], diff_w_in[j], qg, kg,
                                   DIFF_HEAD_DIM, d, d)
            o = _diff_attention(tok, q, k, v, qg, kg, diff_lambda[j], diff_subln_gain[j], lam_init)
            w_o, b_o = diff_w_out[j], zero_bias
        elif m == 2:
            q, lff, lfb, v, g = _hgrn_project(tok, x_all, mod, i, norm_gain[i, 0], hgrn_w_in[j], hgrn_lower_bound)
            o = _hgrn_scan(tok, q, lff, lfb, v, g, hgrn_norm_gain[j])
            w_o, b_o = hgrn_w_out[j], zero_bias
        else:
            kv = (gqa_w_in.shape[-1] - d) // 2
            qg = gqa_q_gain[j].reshape(1, LANES) * (GQA_HEAD_DIM ** -0.5 * LOG2_E)
            kg = gqa_k_gain[j].reshape(1, LANES)
            q, k, v = _qkv_project(tok, x_all, mod, i, norm_gain[i, 0], gqa_w_in[j], qg, kg, GQA_HEAD_DIM, d, kv)
            o = _gqa_attention(tok, q, k, v, qg, kg, need_ctx)
            w_o, b_o = gqa_w_out[j], zero_bias
        x_all = _out_proj_ffn(tok, x_all, o, mod, i, w_o, b_o, norm_gain[i, 1], ffn_w_in_bf16, ffn_w_out_bf16,
                              lat_only=not need_ctx)
    return x_all.reshape(batch, seq, d)
```

```python
import functools
import math

import numpy as np
import jax
import jax.numpy as jnp
from jax import lax
from jax.experimental import pallas as pl
from jax.experimental.pallas import tpu as pltpu

F32 = jnp.float32
BF16 = jnp.bfloat16

EPS = 1e-6
GRID_W = 64
ROPE_THETA = 10000.0
N_MIXERS = 4

LANES = 128
SUBLANES = 8
TM = 256
COND_ROWS = 16
VMEM_LIMIT = 56 * 1024 * 1024

FNET_GROUPS = 8
DIFF_HEAD_DIM = 64
GQA_HEAD_DIM = 128
GQA_GROUP = 4
HGRN_CHUNK = 64
HGRN_HALF = 32
HGRN_SAFE_LOG2_DECAY = -115.0
FF_CHUNK = 256
FFN_MAX_ROWS = 768
PROJ_MAX_ROWS = 768
HGRN_PROJ_MAX_ROWS = 256
MXU_DIM = 256
LOG2_E = math.log2(math.e)
EXP2_SAFE_LOGIT = 100.0
BF16_SLACK = 1.02
DIFF_ATTN_UNROLL = 4
GQA_ATTN_UNROLL = 2


def _cparams(*sem):
    return pltpu.CompilerParams(dimension_semantics=sem, vmem_limit_bytes=VMEM_LIMIT)


def _resident(shape):
    nd = len(shape)
    return pl.BlockSpec(shape, lambda *_: (0,) * nd, pipeline_mode=pl.Buffered(1))


def _silu(x):
    return x * jax.nn.sigmoid(x)


def _norm_mod(x, gain, shift, scale):
    ms = jnp.mean(x * x, axis=-1, keepdims=True)
    y = x * lax.rsqrt(ms + EPS) * gain
    return y * (1.0 + scale) + shift


def _dot(a, b):
    return jnp.dot(a, b, preferred_element_type=F32)


def _dot_nt(a, b):
    return lax.dot_general(a, b, (((1,), (1,)), ((), ())), preferred_element_type=F32)


def _dot_tn(a, b):
    return lax.dot_general(a, b, (((0,), (0,)), ((), ())), preferred_element_type=F32)


def _ada_kernel(cond_ref, w_ref, b_ref, o_ref):
    a = _silu(cond_ref[...]).astype(BF16)
    o_ref[...] = _dot(a, w_ref[...].astype(BF16)) + b_ref[...]


def _ada_modulation(cond, w_ada, b_ada):
    depth, d, n6 = w_ada.shape
    bn = n6 // 4
    out = pl.pallas_call(
        _ada_kernel,
        grid=(depth, n6 // bn),
        in_specs=[
            pl.BlockSpec((COND_ROWS, d), lambda i, j: (0, 0)),
            pl.BlockSpec((None, d, bn), lambda i, j: (i, 0, j)),
            pl.BlockSpec((None, 1, bn), lambda i, j: (i, 0, j)),
        ],
        out_specs=pl.BlockSpec((None, COND_ROWS, bn), lambda i, j: (i, 0, j)),
        out_shape=jax.ShapeDtypeStruct((depth, COND_ROWS, n6), F32),
        compiler_params=_cparams("parallel", "parallel"),
        name="ada_modulation",
    )(cond, w_ada, b_ada.reshape(depth, 1, n6))
    return out.reshape(depth * COND_ROWS * 6, 1, d)


class _Tokens:
    def __init__(self, batch, n_ctx, seq):
        assert n_ctx % TM == 0 and seq % TM == 0
        self.batch, self.n_ctx, self.seq = batch, n_ctx, seq
        self.t = n_ctx + seq
        self.ctx_tiles = n_ctx // TM
        self.tiles = self.t // TM

    def mod_specs(self, d, layer, chunk, tiles_per_batch):
        ctx = pl.BlockSpec((None, 1, d), lambda i: (layer * COND_ROWS * 6 + chunk, 0, 0))
        lat = pl.BlockSpec((None, 1, d), lambda i: ((layer * COND_ROWS + 1 + i // tiles_per_batch) * 6 + chunk, 0, 0))
        return [ctx, lat]


def _ctx_chunks(rows, tiles_per_batch, n_ctx):
    row0 = (pl.program_id(0) % tiles_per_batch) * rows
    chunks = [(slice(r, r + TM), row0 + r < n_ctx) for r in range(0, rows, TM)]
    if n_ctx == 0:
        return chunks, lambda is_ctx, c_ref, l_ref: l_ref[...]
    return chunks, lambda is_ctx, c_ref, l_ref: jnp.where(is_ctx, c_ref[...], l_ref[...])


def _norm_mod_tile(x_ref, ng_ref, shc, shl, scc, scl, tiles_per_batch, n_ctx):
    chunks, pick = _ctx_chunks(x_ref.shape[0], tiles_per_batch, n_ctx)
    return jnp.concatenate([_norm_mod(x_ref[rs, :], ng_ref[...], pick(ic, shc, shl), pick(ic, scc, scl)).astype(BF16)
                            for rs, ic in chunks], axis=0)


def _row_tile(n, cap, step=TM):
    return max(r for r in range(step, cap + 1, step) if n % r == 0)


def _vec_spec(n):
    return pl.BlockSpec((1, n), lambda *_: (0, 0))


def _dft_tables(n_ctx, seq, group_dim):
    def cs(n):
        k = np.arange(n, dtype=np.int64)
        ang = 2.0 * np.pi * ((k[:, None] * k[None, :]) % n).astype(np.float64) / n
        return np.cos(ang), np.sin(ang)
    cc, sc = cs(group_dim)
    chan = np.concatenate([cc, sc], axis=1)
    cl, sl = cs(seq)
    pos_lat = np.concatenate([cl, -sl], axis=1)
    cx, sx = cs(n_ctx)
    pos_ctx = np.concatenate([cx, -sx], axis=1)
    as_bf16 = lambda a: jnp.asarray(a.astype(np.float32)).astype(BF16)
    return as_bf16(chan), as_bf16(pos_ctx), as_bf16(pos_lat)


def _fnet_kernel(ctx_ref, lat_ref, ng_ref, shc_ref, scc_ref, shl_ref, scl_ref, chan_ref, pctx_ref, plat_ref,
                 y_ref, xs_ref, ab_ctx, ab_lat, *, group_dim):
    j = pl.program_id(1)
    n_ctx, d = ctx_ref.shape
    seq = lat_ref.shape[0]
    ctx_tiles = n_ctx // TM

    @pl.when(j == 0)
    def _():
        for src, sh, sc, dst, n in ((ctx_ref, shc_ref, scc_ref, ab_ctx, n_ctx), (lat_ref, shl_ref, scl_ref, ab_lat, seq)):
            for row0 in range(0, n, TM):
                h = _norm_mod(src[row0:row0 + TM, :], ng_ref[...], sh[...], sc[...]).astype(BF16)
                for g in range(d // group_dim):
                    cols = slice(g * group_dim, (g + 1) * group_dim)
                    ab = _dot(h[:, cols], chan_ref[...])
                    dst[row0:row0 + TM, cols] = ab[:, :group_dim].astype(BF16)
                    dst[n + row0:n + row0 + TM, cols] = ab[:, group_dim:].astype(BF16)

    @pl.when(j < ctx_tiles)
    def _():
        rows = pl.ds(pl.multiple_of(j * TM, TM), TM)
        y = _dot(pctx_ref[rows, :], ab_ctx[...]) * (1.0 / math.sqrt(n_ctx * group_dim))
        y_ref[...] = y.astype(BF16)
        xs_ref[...] = ctx_ref[rows, :]

    @pl.when(j >= ctx_tiles)
    def _():
        y = _dot(plat_ref[...], ab_lat[...]) * (1.0 / math.sqrt(seq * group_dim))
        y_ref[...] = y.astype(BF16)
        xs_ref[...] = lat_ref[pl.ds(pl.multiple_of((j - ctx_tiles) * TM, TM), TM), :]


def _fnet_mix(tok, ctx, x, mod, layer, norm_gain, group_dim):
    b, t, d = tok.batch, tok.t, x.shape[-1]
    chan, pos_ctx, pos_lat = _dft_tables(tok.n_ctx, tok.seq, group_dim)
    ct = tok.ctx_tiles

    def mod_spec(chunk, is_ctx):
        return pl.BlockSpec((None, 1, d), lambda i, j: ((layer * COND_ROWS + (0 if is_ctx else 1 + i)) * 6 + chunk, 0, 0))

    tile_spec = pl.BlockSpec((TM, d), lambda i, j: (i * tok.tiles + j, 0))
    return pl.pallas_call(
        functools.partial(_fnet_kernel, group_dim=group_dim),
        grid=(b, tok.tiles),
        in_specs=[
            pl.BlockSpec((None, tok.n_ctx, d), lambda i, j: (i, 0, 0)),
            pl.BlockSpec((None, tok.seq, d), lambda i, j: (i, 0, 0)),
            pl.BlockSpec((1, d), lambda i, j: (0, 0)),
            mod_spec(0, True), mod_spec(1, True), mod_spec(0, False), mod_spec(1, False),
            _resident(chan.shape),
            _resident(pos_ctx.shape),
            pl.BlockSpec((TM, 2 * tok.seq), lambda i, j: (jnp.maximum(j - ct, 0), 0)),
        ],
        out_specs=[tile_spec, tile_spec],
        out_shape=[jax.ShapeDtypeStruct((b * t, d), BF16), jax.ShapeDtypeStruct((b * t, d), F32)],
        scratch_shapes=[pltpu.VMEM((2 * tok.n_ctx, d), BF16), pltpu.VMEM((2 * tok.seq, d), BF16)],
        compiler_params=_cparams("parallel", "arbitrary"),
        name="fnet_mix",
    )(ctx, x, norm_gain.reshape(1, d), mod, mod, mod, mod, chan, pos_ctx, pos_lat)


def _rope_tables(tok, head_dim):
    rows = tok.seq // GRID_W
    row = jnp.repeat(jnp.arange(rows, dtype=F32), GRID_W)
    col = jnp.tile(jnp.arange(GRID_W, dtype=F32), rows)
    n_freq = head_dim // 4
    inv_freq = ROPE_THETA ** (-jnp.arange(n_freq, dtype=F32) / n_freq)
    ang = jnp.concatenate([row[:, None] * inv_freq, col[:, None] * inv_freq], axis=-1)
    cos = jnp.concatenate([jnp.ones((tok.n_ctx, head_dim // 2), F32), jnp.cos(ang)], axis=0)
    sin = jnp.concatenate([jnp.zeros((tok.n_ctx, head_dim // 2), F32), jnp.sin(ang)], axis=0)
    reps = LANES // head_dim
    cos = jnp.tile(jnp.concatenate([cos, cos], axis=-1), (1, reps))
    sin = jnp.tile(jnp.concatenate([-sin, sin], axis=-1), (1, reps))
    return cos, sin


def _group_mean_matrix(group, width):
    g = np.arange(width) // group
    return jnp.asarray((g[:, None] == g[None, :]).astype(np.float32) / group).astype(BF16)


def _rope(yn, cos, sin, head_dim):
    half = head_dim // 2
    if head_dim == LANES:
        partner = pltpu.roll(yn, half, 1)
    else:
        lane = lax.broadcasted_iota(jnp.int32, yn.shape, 1)
        partner = jnp.where(lane % head_dim < half, pltpu.roll(yn, LANES - half, 1), pltpu.roll(yn, half, 1))
    return yn * cos + partner * sin


def _qkv_kernel(x_ref, ng_ref, shc, shl, scc, scl, w_ref, qg_ref, kg_ref, cos_ref, sin_ref, gm_ref,
                q_ref, k_ref, v_ref, *, head_dim, tiles_per_batch, n_ctx):
    h = _norm_mod_tile(x_ref, ng_ref, shc, shl, scc, scl, tiles_per_batch, n_ctx)
    cos, sin, gm = cos_ref[...], sin_ref[...], gm_ref[...]
    nq, nk = q_ref.shape[-1], k_ref.shape[-1]
    for o_ref, g_ref, col0, n in ((q_ref, qg_ref, 0, nq), (k_ref, kg_ref, nq, nk)):
        y = _dot(h, w_ref[:, col0:col0 + n])
        for c0 in range(0, n, MXU_DIM):
            yb = y[:, c0:c0 + MXU_DIM]
            yn = yb * lax.rsqrt(_dot((yb * yb).astype(BF16), gm) + EPS) * g_ref[...]
            for l0 in range(0, MXU_DIM, LANES):
                o_ref[:, c0 + l0:c0 + l0 + LANES] = _rope(yn[:, l0:l0 + LANES], cos, sin, head_dim).astype(BF16)
    v_ref[...] = _dot(h, w_ref[:, nq + nk:]).astype(BF16)


def _qkv_project(tok, x_all, mod, layer, norm_gain, w_in, q_gain, k_gain, head_dim, nq, nk):
    n_rows, d = x_all.shape
    nv = w_in.shape[1] - nq - nk
    assert nq % MXU_DIM == 0 and nk % MXU_DIM == 0
    cos, sin = _rope_tables(tok, head_dim)
    tm = _row_tile(tok.t, PROJ_MAX_ROWS)
    per_b = tok.t // tm
    row_spec = lambda n: pl.BlockSpec((tm, n), lambda i: (i, 0))
    tab_spec = pl.BlockSpec((tm, LANES), lambda i: (i % per_b, 0))
    gain2 = lambda g: jnp.tile(g, (1, MXU_DIM // LANES))
    return pl.pallas_call(
        functools.partial(_qkv_kernel, head_dim=head_dim, tiles_per_batch=per_b, n_ctx=tok.n_ctx),
        grid=(n_rows // tm,),
        in_specs=[row_spec(d), _vec_spec(d)] + tok.mod_specs(d, layer, 0, per_b) + tok.mod_specs(d, layer, 1, per_b)
                 + [_resident(w_in.shape), _vec_spec(MXU_DIM), _vec_spec(MXU_DIM), tab_spec, tab_spec,
                    _resident((MXU_DIM, MXU_DIM))],
        out_specs=[row_spec(nq), row_spec(nk), row_spec(nv)],
        out_shape=[jax.ShapeDtypeStruct((n_rows, n), BF16) for n in (nq, nk, nv)],
        compiler_params=_cparams("parallel"),
        name=f"qkv_project_hd{head_dim}",
    )(x_all, norm_gain.reshape(1, d), mod, mod, mod, mod, w_in.astype(BF16), gain2(q_gain), gain2(k_gain),
      cos, sin, _group_mean_matrix(head_dim, MXU_DIM))


def _softmax_parts(q, k):
    s = _dot_nt(q, k)
    p = jnp.exp2(s - jnp.max(s, axis=-1, keepdims=True))
    return p, jnp.sum(p, axis=-1, keepdims=True)


def _softmax_parts_bounded(k, q, vt):
    p = jnp.exp2(_dot_nt(k, q))
    l = jnp.sum(p, axis=0, keepdims=True)
    return _dot(vt, p.astype(BF16)) * (1.0 / l)


def _logit_bound(qg_ref, kg_ref, head_dim):
    return BF16_SLACK * head_dim * jnp.max(jnp.abs(qg_ref[...])) * jnp.max(jnp.abs(kg_ref[...]))


def _diff_attn_kernel(lp_ref, sg_ref, qg_ref, kg_ref, q_ref, k_ref, v_ref, o_ref, vt, *, lam_init, n_ctx):
    lp = lp_ref[...]
    lam = (jnp.exp(jnp.sum(lp[0:1] * lp[1:2], axis=-1, keepdims=True))
           - jnp.exp(jnp.sum(lp[2:3] * lp[3:4], axis=-1, keepdims=True)) + lam_init)
    t = k_ref.shape[0]
    lane = lax.broadcasted_iota(jnp.int32, (TM, LANES), 1)
    bounded = _logit_bound(qg_ref, kg_ref, DIFF_HEAD_DIM) <= EXP2_SAFE_LOGIT

    def split_q(rows):
        q = q_ref[rows, :]
        zero = jnp.zeros_like(q)
        return jnp.where(lane < DIFF_HEAD_DIM, q, zero), jnp.where(lane >= DIFF_HEAD_DIM, q, zero)

    def tile_bounded(rows, nk):
        o12 = _softmax_parts_bounded(k_ref[0:nk, :], jnp.concatenate(split_q(rows), axis=0), vt[:, 0:nk])
        ot = o12[:, :TM] - lam * o12[:, TM:]
        ms = jnp.mean(ot * ot, axis=0, keepdims=True)
        o_ref[rows, :] = ((ot * lax.rsqrt(ms + EPS)).T * sg_ref[...]).astype(BF16)

    def tile_exact(rows, nk):
        k, v = k_ref[0:nk, :], v_ref[0:nk, :]
        q1, q2 = split_q(rows)
        p1, l1 = _softmax_parts(q1, k)
        p2, l2 = _softmax_parts(q2, k)
        w = p1 * (1.0 / l1) - p2 * (lam / l2)
        o = _dot(w.astype(BF16), v)
        ms = jnp.mean(o * o, axis=-1, keepdims=True)
        o_ref[rows, :] = (o * lax.rsqrt(ms + EPS) * sg_ref[...]).astype(BF16)

    @pl.when(bounded)
    def _():
        vt[...] = v_ref[...].T
        _for_query_tiles(tile_bounded, n_ctx, t, 0, DIFF_ATTN_UNROLL)

    @pl.when(jnp.logical_not(bounded))
    def _():
        _for_query_tiles(tile_exact, n_ctx, t, 0, 1)


def _for_query_tiles(tile, n_ctx, t, first_row, unroll):
    for r0 in range(first_row, n_ctx, TM):
        tile(pl.ds(r0, TM), n_ctx)
    lat0 = max(first_row, n_ctx)
    n_lat = (t - lat0) // TM
    unroll = unroll if n_lat % unroll == 0 else 1

    def group(i, _):
        for u in range(unroll):
            tile(pl.ds(pl.multiple_of(lat0 + (i * unroll + u) * TM, TM), TM), t)
        return 0

    lax.fori_loop(0, n_lat // unroll, group, 0)


def _diff_attention(tok, q, k, v, q_gain, k_gain, lam_par, subln_gain, lam_init):
    n_rows, d = q.shape
    heads = d // LANES
    t = tok.t
    as3 = lambda a: a.reshape(tok.batch, t, d)
    spec = pl.BlockSpec((None, t, LANES), lambda b, h: (b, 0, h))
    vec = pl.BlockSpec((1, LANES), lambda b, h: (0, 0))
    out = pl.pallas_call(
        functools.partial(_diff_attn_kernel, lam_init=lam_init, n_ctx=tok.n_ctx),
        grid=(tok.batch, heads),
        in_specs=[pl.BlockSpec(lam_par.shape, lambda b, h: (0, 0)), vec, vec, vec, spec, spec, spec],
        out_specs=spec,
        out_shape=jax.ShapeDtypeStruct((tok.batch, t, d), BF16),
        scratch_shapes=[pltpu.VMEM((LANES, t), BF16)],
        compiler_params=_cparams("parallel", "parallel"),
        name="diff_attention",
    )(lam_par, (subln_gain * (1.0 - lam_init)).reshape(1, LANES), q_gain, k_gain, as3(q), as3(k), as3(v))
    return out.reshape(n_rows, d)


def _gqa_attn_kernel(qg_ref, kg_ref, q_ref, k_ref, v_ref, o_ref, vt, *, n_ctx, first_row):
    t = k_ref.shape[0]
    bounded = _logit_bound(qg_ref, kg_ref, GQA_HEAD_DIM) <= EXP2_SAFE_LOGIT

    def out_rows(rows):
        if first_row == 0:
            return rows
        start = rows.start - first_row
        return pl.ds(start if isinstance(start, int) else pl.multiple_of(start, TM), TM)

    def tile_bounded(rows, nk):
        k, vt_k = k_ref[0:nk, :], vt[:, 0:nk]
        for g in range(0, GQA_GROUP, 2):
            q2h = jnp.concatenate([q_ref[rows, g * LANES:(g + 1) * LANES],
                                   q_ref[rows, (g + 1) * LANES:(g + 2) * LANES]], axis=0)
            o2h = _softmax_parts_bounded(k, q2h, vt_k)
            o_ref[out_rows(rows), g * LANES:(g + 1) * LANES] = o2h[:, :TM].T.astype(BF16)
            o_ref[out_rows(rows), (g + 1) * LANES:(g + 2) * LANES] = o2h[:, TM:].T.astype(BF16)

    def tile_exact(rows, nk):
        k, v = k_ref[0:nk, :], v_ref[0:nk, :]
        for g in range(GQA_GROUP):
            cols = slice(g * LANES, (g + 1) * LANES)
            p, l = _softmax_parts(q_ref[rows, cols], k)
            o_ref[out_rows(rows), cols] = (_dot(p.astype(BF16), v) * (1.0 / l)).astype(BF16)

    @pl.when(bounded)
    def _():
        vt[...] = v_ref[...].T
        _for_query_tiles(tile_bounded, n_ctx, t, first_row, GQA_ATTN_UNROLL)

    @pl.when(jnp.logical_not(bounded))
    def _():
        _for_query_tiles(tile_exact, n_ctx, t, first_row, 1)


def _gqa_attention(tok, q, k, v, q_gain, k_gain, need_ctx):
    d = q.shape[1]
    kv_heads = k.shape[1] // LANES
    t = tok.t
    first_row = 0 if need_ctx else tok.n_ctx
    as3 = lambda a: a.reshape(tok.batch, t, a.shape[1])
    kv_spec = pl.BlockSpec((None, t, LANES), lambda b, h: (b, 0, h))
    q_spec = pl.BlockSpec((None, t, GQA_GROUP * LANES), lambda b, h: (b, 0, h))
    o_spec = pl.BlockSpec((None, t - first_row, GQA_GROUP * LANES), lambda b, h: (b, 0, h))
    vec = pl.BlockSpec((1, LANES), lambda b, h: (0, 0))
    out = pl.pallas_call(
        functools.partial(_gqa_attn_kernel, n_ctx=tok.n_ctx, first_row=first_row),
        grid=(tok.batch, kv_heads),
        in_specs=[vec, vec, q_spec, kv_spec, kv_spec],
        out_specs=o_spec,
        out_shape=jax.ShapeDtypeStruct((tok.batch, t - first_row, d), BF16),
        scratch_shapes=[pltpu.VMEM((LANES, t), BF16)],
        compiler_params=_cparams("parallel", "parallel"),
        name="gqa_attention",
    )(q_gain, k_gain, as3(q), as3(k), as3(v))
    return out.reshape(tok.batch * (t - first_row), d)


def _hgrn_proj_kernel(x_ref, ng_ref, shc, shl, scc, scl, w_ref, lb_ref, q_ref, lff_ref, lfb_ref, v_ref, g_ref,
                      *, layer_idx, depth, tiles_per_batch, n_ctx):
    h = _norm_mod_tile(x_ref, ng_ref, shc, shl, scc, scl, tiles_per_batch, n_ctx)
    d = x_ref.shape[-1]
    q_ref[...] = _silu(_dot(h, w_ref[:, 0:d]))
    for direction, o_ref in enumerate((lff_ref, lfb_ref)):
        rows = [lb_ref[direction * depth + i:direction * depth + i + 1, :] for i in range(depth)]
        m = functools.reduce(jnp.maximum, rows)
        e = [jnp.exp(r - m) for r in rows]
        lb = sum(e[1:layer_idx + 1]) / sum(e) if layer_idx > 0 else jnp.zeros_like(m)
        z = _dot(h, w_ref[:, (1 + direction) * d:(2 + direction) * d])
        o_ref[...] = jnp.log2(lb + (1.0 - lb) * jax.nn.sigmoid(z))
    v_ref[...] = _dot(h, w_ref[:, 3 * d:4 * d]).astype(BF16)
    g_ref[...] = _dot(h, w_ref[:, 4 * d:5 * d])


def _hgrn_project(tok, x_all, mod, layer, norm_gain, w_in, lower_bound):
    n_rows, d = x_all.shape
    depth = lower_bound.shape[1]
    tm = _row_tile(tok.t, HGRN_PROJ_MAX_ROWS, step=HGRN_CHUNK)
    per_b = tok.t // tm
    row_spec = pl.BlockSpec((tm, d), lambda i: (i, 0))
    return pl.pallas_call(
        functools.partial(_hgrn_proj_kernel, layer_idx=layer, depth=depth, tiles_per_batch=per_b, n_ctx=tok.n_ctx),
        grid=(n_rows // tm,),
        in_specs=[row_spec, _vec_spec(d)] + tok.mod_specs(d, layer, 0, per_b) + tok.mod_specs(d, layer, 1, per_b)
                 + [_resident(w_in.shape), pl.BlockSpec((2 * depth, d), lambda i: (0, 0))],
        out_specs=[row_spec] * 5,
        out_shape=[jax.ShapeDtypeStruct((n_rows, d), dt) for dt in (F32, F32, F32, BF16, F32)],
        compiler_params=_cparams("parallel"),
        name="hgrn_project",
    )(x_all, norm_gain.reshape(1, d), mod, mod, mod, mod, w_in.astype(BF16), lower_bound.reshape(2 * depth, d))


def _split2(x):
    hi = x.astype(BF16)
    return hi, (x - hi.astype(F32)).astype(BF16)


def _hgrn_scan_kernel(q_ref, lff_ref, lfb_ref, v_ref, g_ref, ng_ref, o_ref,
                      acc, qs, xs, st, ds, cum, kk, *, n_ctx):
    t, dk = q_ref.shape
    c, hh = HGRN_CHUNK, HGRN_HALF
    n_chunks, ctx_chunks = t // c, n_ctx // c
    cpt = TM // c
    row = lax.broadcasted_iota(jnp.int32, (TM, TM), 0)
    col = lax.broadcasted_iota(jnp.int32, (TM, TM), 1)
    same_chunk = (row // c) == (col // c)
    same_half = (row // hh) == (col // hh)
    cross_half = same_chunk & jnp.logical_not(same_half)
    cum_mat = jnp.where(same_chunk & (col <= row), 1.0, 0.0).astype(BF16)
    in_chunk = lax.broadcasted_iota(jnp.int32, (cpt, c, dk), 1)
    row_c = lax.broadcasted_iota(jnp.int32, (c, c), 0)
    col_c = lax.broadcasted_iota(jnp.int32, (c, c), 1)

    def seg_min(ref):
        return jnp.min(ref[...].reshape(t // hh, hh, dk).sum(axis=1))
    unsafe = jnp.minimum(seg_min(lff_ref), seg_min(lfb_ref)) < HGRN_SAFE_LOG2_DECAY

    as4 = lambda a: a.reshape(cpt, c, dk)
    as8 = lambda a: a.reshape(2 * cpt, hh, dk)
    flat = lambda a: a.reshape(TM, dk)
    half_idx = lax.broadcasted_iota(jnp.int32, (2 * cpt, 1, 1), 0)
    per_half = lambda a: jnp.concatenate([a[i // 2:i // 2 + 1] for i in range(2 * cpt)], axis=0)
    n_tiles = t // TM
    unroll = next(u for u in (3, 2, 1) if n_tiles % u == 0)

    def cumulative(ti):
        rows = pl.ds(pl.multiple_of(ti * TM, TM), TM)
        lf_f, lf_b = lff_ref[rows, :], lfb_ref[rows, :]
        hi, lo = _split2(jnp.concatenate([lf_f, lf_b], axis=1))
        pre = _dot(cum_mat, hi) + _dot(cum_mat, lo)
        pre_b = as4(pre[:, dk:])
        bc_b = pre_b[:, c - 1:c, :] - pre_b + as4(lf_b)
        return rows, (as4(lf_f), as4(lf_b)), (as4(pre[:, :dk]), bc_b)

    def tile_operands(ti):
        rows, lfs, bcs = cumulative(ti)
        q8, v = as8(q_ref[rows, :]), v_ref[rows, :]
        q_d2, k_d2, q_s, k_bar, decay, q_o, k_o = [], [], [], [], [], [], []
        for lf4, bc4, reverse in zip(lfs, bcs, (False, True)):
            edge, last = (hh, 0) if reverse else (hh - 1, c - 1)
            second = (half_idx % 2 == 0) if reverse else (half_idx % 2 == 1)
            b_last, b_edge = bc4[:, last:last + 1, :], bc4[:, edge:edge + 1, :]
            b_last8, b_edge8 = per_half(b_last), per_half(b_edge)
            bc8, k8 = as8(bc4), 1.0 - jnp.exp2(as8(lf4))
            ref = bc8[:, hh // 2:hh // 2 + 1, :]
            rel = bc8 - ref
            q_d, k_d = q8 * jnp.exp2(rel), k8 * jnp.exp2(-rel)
            q_d2.append(flat(q_d).astype(BF16))
            k_d2.append(flat(k_d).astype(BF16))
            q_s.append(flat(q_d * jnp.exp2(ref)))
            k_bar.append(flat(k_d * jnp.exp2(b_last8 - ref)))
            decay.append(jnp.exp2(b_last))
            q_o.append(flat(q_d * jnp.where(second, jnp.exp2(ref - b_edge8), 0.0)))
            k_o.append(flat(k_d * jnp.where(second, 0.0, jnp.exp2(b_edge8 - ref))))
        stack = lambda parts: jnp.concatenate(parts, axis=1).astype(BF16)
        return rows, v, q_d2, k_d2, stack(q_o), stack(k_o), stack(q_s), stack(k_bar), jnp.concatenate(decay, axis=2)

    def tile_matmuls(ti, operands):
        rows, v, q_d2, k_d2, q_o2, k_o2, q_s2, k_bar2, decay2 = operands
        a_same = (jnp.where(col <= row, _dot_nt(q_d2[0], k_d2[0]), 0.0)
                  + jnp.where(col >= row, _dot_nt(q_d2[1], k_d2[1]), 0.0))
        a = jnp.where(same_half, a_same, jnp.where(cross_half, _dot_nt(q_o2, k_o2), 0.0))
        acc[rows, :] = _dot(a.astype(BF16), v)
        store_state_inputs(ti, rows, v, q_s2, k_bar2, decay2)

    def store_state_inputs(ti, rows, v, q_s2, k_bar2, decay2):
        qs[rows, :] = q_s2
        for cc in range(cpt):
            ci = ti * cpt + cc
            xs[ci] = _dot_tn(v[cc * c:(cc + 1) * c, :], k_bar2[cc * c:(cc + 1) * c, :])
            ds[ci] = jnp.broadcast_to(decay2[cc], ds.shape[1:])

    def tile_exact(ti, _):
        rows, lfs, bcs = cumulative(ti)
        q4, v = as4(q_ref[rows, :]), v_ref[rows, :]
        q_s, k_bar, decay = [], [], []
        for lf4, bc4, reverse in zip(lfs, bcs, (False, True)):
            b_last = bc4[:, 0:1, :] if reverse else bc4[:, c - 1:c, :]
            q_s.append(flat(q4 * jnp.exp2(bc4)))
            k_bar.append(flat((1.0 - jnp.exp2(lf4)) * jnp.exp2(b_last - bc4)))
            decay.append(jnp.exp2(b_last))
        stack = lambda parts: jnp.concatenate(parts, axis=1).astype(BF16)
        store_state_inputs(ti, rows, v, stack(q_s), stack(k_bar), jnp.concatenate(decay, axis=2))
        for cc in range(cpt):
            a = jnp.zeros((c, c), F32)
            q_c = q4[cc]
            for lf4, bc4, reverse in zip(lfs, bcs, (False, True)):
                bc_c = bc4[cc]
                cum[...] = bc_c
                kk[...] = 1.0 - jnp.exp2(lf4[cc])

                def col_step(s, a_):
                    d_s = jnp.exp2(jnp.minimum(bc_c - cum[pl.ds(s, 1), :], 0.0))
                    w = jnp.sum(q_c * kk[pl.ds(s, 1), :] * d_s, axis=-1, keepdims=True)
                    return jnp.where(col_c == s, w, a_)
                a_dir = lax.fori_loop(0, c, col_step, jnp.zeros((c, c), F32))
                a = a + jnp.where((col_c >= row_c) if reverse else (col_c <= row_c), a_dir, 0.0)
            crow = pl.ds(pl.multiple_of(ti * TM + cc * c, c), c)
            acc[crow, :] = _dot(a.astype(BF16), v[cc * c:(cc + 1) * c, :])
        return 0

    def tile_group(i, _):
        pending = tile_operands(i * unroll)
        for u in range(unroll):
            upcoming = tile_operands(i * unroll + u + 1) if u + 1 < unroll else None
            tile_matmuls(i * unroll + u, pending)
            pending = upcoming
        return 0

    lax.fori_loop(0, n_tiles // unroll, tile_group, 0)

    @pl.when(unsafe)
    def _():
        lax.fori_loop(0, n_tiles, tile_exact, 0)

    def state_step(i, carry):
        s_f, s_b = carry
        cf = i
        cb = jnp.where(i < ctx_chunks, ctx_chunks - 1 - i, n_chunks - 1 + ctx_chunks - i)
        st[cf, :, 0:dk] = s_f.astype(BF16)
        st[cb, :, dk:2 * dk] = s_b.astype(BF16)
        s_f = s_f * ds[cf, 0:1, 0:dk] + xs[cf, :, 0:dk]
        s_b = s_b * ds[cb, 0:1, dk:2 * dk] + xs[cb, :, dk:2 * dk]
        return s_f, s_b

    zero = jnp.zeros((dk, dk), F32)
    lax.fori_loop(0, n_chunks, state_step, (zero, zero))

    def finish_step(ti):
        rows = pl.ds(pl.multiple_of(ti * TM, TM), TM)
        inter = [_dot_nt(qs[pl.ds(pl.multiple_of(ti * TM + cc * c, c), c), :], st[ti * cpt + cc]) for cc in range(cpt)]
        o = acc[rows, :] + jnp.concatenate(inter, axis=0)
        ms = jnp.mean(o * o, axis=-1, keepdims=True)
        o_ref[rows, :] = (o * lax.rsqrt(ms + EPS) * ng_ref[...] * _silu(g_ref[rows, :])).astype(BF16)

    def finish_group(i, _):
        for u in range(unroll):
            finish_step(i * unroll + u)
        return 0

    lax.fori_loop(0, n_tiles // unroll, finish_group, 0)


def _hgrn_scan(tok, q, lff, lfb, v, g, norm_gain):
    n_rows, d = q.shape
    heads = d // LANES
    t = tok.t
    n_chunks = t // HGRN_CHUNK
    spec = pl.BlockSpec((None, t, LANES), lambda b, h: (b, 0, h))
    as3 = lambda a: a.reshape(tok.batch, t, d)
    out = pl.pallas_call(
        functools.partial(_hgrn_scan_kernel, n_ctx=tok.n_ctx),
        grid=(tok.batch, heads),
        in_specs=[spec] * 5 + [pl.BlockSpec((1, LANES), lambda b, h: (0, 0))],
        out_specs=spec,
        out_shape=jax.ShapeDtypeStruct((tok.batch, t, d), BF16),
        scratch_shapes=[pltpu.VMEM((t, LANES), F32), pltpu.VMEM((t, 2 * LANES), BF16),
                        pltpu.VMEM((n_chunks, LANES, 2 * LANES), F32), pltpu.VMEM((n_chunks, LANES, 2 * LANES), BF16),
                        pltpu.VMEM((n_chunks, SUBLANES, 2 * LANES), F32),
                        pltpu.VMEM((HGRN_CHUNK, LANES), F32), pltpu.VMEM((HGRN_CHUNK, LANES), F32)],
        compiler_params=_cparams("parallel", "parallel"),
        name="hgrn_scan",
    )(as3(q), as3(lff), as3(lfb), as3(v), as3(g), norm_gain.reshape(1, LANES))
    return out.reshape(n_rows, d)


def _ffn_kernel(x_ref, o_ref, wo_ref, bo_ref, ng_ref, g1c, g1l, shc, shl, scc, scl, g2c, g2l, wi_ref, wf_ref, y_ref,
                *, tiles_per_batch, n_ctx):
    chunks, pick = _ctx_chunks(x_ref.shape[0], tiles_per_batch, n_ctx)
    mix = _dot(o_ref[...], wo_ref[...])
    x1 = [x_ref[rs, :] + pick(ic, g1c, g1l) * (mix[rs, :] + bo_ref[...]) for rs, ic in chunks]
    h = jnp.concatenate([_norm_mod(x1_c, ng_ref[...], pick(ic, shc, shl), pick(ic, scc, scl)).astype(BF16)
                         for x1_c, (_, ic) in zip(x1, chunks)], axis=0)
    d_ff = wf_ref.shape[0]
    act = []
    for c0 in range(0, d_ff, FF_CHUNK):
        gate = _dot(h, wi_ref[:, c0:c0 + FF_CHUNK])
        up = _dot(h, wi_ref[:, d_ff + c0:d_ff + c0 + FF_CHUNK])
        act.append((_silu(gate) * up).astype(BF16))
    ff = _dot(jnp.concatenate(act, axis=1), wf_ref[...])
    for x1_c, (rs, ic) in zip(x1, chunks):
        y_ref[rs, :] = x1_c + pick(ic, g2c, g2l) * ff[rs, :]


def _resident_layer(stacked, layer):
    return pl.BlockSpec((None,) + stacked.shape[1:], lambda *_: (layer, 0, 0), pipeline_mode=pl.Buffered(1))


def _out_proj_ffn(tok, x_all, o_all, mod, layer, w_o, b_o, norm_gain, w_in_all, w_out_all, lat_only):
    n_rows, d = x_all.shape
    d_ff = w_out_all.shape[1]
    assert d_ff % FF_CHUNK == 0
    rows_b = tok.seq if lat_only else tok.t
    tm = _row_tile(math.gcd(rows_b, tok.n_ctx) if lat_only else rows_b, FFN_MAX_ROWS)
    per_b = rows_b // tm
    first = tok.n_ctx // tm if lat_only else 0
    stream_row = pl.BlockSpec((tm, d), lambda i: ((i // per_b) * (tok.t // tm) + first + i % per_b, 0))
    out_row = pl.BlockSpec((tm, d), lambda i: (i, 0))
    o_row = stream_row if o_all.shape[0] == n_rows else out_row
    mods = [s for chunk in (2, 3, 4, 5) for s in tok.mod_specs(d, layer, chunk, per_b)]
    return pl.pallas_call(
        functools.partial(_ffn_kernel, tiles_per_batch=per_b, n_ctx=0 if lat_only else tok.n_ctx),
        grid=(tok.batch * per_b,),
        in_specs=[stream_row, o_row, _resident(w_o.shape), _vec_spec(d), _vec_spec(d)] + mods
                 + [_resident_layer(w_in_all, layer), _resident_layer(w_out_all, layer)],
        out_specs=out_row,
        out_shape=jax.ShapeDtypeStruct((tok.batch * rows_b, d), F32),
        compiler_params=_cparams("parallel"),
        name="out_proj_ffn",
    )(x_all, o_all, w_o.astype(BF16), b_o.reshape(1, d), norm_gain.reshape(1, d), *([mod] * 8),
      w_in_all, w_out_all)


def kernel(x, c, ctx, c_ctx, w_ada, b_ada, norm_gain, ffn_w_in, ffn_w_out, fnet_w_out, fnet_b_out,
           diff_w_in, diff_q_gain, diff_k_gain, diff_lambda, diff_subln_gain, diff_w_out,
           hgrn_w_in, hgrn_lower_bound, hgrn_norm_gain, hgrn_w_out,
           gqa_w_in, gqa_q_gain, gqa_k_gain, gqa_w_out):
    batch, seq, d = x.shape
    n_ctx = ctx.shape[1]
    depth = w_ada.shape[0]
    assert batch + 1 <= COND_ROWS and d % LANES == 0
    tok = _Tokens(batch, n_ctx, seq)

    cond = jnp.zeros((COND_ROWS, d), F32).at[0].set(c_ctx).at[1:1 + batch].set(c)
    mod = _ada_modulation(cond, w_ada, b_ada)
    x_all = None
    zero_bias = jnp.zeros((d,), F32)
    ffn_w_in_bf16, ffn_w_out_bf16 = ffn_w_in.astype(BF16), ffn_w_out.astype(BF16)

    for i in range(depth):
        m, j = i % N_MIXERS, i // N_MIXERS
        need_ctx = i < depth - 1
        if m == 0:
            if x_all is not None:
                stream = x_all.reshape(batch, tok.t, d)
                ctx, x = stream[:, :n_ctx], stream[:, n_ctx:]
            o, x_all = _fnet_mix(tok, ctx, x, mod, i, norm_gain[i, 0], d // FNET_GROUPS)
            w_o, b_o = fnet_w_out[j], fnet_b_out[j]
        elif m == 1:
            lam_init = 0.8 - 0.6 * math.exp(-0.3 * i)
            qg = diff_q_gain[j].reshape(1, LANES) * (DIFF_HEAD_DIM ** -0.5 * LOG2_E)
            kg = diff_k_gain[j].reshape(1, LANES)
            q, k, v = _qkv_project(tok, x_all, mod, i, norm_gain[i, 0], diff_w_in[j], qg, kg,
                                   DIFF_HEAD_DIM, d, d)
            o = _diff_attention(tok, q, k, v, qg, kg, diff_lambda[j], diff_subln_gain[j], lam_init)
            w_o, b_o = diff_w_out[j], zero_bias
        elif m == 2:
            q, lff, lfb, v, g = _hgrn_project(tok, x_all, mod, i, norm_gain[i, 0], hgrn_w_in[j], hgrn_lower_bound)
            o = _hgrn_scan(tok, q, lff, lfb, v, g, hgrn_norm_gain[j])
            w_o, b_o = hgrn_w_out[j], zero_bias
        else:
            kv = (gqa_w_in.shape[-1] - d) // 2
            qg = gqa_q_gain[j].reshape(1, LANES) * (GQA_HEAD_DIM ** -0.5 * LOG2_E)
            kg = gqa_k_gain[j].reshape(1, LANES)
            q, k, v = _qkv_project(tok, x_all, mod, i, norm_gain[i, 0], gqa_w_in[j], qg, kg, GQA_HEAD_DIM, d, kv)
            o = _gqa_attention(tok, q, k, v, qg, kg, need_ctx)
            w_o, b_o = gqa_w_out[j], zero_bias
        x_all = _out_proj_ffn(tok, x_all, o, mod, i, w_o, b_o, norm_gain[i, 1], ffn_w_in_bf16, ffn_w_out_bf16,
                              lat_only=not need_ctx)
    return x_all.reshape(batch, seq, d)
```

```python
import functools
import math

import numpy as np
import jax
import jax.numpy as jnp
from jax import lax
from jax.experimental import pallas as pl
from jax.experimental.pallas import tpu as pltpu

F32 = jnp.float32
BF16 = jnp.bfloat16

EPS = 1e-6
GRID_W = 64
ROPE_THETA = 10000.0
N_MIXERS = 4

LANES = 128
SUBLANES = 8
BF16_SUBLANES = 16
TM = 256
COND_ROWS = 16
VMEM_LIMIT = 56 * 1024 * 1024

FNET_GROUPS = 8
DIFF_HEAD_DIM = 64
GQA_HEAD_DIM = 128
GQA_GROUP = 4
HGRN_CHUNK = 64
HGRN_HALF = 32
HGRN_TILE_UNROLL = 9
HGRN_SAFE_LOG2_DECAY = -115.0
FF_CHUNK = 256
FFN_MAX_ROWS = 768
PROJ_MAX_ROWS = 768
HGRN_PROJ_MAX_ROWS = 256
MXU_DIM = 256
LOG2_E = math.log2(math.e)
EXP2_SAFE_LOGIT = 100.0
BF16_SLACK = 1.02
DIFF_ATTN_UNROLL = 4
GQA_ATTN_UNROLL = 2


def _cparams(*sem):
    return pltpu.CompilerParams(dimension_semantics=sem, vmem_limit_bytes=VMEM_LIMIT)


def _resident(shape):
    nd = len(shape)
    return pl.BlockSpec(shape, lambda *_: (0,) * nd, pipeline_mode=pl.Buffered(1))


def _silu(x):
    return x * jax.nn.sigmoid(x)


def _norm_mod(x, gain, shift, scale):
    ms = jnp.mean(x * x, axis=-1, keepdims=True)
    y = x * lax.rsqrt(ms + EPS) * gain
    return y * (1.0 + scale) + shift


def _dot(a, b):
    return jnp.dot(a, b, preferred_element_type=F32)


def _dot_nt(a, b):
    return lax.dot_general(a, b, (((1,), (1,)), ((), ())), preferred_element_type=F32)


def _dot_tn(a, b):
    return lax.dot_general(a, b, (((0,), (0,)), ((), ())), preferred_element_type=F32)


def _ada_kernel(cond_ref, w_ref, b_ref, o_ref):
    a = _silu(cond_ref[...]).astype(BF16)
    o_ref[...] = _dot(a, w_ref[...].astype(BF16)) + b_ref[...]


def _ada_modulation(cond, w_ada, b_ada):
    depth, d, n6 = w_ada.shape
    bn = n6 // 4
    out = pl.pallas_call(
        _ada_kernel,
        grid=(depth, n6 // bn),
        in_specs=[
            pl.BlockSpec((COND_ROWS, d), lambda i, j: (0, 0)),
            pl.BlockSpec((None, d, bn), lambda i, j: (i, 0, j)),
            pl.BlockSpec((None, 1, bn), lambda i, j: (i, 0, j)),
        ],
        out_specs=pl.BlockSpec((None, COND_ROWS, bn), lambda i, j: (i, 0, j)),
        out_shape=jax.ShapeDtypeStruct((depth, COND_ROWS, n6), F32),
        compiler_params=_cparams("parallel", "parallel"),
        name="ada_modulation",
    )(cond, w_ada, b_ada.reshape(depth, 1, n6))
    return out.reshape(depth * COND_ROWS * 6, 1, d)


class _Tokens:
    def __init__(self, batch, n_ctx, seq):
        assert n_ctx % TM == 0 and seq % TM == 0
        self.batch, self.n_ctx, self.seq = batch, n_ctx, seq
        self.t = n_ctx + seq
        self.ctx_tiles = n_ctx // TM
        self.tiles = self.t // TM

    def mod_specs(self, d, layer, chunk, tiles_per_batch):
        ctx = pl.BlockSpec((None, 1, d), lambda i: (layer * COND_ROWS * 6 + chunk, 0, 0))
        lat = pl.BlockSpec((None, 1, d), lambda i: ((layer * COND_ROWS + 1 + i // tiles_per_batch) * 6 + chunk, 0, 0))
        return [ctx, lat]


def _ctx_chunks(rows, tiles_per_batch, n_ctx):
    row0 = (pl.program_id(0) % tiles_per_batch) * rows
    chunks = [(slice(r, r + TM), row0 + r < n_ctx) for r in range(0, rows, TM)]
    if n_ctx == 0:
        return chunks, lambda is_ctx, c_ref, l_ref: l_ref[...]
    return chunks, lambda is_ctx, c_ref, l_ref: jnp.where(is_ctx, c_ref[...], l_ref[...])


def _norm_mod_tile(x_ref, ng_ref, shc, shl, scc, scl, tiles_per_batch, n_ctx):
    chunks, pick = _ctx_chunks(x_ref.shape[0], tiles_per_batch, n_ctx)
    return jnp.concatenate([_norm_mod(x_ref[rs, :], ng_ref[...], pick(ic, shc, shl), pick(ic, scc, scl)).astype(BF16)
                            for rs, ic in chunks], axis=0)


def _row_tile(n, cap, step=TM):
    return max(r for r in range(step, cap + 1, step) if n % r == 0)


def _vec_spec(n):
    return pl.BlockSpec((1, n), lambda *_: (0, 0))


def _dft_tables(n_ctx, seq, group_dim):
    def cs(n):
        k = np.arange(n, dtype=np.int64)
        ang = 2.0 * np.pi * ((k[:, None] * k[None, :]) % n).astype(np.float64) / n
        return np.cos(ang), np.sin(ang)
    cc, sc = cs(group_dim)
    chan = np.concatenate([cc, sc], axis=1)
    cx, sx = cs(n_ctx)
    pos_ctx = np.concatenate([cx, -sx], axis=1)
    cl, sl = cs(seq)
    pt = _half_spectrum_tile(seq)
    rows = (np.arange(seq // 2 // pt)[:, None] * pt + np.arange(pt + BF16_SUBLANES)[None, :]) % seq
    rev = np.zeros((pt, pt + BF16_SUBLANES), np.float32)
    rev[np.arange(pt), pt - np.arange(pt)] = 1.0
    as_bf16 = lambda a: jnp.asarray(a.astype(np.float32)).astype(BF16)
    return as_bf16(chan), as_bf16(pos_ctx), as_bf16(cl[rows]), as_bf16(sl[rows]), as_bf16(rev)


def _half_spectrum_tile(seq):
    return min(TM, seq // 2)


def _fnet_kernel(ctx_ref, lat_ref, ng_ref, shc_ref, scc_ref, shl_ref, scl_ref, chan_ref, pctx_ref, cos_ref, sin_ref,
                 rev_ref, y_ref, xs_ref, ab_ctx, ab_lat, *, group_dim):
    j = pl.program_id(1)
    n_ctx, d = ctx_ref.shape
    seq = lat_ref.shape[0]
    ctx_tiles = n_ctx // TM
    pt = _half_spectrum_tile(seq)
    p_tiles = seq // 2 // pt

    @pl.when(j == 0)
    def _():
        for src, sh, sc, dst, n in ((ctx_ref, shc_ref, scc_ref, ab_ctx, n_ctx), (lat_ref, shl_ref, scl_ref, ab_lat, seq)):
            for row0 in range(0, n, TM):
                h = _norm_mod(src[row0:row0 + TM, :], ng_ref[...], sh[...], sc[...]).astype(BF16)
                for g in range(d // group_dim):
                    cols = slice(g * group_dim, (g + 1) * group_dim)
                    ab = _dot(h[:, cols], chan_ref[...])
                    dst[row0:row0 + TM, cols] = ab[:, :group_dim].astype(BF16)
                    dst[n + row0:n + row0 + TM, cols] = ab[:, group_dim:].astype(BF16)

    @pl.when(j < ctx_tiles)
    def _():
        rows = pl.ds(pl.multiple_of(j * TM, TM), TM)
        y = _dot(pctx_ref[rows, :], ab_ctx[...]) * (1.0 / math.sqrt(n_ctx * group_dim))
        y_ref[rows, :] = y.astype(BF16)
        xs_ref[...] = ctx_ref[rows, :]

    @pl.when(j >= ctx_tiles)
    def _():
        xs_ref[...] = lat_ref[pl.ds(pl.multiple_of((j - ctx_tiles) * TM, TM), TM), :]

    @pl.when((j >= ctx_tiles) & (j < ctx_tiles + p_tiles))
    def _():
        tile = j - ctx_tiles
        scale = 1.0 / math.sqrt(seq * group_dim)
        p_part = _dot(cos_ref[...], ab_lat[0:seq, :])
        q_part = _dot(sin_ref[...], ab_lat[seq:2 * seq, :])
        y_ref[pl.ds(pl.multiple_of(n_ctx + tile * pt, pt), pt), :] = ((p_part - q_part)[0:pt] * scale).astype(BF16)
        mirrored = _dot(rev_ref[...], ((p_part + q_part) * scale).astype(BF16))
        y_ref[pl.ds(pl.multiple_of(n_ctx + seq - (tile + 1) * pt, pt), pt), :] = mirrored.astype(BF16)


def _fnet_mix(tok, ctx, x, mod, layer, norm_gain, group_dim):
    b, t, d = tok.batch, tok.t, x.shape[-1]
    chan, pos_ctx, cos_lat, sin_lat, rev = _dft_tables(tok.n_ctx, tok.seq, group_dim)
    ct, p_tiles = tok.ctx_tiles, cos_lat.shape[0]
    assert p_tiles <= tok.tiles - ct

    def mod_spec(chunk, is_ctx):
        return pl.BlockSpec((None, 1, d), lambda i, j: ((layer * COND_ROWS + (0 if is_ctx else 1 + i)) * 6 + chunk, 0, 0))

    table_spec = pl.BlockSpec((None,) + cos_lat.shape[1:], lambda i, j: (jnp.clip(j - ct, 0, p_tiles - 1), 0, 0))
    y, xs = pl.pallas_call(
        functools.partial(_fnet_kernel, group_dim=group_dim),
        grid=(b, tok.tiles),
        in_specs=[
            pl.BlockSpec((None, tok.n_ctx, d), lambda i, j: (i, 0, 0)),
            pl.BlockSpec((None, tok.seq, d), lambda i, j: (i, 0, 0)),
            pl.BlockSpec((1, d), lambda i, j: (0, 0)),
            mod_spec(0, True), mod_spec(1, True), mod_spec(0, False), mod_spec(1, False),
            _resident(chan.shape), _resident(pos_ctx.shape), table_spec, table_spec, _resident(rev.shape),
        ],
        out_specs=[pl.BlockSpec((None, t, d), lambda i, j: (i, 0, 0)),
                   pl.BlockSpec((TM, d), lambda i, j: (i * tok.tiles + j, 0))],
        out_shape=[jax.ShapeDtypeStruct((b, t, d), BF16), jax.ShapeDtypeStruct((b * t, d), F32)],
        scratch_shapes=[pltpu.VMEM((2 * tok.n_ctx, d), BF16), pltpu.VMEM((2 * tok.seq, d), BF16)],
        compiler_params=_cparams("parallel", "arbitrary"),
        name="fnet_mix",
    )(ctx, x, norm_gain.reshape(1, d), mod, mod, mod, mod, chan, pos_ctx, cos_lat, sin_lat, rev)
    return y.reshape(b * t, d), xs


def _rope_tables(tok, head_dim):
    rows = tok.seq // GRID_W
    row = jnp.repeat(jnp.arange(rows, dtype=F32), GRID_W)
    col = jnp.tile(jnp.arange(GRID_W, dtype=F32), rows)
    n_freq = head_dim // 4
    inv_freq = ROPE_THETA ** (-jnp.arange(n_freq, dtype=F32) / n_freq)
    ang = jnp.concatenate([row[:, None] * inv_freq, col[:, None] * inv_freq], axis=-1)
    cos = jnp.concatenate([jnp.ones((tok.n_ctx, head_dim // 2), F32), jnp.cos(ang)], axis=0)
    sin = jnp.concatenate([jnp.zeros((tok.n_ctx, head_dim // 2), F32), jnp.sin(ang)], axis=0)
    reps = LANES // head_dim
    cos = jnp.tile(jnp.concatenate([cos, cos], axis=-1), (1, reps))
    sin = jnp.tile(jnp.concatenate([-sin, sin], axis=-1), (1, reps))
    return cos, sin


def _group_mean_matrix(group, width):
    g = np.arange(width) // group
    return jnp.asarray((g[:, None] == g[None, :]).astype(np.float32) / group).astype(BF16)


def _rope(yn, cos, sin, head_dim):
    half = head_dim // 2
    if head_dim == LANES:
        partner = pltpu.roll(yn, half, 1)
    else:
        lane = lax.broadcasted_iota(jnp.int32, yn.shape, 1)
        partner = jnp.where(lane % head_dim < half, pltpu.roll(yn, LANES - half, 1), pltpu.roll(yn, half, 1))
    return yn * cos + partner * sin


def _qkv_kernel(x_ref, ng_ref, shc, shl, scc, scl, w_ref, qg_ref, kg_ref, cos_ref, sin_ref, gm_ref,
                q_ref, k_ref, vt_ref, *, head_dim, tiles_per_batch, n_ctx):
    h = _norm_mod_tile(x_ref, ng_ref, shc, shl, scc, scl, tiles_per_batch, n_ctx)
    cos, sin, gm = cos_ref[...], sin_ref[...], gm_ref[...]
    nq, nk = q_ref.shape[-1], k_ref.shape[-1]
    for o_ref, g_ref, col0, n in ((q_ref, qg_ref, 0, nq), (k_ref, kg_ref, nq, nk)):
        y = _dot(h, w_ref[:, col0:col0 + n])
        for c0 in range(0, n, MXU_DIM):
            yb = y[:, c0:c0 + MXU_DIM]
            yn = yb * lax.rsqrt(_dot((yb * yb).astype(BF16), gm) + EPS) * g_ref[...]
            for l0 in range(0, MXU_DIM, LANES):
                o_ref[:, c0 + l0:c0 + l0 + LANES] = _rope(yn[:, l0:l0 + LANES], cos, sin, head_dim).astype(BF16)
    vt_ref[...] = _dot(h, w_ref[:, nq + nk:]).astype(BF16).T


def _qkv_project(tok, x_all, mod, layer, norm_gain, w_in, q_gain, k_gain, head_dim, nq, nk):
    n_rows, d = x_all.shape
    nv = w_in.shape[1] - nq - nk
    assert nq % MXU_DIM == 0 and nk % MXU_DIM == 0
    cos, sin = _rope_tables(tok, head_dim)
    tm = _row_tile(tok.t, PROJ_MAX_ROWS)
    per_b = tok.t // tm
    row_spec = lambda n: pl.BlockSpec((tm, n), lambda i: (i, 0))
    tab_spec = pl.BlockSpec((tm, LANES), lambda i: (i % per_b, 0))
    gain2 = lambda g: jnp.tile(g, (1, MXU_DIM // LANES))
    return pl.pallas_call(
        functools.partial(_qkv_kernel, head_dim=head_dim, tiles_per_batch=per_b, n_ctx=tok.n_ctx),
        grid=(n_rows // tm,),
        in_specs=[row_spec(d), _vec_spec(d)] + tok.mod_specs(d, layer, 0, per_b) + tok.mod_specs(d, layer, 1, per_b)
                 + [_resident(w_in.shape), _vec_spec(MXU_DIM), _vec_spec(MXU_DIM), tab_spec, tab_spec,
                    _resident((MXU_DIM, MXU_DIM))],
        out_specs=[row_spec(nq), row_spec(nk), pl.BlockSpec((None, nv, tm), lambda i: (i // per_b, 0, i % per_b))],
        out_shape=[jax.ShapeDtypeStruct((n_rows, nq), BF16), jax.ShapeDtypeStruct((n_rows, nk), BF16),
                   jax.ShapeDtypeStruct((tok.batch, nv, tok.t), BF16)],
        compiler_params=_cparams("parallel"),
        name=f"qkv_project_hd{head_dim}",
    )(x_all, norm_gain.reshape(1, d), mod, mod, mod, mod, w_in.astype(BF16), gain2(q_gain), gain2(k_gain),
      cos, sin, _group_mean_matrix(head_dim, MXU_DIM))


def _softmax_parts(q, k):
    s = _dot_nt(q, k)
    p = jnp.exp2(s - jnp.max(s, axis=-1, keepdims=True))
    return p, jnp.sum(p, axis=-1, keepdims=True)


def _softmax_parts_bounded(k, q, vt):
    p = jnp.exp2(_dot_nt(k, q))
    l = jnp.sum(p, axis=0, keepdims=True)
    return _dot(vt, p.astype(BF16)) * (1.0 / l)


def _logit_bound(qg_ref, kg_ref, head_dim):
    return BF16_SLACK * head_dim * jnp.max(jnp.abs(qg_ref[...])) * jnp.max(jnp.abs(kg_ref[...]))


def _diff_attn_kernel(lp_ref, sg_ref, qg_ref, kg_ref, q_ref, k_ref, vt, o_ref, *, lam_init, n_ctx):
    lp = lp_ref[...]
    lam = (jnp.exp(jnp.sum(lp[0:1] * lp[1:2], axis=-1, keepdims=True))
           - jnp.exp(jnp.sum(lp[2:3] * lp[3:4], axis=-1, keepdims=True)) + lam_init)
    t = k_ref.shape[0]
    lane = lax.broadcasted_iota(jnp.int32, (TM, LANES), 1)
    bounded = _logit_bound(qg_ref, kg_ref, DIFF_HEAD_DIM) <= EXP2_SAFE_LOGIT

    def split_q(rows):
        q = q_ref[rows, :]
        zero = jnp.zeros_like(q)
        return jnp.where(lane < DIFF_HEAD_DIM, q, zero), jnp.where(lane >= DIFF_HEAD_DIM, q, zero)

    def tile_bounded(rows, nk):
        o12 = _softmax_parts_bounded(k_ref[0:nk, :], jnp.concatenate(split_q(rows), axis=0), vt[:, 0:nk])
        ot = o12[:, :TM] - lam * o12[:, TM:]
        ms = jnp.mean(ot * ot, axis=0, keepdims=True)
        o_ref[rows, :] = ((ot * lax.rsqrt(ms + EPS)).T * sg_ref[...]).astype(BF16)

    def tile_exact(rows, nk):
        k = k_ref[0:nk, :]
        q1, q2 = split_q(rows)
        p1, l1 = _softmax_parts(q1, k)
        p2, l2 = _softmax_parts(q2, k)
        w = p1 * (1.0 / l1) - p2 * (lam / l2)
        o = _dot(w.astype(BF16), vt[:, 0:nk].T)
        ms = jnp.mean(o * o, axis=-1, keepdims=True)
        o_ref[rows, :] = (o * lax.rsqrt(ms + EPS) * sg_ref[...]).astype(BF16)

    @pl.when(bounded)
    def _():
        _for_query_tiles(tile_bounded, n_ctx, t, 0, DIFF_ATTN_UNROLL)

    @pl.when(jnp.logical_not(bounded))
    def _():
        _for_query_tiles(tile_exact, n_ctx, t, 0, 1)


def _for_query_tiles(tile, n_ctx, t, first_row, unroll):
    for r0 in range(first_row, n_ctx, TM):
        tile(pl.ds(r0, TM), n_ctx)
    lat0 = max(first_row, n_ctx)
    n_lat = (t - lat0) // TM
    unroll = unroll if n_lat % unroll == 0 else 1

    def group(i, _):
        for u in range(unroll):
            tile(pl.ds(pl.multiple_of(lat0 + (i * unroll + u) * TM, TM), TM), t)
        return 0

    lax.fori_loop(0, n_lat // unroll, group, 0)


def _diff_attention(tok, q, k, vt, q_gain, k_gain, lam_par, subln_gain, lam_init):
    n_rows, d = q.shape
    heads = d // LANES
    t = tok.t
    as3 = lambda a: a.reshape(tok.batch, t, d)
    spec = pl.BlockSpec((None, t, LANES), lambda b, h: (b, 0, h))
    vt_spec = pl.BlockSpec((None, LANES, t), lambda b, h: (b, h, 0))
    vec = pl.BlockSpec((1, LANES), lambda b, h: (0, 0))
    out = pl.pallas_call(
        functools.partial(_diff_attn_kernel, lam_init=lam_init, n_ctx=tok.n_ctx),
        grid=(tok.batch, heads),
        in_specs=[pl.BlockSpec(lam_par.shape, lambda b, h: (0, 0)), vec, vec, vec, spec, spec, vt_spec],
        out_specs=spec,
        out_shape=jax.ShapeDtypeStruct((tok.batch, t, d), BF16),
        compiler_params=_cparams("parallel", "parallel"),
        name="diff_attention",
    )(lam_par, (subln_gain * (1.0 - lam_init)).reshape(1, LANES), q_gain, k_gain, as3(q), as3(k), vt)
    return out.reshape(n_rows, d)


def _gqa_attn_kernel(qg_ref, kg_ref, q_ref, k_ref, vt, o_ref, *, n_ctx, first_row):
    t = k_ref.shape[0]
    bounded = _logit_bound(qg_ref, kg_ref, GQA_HEAD_DIM) <= EXP2_SAFE_LOGIT

    def out_rows(rows):
        if first_row == 0:
            return rows
        start = rows.start - first_row
        return pl.ds(start if isinstance(start, int) else pl.multiple_of(start, TM), TM)

    def tile_bounded(rows, nk):
        k, vt_k = k_ref[0:nk, :], vt[:, 0:nk]
        for g in range(0, GQA_GROUP, 2):
            q2h = jnp.concatenate([q_ref[rows, g * LANES:(g + 1) * LANES],
                                   q_ref[rows, (g + 1) * LANES:(g + 2) * LANES]], axis=0)
            o2h = _softmax_parts_bounded(k, q2h, vt_k)
            o_ref[out_rows(rows), g * LANES:(g + 1) * LANES] = o2h[:, :TM].T.astype(BF16)
            o_ref[out_rows(rows), (g + 1) * LANES:(g + 2) * LANES] = o2h[:, TM:].T.astype(BF16)

    def tile_exact(rows, nk):
        k, v = k_ref[0:nk, :], vt[:, 0:nk].T
        for g in range(GQA_GROUP):
            cols = slice(g * LANES, (g + 1) * LANES)
            p, l = _softmax_parts(q_ref[rows, cols], k)
            o_ref[out_rows(rows), cols] = (_dot(p.astype(BF16), v) * (1.0 / l)).astype(BF16)

    @pl.when(bounded)
    def _():
        _for_query_tiles(tile_bounded, n_ctx, t, first_row, GQA_ATTN_UNROLL)

    @pl.when(jnp.logical_not(bounded))
    def _():
        _for_query_tiles(tile_exact, n_ctx, t, first_row, 1)


def _gqa_attention(tok, q, k, vt, q_gain, k_gain, need_ctx):
    d = q.shape[1]
    kv_heads = k.shape[1] // LANES
    t = tok.t
    first_row = 0 if need_ctx else tok.n_ctx
    as3 = lambda a: a.reshape(tok.batch, t, a.shape[1])
    k_spec = pl.BlockSpec((None, t, LANES), lambda b, h: (b, 0, h))
    vt_spec = pl.BlockSpec((None, LANES, t), lambda b, h: (b, h, 0))
    q_spec = pl.BlockSpec((None, t, GQA_GROUP * LANES), lambda b, h: (b, 0, h))
    o_spec = pl.BlockSpec((None, t - first_row, GQA_GROUP * LANES), lambda b, h: (b, 0, h))
    vec = pl.BlockSpec((1, LANES), lambda b, h: (0, 0))
    out = pl.pallas_call(
        functools.partial(_gqa_attn_kernel, n_ctx=tok.n_ctx, first_row=first_row),
        grid=(tok.batch, kv_heads),
        in_specs=[vec, vec, q_spec, k_spec, vt_spec],
        out_specs=o_spec,
        out_shape=jax.ShapeDtypeStruct((tok.batch, t - first_row, d), BF16),
        compiler_params=_cparams("parallel", "parallel"),
        name="gqa_attention",
    )(q_gain, k_gain, as3(q), as3(k), vt)
    return out.reshape(tok.batch * (t - first_row), d)


def _hgrn_proj_kernel(x_ref, ng_ref, shc, shl, scc, scl, w_ref, lb_ref, q_ref, lff_ref, lfb_ref, v_ref, g_ref,
                      *, layer_idx, depth, tiles_per_batch, n_ctx):
    h = _norm_mod_tile(x_ref, ng_ref, shc, shl, scc, scl, tiles_per_batch, n_ctx)
    d = x_ref.shape[-1]
    q_ref[...] = _silu(_dot(h, w_ref[:, 0:d]))
    for direction, o_ref in enumerate((lff_ref, lfb_ref)):
        rows = [lb_ref[direction * depth + i:direction * depth + i + 1, :] for i in range(depth)]
        m = functools.reduce(jnp.maximum, rows)
        e = [jnp.exp(r - m) for r in rows]
        lb = sum(e[1:layer_idx + 1]) / sum(e) if layer_idx > 0 else jnp.zeros_like(m)
        z = _dot(h, w_ref[:, (1 + direction) * d:(2 + direction) * d])
        o_ref[...] = jnp.log2(lb + (1.0 - lb) * jax.nn.sigmoid(z))
    v_ref[...] = _dot(h, w_ref[:, 3 * d:4 * d]).astype(BF16)
    g_ref[...] = _dot(h, w_ref[:, 4 * d:5 * d])


def _hgrn_project(tok, x_all, mod, layer, norm_gain, w_in, lower_bound):
    n_rows, d = x_all.shape
    depth = lower_bound.shape[1]
    tm = _row_tile(tok.t, HGRN_PROJ_MAX_ROWS, step=HGRN_CHUNK)
    per_b = tok.t // tm
    row_spec = pl.BlockSpec((tm, d), lambda i: (i, 0))
    return pl.pallas_call(
        functools.partial(_hgrn_proj_kernel, layer_idx=layer, depth=depth, tiles_per_batch=per_b, n_ctx=tok.n_ctx),
        grid=(n_rows // tm,),
        in_specs=[row_spec, _vec_spec(d)] + tok.mod_specs(d, layer, 0, per_b) + tok.mod_specs(d, layer, 1, per_b)
                 + [_resident(w_in.shape), pl.BlockSpec((2 * depth, d), lambda i: (0, 0))],
        out_specs=[row_spec] * 5,
        out_shape=[jax.ShapeDtypeStruct((n_rows, d), dt) for dt in (F32, F32, F32, BF16, F32)],
        compiler_params=_cparams("parallel"),
        name="hgrn_project",
    )(x_all, norm_gain.reshape(1, d), mod, mod, mod, mod, w_in.astype(BF16), lower_bound.reshape(2 * depth, d))


def _split2(x):
    hi = x.astype(BF16)
    return hi, (x - hi.astype(F32)).astype(BF16)


def _hgrn_scan_kernel(q_ref, lff_ref, lfb_ref, v_ref, g_ref, ng_ref, o_ref,
                      acc, qs, xs, st, ds, cum, kk, *, n_ctx):
    t, dk = q_ref.shape
    c, hh = HGRN_CHUNK, HGRN_HALF
    n_chunks, ctx_chunks = t // c, n_ctx // c
    cpt = TM // c
    row = lax.broadcasted_iota(jnp.int32, (TM, TM), 0)
    col = lax.broadcasted_iota(jnp.int32, (TM, TM), 1)
    same_chunk = (row // c) == (col // c)
    same_half = (row // hh) == (col // hh)
    cross_half = same_chunk & jnp.logical_not(same_half)
    cum_mat = jnp.where(same_chunk & (col <= row), 1.0, 0.0).astype(BF16)
    in_chunk = lax.broadcasted_iota(jnp.int32, (cpt, c, dk), 1)
    row_c = lax.broadcasted_iota(jnp.int32, (c, c), 0)
    col_c = lax.broadcasted_iota(jnp.int32, (c, c), 1)

    def seg_min(ref):
        return jnp.min(ref[...].reshape(t // hh, hh, dk).sum(axis=1))
    unsafe = jnp.minimum(seg_min(lff_ref), seg_min(lfb_ref)) < HGRN_SAFE_LOG2_DECAY

    as4 = lambda a: a.reshape(cpt, c, dk)
    as8 = lambda a: a.reshape(2 * cpt, hh, dk)
    flat = lambda a: a.reshape(TM, dk)
    half_idx = lax.broadcasted_iota(jnp.int32, (2 * cpt, 1, 1), 0)
    per_half = lambda a: jnp.concatenate([a[i // 2:i // 2 + 1] for i in range(2 * cpt)], axis=0)
    n_tiles = t // TM
    unroll = next(u for u in (HGRN_TILE_UNROLL, 3, 2, 1) if n_tiles % u == 0)

    def cumulative(ti):
        rows = pl.ds(pl.multiple_of(ti * TM, TM), TM)
        lf_f, lf_b = lff_ref[rows, :], lfb_ref[rows, :]
        hi, lo = _split2(jnp.concatenate([lf_f, lf_b], axis=1))
        pre = _dot(cum_mat, hi) + _dot(cum_mat, lo)
        pre_b = as4(pre[:, dk:])
        bc_b = pre_b[:, c - 1:c, :] - pre_b + as4(lf_b)
        return rows, (as4(lf_f), as4(lf_b)), (as4(pre[:, :dk]), bc_b)

    def tile_operands(cumulated):
        rows, lfs, bcs = cumulated
        q8, v = as8(q_ref[rows, :]), v_ref[rows, :]
        q_d2, k_d2, q_s, k_bar, decay, q_o, k_o = [], [], [], [], [], [], []
        for lf4, bc4, reverse in zip(lfs, bcs, (False, True)):
            edge, last = (hh, 0) if reverse else (hh - 1, c - 1)
            second = (half_idx % 2 == 0) if reverse else (half_idx % 2 == 1)
            b_last, b_edge = bc4[:, last:last + 1, :], bc4[:, edge:edge + 1, :]
            b_last8, b_edge8 = per_half(b_last), per_half(b_edge)
            bc8, k8 = as8(bc4), 1.0 - jnp.exp2(as8(lf4))
            ref = bc8[:, hh // 2:hh // 2 + 1, :]
            rel = bc8 - ref
            q_d, k_d = q8 * jnp.exp2(rel), k8 * jnp.exp2(-rel)
            q_d2.append(flat(q_d).astype(BF16))
            k_d2.append(flat(k_d).astype(BF16))
            q_s.append(flat(q_d * jnp.exp2(ref)))
            k_bar.append(flat(k_d * jnp.exp2(b_last8 - ref)))
            decay.append(jnp.exp2(b_last))
            q_o.append(flat(q_d * jnp.where(second, jnp.exp2(ref - b_edge8), 0.0)))
            k_o.append(flat(k_d * jnp.where(second, 0.0, jnp.exp2(b_edge8 - ref))))
        stack = lambda parts: jnp.concatenate(parts, axis=1).astype(BF16)
        return rows, v, q_d2, k_d2, stack(q_o), stack(k_o), stack(q_s), stack(k_bar), jnp.concatenate(decay, axis=2)

    def tile_matmuls(ti, operands):
        rows, v, q_d2, k_d2, q_o2, k_o2, q_s2, k_bar2, decay2 = operands
        a_same = (jnp.where(col <= row, _dot_nt(q_d2[0], k_d2[0]), 0.0)
                  + jnp.where(col >= row, _dot_nt(q_d2[1], k_d2[1]), 0.0))
        a = jnp.where(same_half, a_same, jnp.where(cross_half, _dot_nt(q_o2, k_o2), 0.0))
        acc[rows, :] = _dot(a.astype(BF16), v)
        store_state_inputs(ti, rows, v, q_s2, k_bar2, decay2)

    def store_state_inputs(ti, rows, v, q_s2, k_bar2, decay2):
        qs[rows, :] = q_s2
        for cc in range(cpt):
            ci = ti * cpt + cc
            xs[ci] = _dot_tn(v[cc * c:(cc + 1) * c, :], k_bar2[cc * c:(cc + 1) * c, :])
            ds[ci] = jnp.broadcast_to(decay2[cc], ds.shape[1:])

    def tile_exact(ti, _):
        rows, lfs, bcs = cumulative(ti)
        q4, v = as4(q_ref[rows, :]), v_ref[rows, :]
        q_s, k_bar, decay = [], [], []
        for lf4, bc4, reverse in zip(lfs, bcs, (False, True)):
            b_last = bc4[:, 0:1, :] if reverse else bc4[:, c - 1:c, :]
            q_s.append(flat(q4 * jnp.exp2(bc4)))
            k_bar.append(flat((1.0 - jnp.exp2(lf4)) * jnp.exp2(b_last - bc4)))
            decay.append(jnp.exp2(b_last))
        stack = lambda parts: jnp.concatenate(parts, axis=1).astype(BF16)
        store_state_inputs(ti, rows, v, stack(q_s), stack(k_bar), jnp.concatenate(decay, axis=2))
        for cc in range(cpt):
            a = jnp.zeros((c, c), F32)
            q_c = q4[cc]
            for lf4, bc4, reverse in zip(lfs, bcs, (False, True)):
                bc_c = bc4[cc]
                cum[...] = bc_c
                kk[...] = 1.0 - jnp.exp2(lf4[cc])

                def col_step(s, a_):
                    d_s = jnp.exp2(jnp.minimum(bc_c - cum[pl.ds(s, 1), :], 0.0))
                    w = jnp.sum(q_c * kk[pl.ds(s, 1), :] * d_s, axis=-1, keepdims=True)
                    return jnp.where(col_c == s, w, a_)
                a_dir = lax.fori_loop(0, c, col_step, jnp.zeros((c, c), F32))
                a = a + jnp.where((col_c >= row_c) if reverse else (col_c <= row_c), a_dir, 0.0)
            crow = pl.ds(pl.multiple_of(ti * TM + cc * c, c), c)
            acc[crow, :] = _dot(a.astype(BF16), v[cc * c:(cc + 1) * c, :])
        return 0

    def tile_group(i, _):
        tiles = [i * unroll + u for u in range(unroll)]
        cums = {0: cumulative(tiles[0])}
        if unroll > 1:
            cums[1] = cumulative(tiles[1])
        ops = {0: tile_operands(cums[0])}
        for u in range(unroll):
            if u + 2 < unroll:
                cums[u + 2] = cumulative(tiles[u + 2])
            if u + 1 < unroll:
                ops[u + 1] = tile_operands(cums[u + 1])
            tile_matmuls(tiles[u], ops[u])
        return 0

    lax.fori_loop(0, n_tiles // unroll, tile_group, 0)

    @pl.when(unsafe)
    def _():
        lax.fori_loop(0, n_tiles, tile_exact, 0)

    def state_step(i, carry):
        s_f, s_b = carry
        cf = i
        cb = jnp.where(i < ctx_chunks, ctx_chunks - 1 - i, n_chunks - 1 + ctx_chunks - i)
        st[cf, :, 0:dk] = s_f.astype(BF16)
        st[cb, :, dk:2 * dk] = s_b.astype(BF16)
        s_f = s_f * ds[cf, 0:1, 0:dk] + xs[cf, :, 0:dk]
        s_b = s_b * ds[cb, 0:1, dk:2 * dk] + xs[cb, :, dk:2 * dk]
        return s_f, s_b

    zero = jnp.zeros((dk, dk), F32)
    lax.fori_loop(0, n_chunks, state_step, (zero, zero))

    def finish_step(ti):
        rows = pl.ds(pl.multiple_of(ti * TM, TM), TM)
        inter = [_dot_nt(qs[pl.ds(pl.multiple_of(ti * TM + cc * c, c), c), :], st[ti * cpt + cc]) for cc in range(cpt)]
        o = acc[rows, :] + jnp.concatenate(inter, axis=0)
        ms = jnp.mean(o * o, axis=-1, keepdims=True)
        o_ref[rows, :] = (o * lax.rsqrt(ms + EPS) * ng_ref[...] * _silu(g_ref[rows, :])).astype(BF16)

    def finish_group(i, _):
        for u in range(unroll):
            finish_step(i * unroll + u)
        return 0

    lax.fori_loop(0, n_tiles // unroll, finish_group, 0)


def _hgrn_scan(tok, q, lff, lfb, v, g, norm_gain):
    n_rows, d = q.shape
    heads = d // LANES
    t = tok.t
    n_chunks = t // HGRN_CHUNK
    spec = pl.BlockSpec((None, t, LANES), lambda b, h: (b, 0, h))
    as3 = lambda a: a.reshape(tok.batch, t, d)
    out = pl.pallas_call(
        functools.partial(_hgrn_scan_kernel, n_ctx=tok.n_ctx),
        grid=(tok.batch, heads),
        in_specs=[spec] * 5 + [pl.BlockSpec((1, LANES), lambda b, h: (0, 0))],
        out_specs=spec,
        out_shape=jax.ShapeDtypeStruct((tok.batch, t, d), BF16),
        scratch_shapes=[pltpu.VMEM((t, LANES), F32), pltpu.VMEM((t, 2 * LANES), BF16),
                        pltpu.VMEM((n_chunks, LANES, 2 * LANES), F32), pltpu.VMEM((n_chunks, LANES, 2 * LANES), BF16),
                        pltpu.VMEM((n_chunks, SUBLANES, 2 * LANES), F32),
                        pltpu.VMEM((HGRN_CHUNK, LANES), F32), pltpu.VMEM((HGRN_CHUNK, LANES), F32)],
        compiler_params=_cparams("parallel", "parallel"),
        name="hgrn_scan",
    )(as3(q), as3(lff), as3(lfb), as3(v), as3(g), norm_gain.reshape(1, LANES))
    return out.reshape(n_rows, d)


def _ffn_kernel(x_ref, o_ref, wo_ref, bo_ref, ng_ref, g1c, g1l, shc, shl, scc, scl, g2c, g2l, wi_ref, wf_ref, y_ref,
                *, tiles_per_batch, n_ctx):
    chunks, pick = _ctx_chunks(x_ref.shape[0], tiles_per_batch, n_ctx)
    mix = _dot(o_ref[...], wo_ref[...])
    x1 = [x_ref[rs, :] + pick(ic, g1c, g1l) * (mix[rs, :] + bo_ref[...]) for rs, ic in chunks]
    h = jnp.concatenate([_norm_mod(x1_c, ng_ref[...], pick(ic, shc, shl), pick(ic, scc, scl)).astype(BF16)
                         for x1_c, (_, ic) in zip(x1, chunks)], axis=0)
    d_ff = wf_ref.shape[0]
    act = []
    for c0 in range(0, d_ff, FF_CHUNK):
        gate = _dot(h, wi_ref[:, c0:c0 + FF_CHUNK])
        up = _dot(h, wi_ref[:, d_ff + c0:d_ff + c0 + FF_CHUNK])
        act.append((_silu(gate) * up).astype(BF16))
    ff = _dot(jnp.concatenate(act, axis=1), wf_ref[...])
    for x1_c, (rs, ic) in zip(x1, chunks):
        y_ref[rs, :] = x1_c + pick(ic, g2c, g2l) * ff[rs, :]


def _resident_layer(stacked, layer):
    return pl.BlockSpec((None,) + stacked.shape[1:], lambda *_: (layer, 0, 0), pipeline_mode=pl.Buffered(1))


def _out_proj_ffn(tok, x_all, o_all, mod, layer, w_o, b_o, norm_gain, w_in_all, w_out_all, lat_only):
    n_rows, d = x_all.shape
    d_ff = w_out_all.shape[1]
    assert d_ff % FF_CHUNK == 0
    rows_b = tok.seq if lat_only else tok.t
    tm = _row_tile(math.gcd(rows_b, tok.n_ctx) if lat_only else rows_b, FFN_MAX_ROWS)
    per_b = rows_b // tm
    first = tok.n_ctx // tm if lat_only else 0
    stream_row = pl.BlockSpec((tm, d), lambda i: ((i // per_b) * (tok.t // tm) + first + i % per_b, 0))
    out_row = pl.BlockSpec((tm, d), lambda i: (i, 0))
    o_row = stream_row if o_all.shape[0] == n_rows else out_row
    mods = [s for chunk in (2, 3, 4, 5) for s in tok.mod_specs(d, layer, chunk, per_b)]
    return pl.pallas_call(
        functools.partial(_ffn_kernel, tiles_per_batch=per_b, n_ctx=0 if lat_only else tok.n_ctx),
        grid=(tok.batch * per_b,),
        in_specs=[stream_row, o_row, _resident(w_o.shape), _vec_spec(d), _vec_spec(d)] + mods
                 + [_resident_layer(w_in_all, layer), _resident_layer(w_out_all, layer)],
        out_specs=out_row,
        out_shape=jax.ShapeDtypeStruct((tok.batch * rows_b, d), F32),
        compiler_params=_cparams("parallel"),
        name="out_proj_ffn",
    )(x_all, o_all, w_o.astype(BF16), b_o.reshape(1, d), norm_gain.reshape(1, d), *([mod] * 8),
      w_in_all, w_out_all)


def kernel(x, c, ctx, c_ctx, w_ada, b_ada, norm_gain, ffn_w_in, ffn_w_out, fnet_w_out, fnet_b_out,
           diff_w_in, diff_q_gain, diff_k_gain, diff_lambda, diff_subln_gain, diff_w_out,
           hgrn_w_in, hgrn_lower_bound, hgrn_norm_gain, hgrn_w_out,
           gqa_w_in, gqa_q_gain, gqa_k_gain, gqa_w_out):
    batch, seq, d = x.shape
    n_ctx = ctx.shape[1]
    depth = w_ada.shape[0]
    assert batch + 1 <= COND_ROWS and d % LANES == 0
    tok = _Tokens(batch, n_ctx, seq)

    cond = jnp.zeros((COND_ROWS, d), F32).at[0].set(c_ctx).at[1:1 + batch].set(c)
    mod = _ada_modulation(cond, w_ada, b_ada)
    x_all = None
    zero_bias = jnp.zeros((d,), F32)
    ffn_w_in_bf16, ffn_w_out_bf16 = ffn_w_in.astype(BF16), ffn_w_out.astype(BF16)

    for i in range(depth):
        m, j = i % N_MIXERS, i // N_MIXERS
        need_ctx = i < depth - 1
        if m == 0:
            if x_all is not None:
                stream = x_all.reshape(batch, tok.t, d)
                ctx, x = stream[:, :n_ctx], stream[:, n_ctx:]
            o, x_all = _fnet_mix(tok, ctx, x, mod, i, norm_gain[i, 0], d // FNET_GROUPS)
            w_o, b_o = fnet_w_out[j], fnet_b_out[j]
        elif m == 1:
            lam_init = 0.8 - 0.6 * math.exp(-0.3 * i)
            qg = diff_q_gain[j].reshape(1, LANES) * (DIFF_HEAD_DIM ** -0.5 * LOG2_E)
            kg = diff_k_gain[j].reshape(1, LANES)
            q, k, v = _qkv_project(tok, x_all, mod, i, norm_gain[i, 0], diff_w_in[j], qg, kg,
                                   DIFF_HEAD_DIM, d, d)
            o = _diff_attention(tok, q, k, v, qg, kg, diff_lambda[j], diff_subln_gain[j], lam_init)
            w_o, b_o = diff_w_out[j], zero_bias
        elif m == 2:
            q, lff, lfb, v, g = _hgrn_project(tok, x_all, mod, i, norm_gain[i, 0], hgrn_w_in[j], hgrn_lower_bound)
            o = _hgrn_scan(tok, q, lff, lfb, v, g, hgrn_norm_gain[j])
            w_o, b_o = hgrn_w_out[j], zero_bias
        else:
            kv = (gqa_w_in.shape[-1] - d) // 2
            qg = gqa_q_gain[j].reshape(1, LANES) * (GQA_HEAD_DIM ** -0.5 * LOG2_E)
            kg = gqa_k_gain[j].reshape(1, LANES)
            q, k, v = _qkv_project(tok, x_all, mod, i, norm_gain[i, 0], gqa_w_in[j], qg, kg, GQA_HEAD_DIM, d, kv)
            o = _gqa_attention(tok, q, k, v, qg, kg, need_ctx)
            w_o, b_o = gqa_w_out[j], zero_bias
        x_all = _out_proj_ffn(tok, x_all, o, mod, i, w_o, b_o, norm_gain[i, 1], ffn_w_in_bf16, ffn_w_out_bf16,
                              lat_only=not need_ctx)
    return x_all.reshape(batch, seq, d)
```

```python
import functools
import math

import numpy as np
import jax
import jax.numpy as jnp
from jax import lax
from jax.experimental import pallas as pl
from jax.experimental.pallas import tpu as pltpu

F32 = jnp.float32
BF16 = jnp.bfloat16

EPS = 1e-6
GRID_W = 64
ROPE_THETA = 10000.0
N_MIXERS = 4

LANES = 128
SUBLANES = 8
BF16_SUBLANES = 16
TM = 256
COND_ROWS = 16
VMEM_LIMIT = 56 * 1024 * 1024

FNET_GROUPS = 8
DIFF_HEAD_DIM = 64
GQA_HEAD_DIM = 128
GQA_GROUP = 4
HGRN_CHUNK = 64
HGRN_HALF = 32
HGRN_TILE_UNROLL = 9
HGRN_SAFE_LOG2_DECAY = -115.0
FF_CHUNK = 256
FFN_MAX_ROWS = 768
PROJ_MAX_ROWS = 768
HGRN_PROJ_MAX_ROWS = 768
MXU_DIM = 256
LOG2_E = math.log2(math.e)
EXP2_SAFE_LOGIT = 100.0
BF16_SLACK = 1.02
DIFF_ATTN_UNROLL = 8
GQA_ATTN_UNROLL = 8


def _cparams(*sem):
    return pltpu.CompilerParams(dimension_semantics=sem, vmem_limit_bytes=VMEM_LIMIT)


def _resident(shape):
    nd = len(shape)
    return pl.BlockSpec(shape, lambda *_: (0,) * nd, pipeline_mode=pl.Buffered(1))


def _silu(x):
    return x * jax.nn.sigmoid(x)


def _norm_mod(x, gain, shift, scale):
    ms = jnp.mean(x * x, axis=-1, keepdims=True)
    y = x * lax.rsqrt(ms + EPS) * gain
    return y * (1.0 + scale) + shift


def _dot(a, b):
    return jnp.dot(a, b, preferred_element_type=F32)


def _dot_nt(a, b):
    return lax.dot_general(a, b, (((1,), (1,)), ((), ())), preferred_element_type=F32)


def _dot_tn(a, b):
    return lax.dot_general(a, b, (((0,), (0,)), ((), ())), preferred_element_type=F32)


def _ada_kernel(cond_ref, w_ref, b_ref, o_ref):
    a = _silu(cond_ref[...]).astype(BF16)
    o_ref[...] = _dot(a, w_ref[...].astype(BF16)) + b_ref[...]


def _ada_modulation(cond, w_ada, b_ada):
    depth, d, n6 = w_ada.shape
    bn = n6 // 4
    out = pl.pallas_call(
        _ada_kernel,
        grid=(depth, n6 // bn),
        in_specs=[
            pl.BlockSpec((COND_ROWS, d), lambda i, j: (0, 0)),
            pl.BlockSpec((None, d, bn), lambda i, j: (i, 0, j)),
            pl.BlockSpec((None, 1, bn), lambda i, j: (i, 0, j)),
        ],
        out_specs=pl.BlockSpec((None, COND_ROWS, bn), lambda i, j: (i, 0, j)),
        out_shape=jax.ShapeDtypeStruct((depth, COND_ROWS, n6), F32),
        compiler_params=_cparams("parallel", "parallel"),
        name="ada_modulation",
    )(cond, w_ada, b_ada.reshape(depth, 1, n6))
    return out.reshape(depth * COND_ROWS * 6, 1, d)


class _Tokens:
    def __init__(self, batch, n_ctx, seq):
        assert n_ctx % TM == 0 and seq % TM == 0
        self.batch, self.n_ctx, self.seq = batch, n_ctx, seq
        self.t = n_ctx + seq
        self.ctx_tiles = n_ctx // TM
        self.tiles = self.t // TM

    def mod_specs(self, d, layer, chunk, tiles_per_batch):
        ctx = pl.BlockSpec((None, 1, d), lambda i: (layer * COND_ROWS * 6 + chunk, 0, 0))
        lat = pl.BlockSpec((None, 1, d), lambda i: ((layer * COND_ROWS + 1 + i // tiles_per_batch) * 6 + chunk, 0, 0))
        return [ctx, lat]


def _ctx_chunks(rows, tiles_per_batch, n_ctx):
    row0 = (pl.program_id(0) % tiles_per_batch) * rows
    chunks = [(slice(r, r + TM), row0 + r < n_ctx) for r in range(0, rows, TM)]
    if n_ctx == 0:
        return chunks, lambda is_ctx, c_ref, l_ref: l_ref[...]
    return chunks, lambda is_ctx, c_ref, l_ref: jnp.where(is_ctx, c_ref[...], l_ref[...])


def _row_tile(n, cap, step=TM):
    return max(r for r in range(step, cap + 1, step) if n % r == 0)


def _vec_spec(n):
    return pl.BlockSpec((1, n), lambda *_: (0, 0))


def _dft_tables(n_ctx, seq, group_dim):
    def cs(n):
        k = np.arange(n, dtype=np.int64)
        ang = 2.0 * np.pi * ((k[:, None] * k[None, :]) % n).astype(np.float64) / n
        return np.cos(ang), np.sin(ang)
    cc, sc = cs(group_dim)
    chan = np.concatenate([cc, sc], axis=1)
    cx, sx = cs(n_ctx)
    pos_ctx = np.concatenate([cx, -sx], axis=1)
    cl, sl = cs(seq)
    pt = _half_spectrum_tile(seq)
    rows = (np.arange(seq // 2 // pt)[:, None] * pt + np.arange(pt + BF16_SUBLANES)[None, :]) % seq
    rev = np.zeros((pt, pt + BF16_SUBLANES), np.float32)
    rev[np.arange(pt), pt - np.arange(pt)] = 1.0
    as_bf16 = lambda a: jnp.asarray(a.astype(np.float32)).astype(BF16)
    return as_bf16(chan), as_bf16(pos_ctx), as_bf16(cl[rows]), as_bf16(sl[rows]), as_bf16(rev)


def _half_spectrum_tile(seq):
    return min(TM, seq // 2)


def _fnet_kernel(ctx_ref, lat_ref, ng_ref, shc_ref, scc_ref, shl_ref, scl_ref, chan_ref, pctx_ref, cos_ref, sin_ref,
                 rev_ref, y_ref, xs_ref, ab_ctx, ab_lat, *, group_dim):
    j = pl.program_id(1)
    n_ctx, d = ctx_ref.shape
    seq = lat_ref.shape[0]
    ctx_tiles = n_ctx // TM
    pt = _half_spectrum_tile(seq)
    p_tiles = seq // 2 // pt

    @pl.when(j == 0)
    def _():
        for src, sh, sc, dst, n in ((ctx_ref, shc_ref, scc_ref, ab_ctx, n_ctx), (lat_ref, shl_ref, scl_ref, ab_lat, seq)):
            for row0 in range(0, n, TM):
                h = _norm_mod(src[row0:row0 + TM, :], ng_ref[...], sh[...], sc[...]).astype(BF16)
                for g in range(d // group_dim):
                    cols = slice(g * group_dim, (g + 1) * group_dim)
                    ab = _dot(h[:, cols], chan_ref[...])
                    dst[row0:row0 + TM, cols] = ab[:, :group_dim].astype(BF16)
                    dst[n + row0:n + row0 + TM, cols] = ab[:, group_dim:].astype(BF16)

    @pl.when(j < ctx_tiles)
    def _():
        rows = pl.ds(pl.multiple_of(j * TM, TM), TM)
        y = _dot(pctx_ref[rows, :], ab_ctx[...]) * (1.0 / math.sqrt(n_ctx * group_dim))
        y_ref[rows, :] = y.astype(BF16)
        xs_ref[...] = ctx_ref[rows, :]

    @pl.when(j >= ctx_tiles)
    def _():
        xs_ref[...] = lat_ref[pl.ds(pl.multiple_of((j - ctx_tiles) * TM, TM), TM), :]

    first_p_step = pl.num_programs(1) - p_tiles

    @pl.when(j >= first_p_step)
    def _():
        tile = j - first_p_step
        scale = 1.0 / math.sqrt(seq * group_dim)
        p_part = _dot(cos_ref[...], ab_lat[0:seq, :])
        q_part = _dot(sin_ref[...], ab_lat[seq:2 * seq, :])
        y_ref[pl.ds(pl.multiple_of(n_ctx + tile * pt, pt), pt), :] = ((p_part - q_part)[0:pt] * scale).astype(BF16)
        mirrored = _dot(rev_ref[...], ((p_part + q_part) * scale).astype(BF16))
        y_ref[pl.ds(pl.multiple_of(n_ctx + seq - (tile + 1) * pt, pt), pt), :] = mirrored.astype(BF16)


def _fnet_mix(tok, ctx, x, mod, layer, norm_gain, group_dim):
    b, t, d = tok.batch, tok.t, x.shape[-1]
    chan, pos_ctx, cos_lat, sin_lat, rev = _dft_tables(tok.n_ctx, tok.seq, group_dim)
    ct, p_tiles = tok.ctx_tiles, cos_lat.shape[0]
    assert p_tiles < tok.tiles

    def mod_spec(chunk, is_ctx):
        return pl.BlockSpec((None, 1, d), lambda i, j: ((layer * COND_ROWS + (0 if is_ctx else 1 + i)) * 6 + chunk, 0, 0))

    table_spec = pl.BlockSpec((None,) + cos_lat.shape[1:],
                              lambda i, j: (jnp.maximum(j - (tok.tiles - p_tiles), 0), 0, 0))
    y, xs = pl.pallas_call(
        functools.partial(_fnet_kernel, group_dim=group_dim),
        grid=(b, tok.tiles),
        in_specs=[
            pl.BlockSpec((None, tok.n_ctx, d), lambda i, j: (i, 0, 0)),
            pl.BlockSpec((None, tok.seq, d), lambda i, j: (i, 0, 0)),
            pl.BlockSpec((1, d), lambda i, j: (0, 0)),
            mod_spec(0, True), mod_spec(1, True), mod_spec(0, False), mod_spec(1, False),
            _resident(chan.shape), _resident(pos_ctx.shape), table_spec, table_spec, _resident(rev.shape),
        ],
        out_specs=[pl.BlockSpec((None, t, d), lambda i, j: (i, 0, 0)),
                   pl.BlockSpec((TM, d), lambda i, j: (i * tok.tiles + j, 0))],
        out_shape=[jax.ShapeDtypeStruct((b, t, d), BF16), jax.ShapeDtypeStruct((b * t, d), F32)],
        scratch_shapes=[pltpu.VMEM((2 * tok.n_ctx, d), BF16), pltpu.VMEM((2 * tok.seq, d), BF16)],
        compiler_params=_cparams("parallel", "arbitrary"),
        name="fnet_mix",
    )(ctx, x, norm_gain.reshape(1, d), mod, mod, mod, mod, chan, pos_ctx, cos_lat, sin_lat, rev)
    return y.reshape(b * t, d), xs


def _rope_tables(tok, head_dim):
    rows = tok.seq // GRID_W
    row = jnp.repeat(jnp.arange(rows, dtype=F32), GRID_W)
    col = jnp.tile(jnp.arange(GRID_W, dtype=F32), rows)
    n_freq = head_dim // 4
    inv_freq = ROPE_THETA ** (-jnp.arange(n_freq, dtype=F32) / n_freq)
    ang = jnp.concatenate([row[:, None] * inv_freq, col[:, None] * inv_freq], axis=-1)
    cos = jnp.concatenate([jnp.ones((tok.n_ctx, head_dim // 2), F32), jnp.cos(ang)], axis=0)
    sin = jnp.concatenate([jnp.zeros((tok.n_ctx, head_dim // 2), F32), jnp.sin(ang)], axis=0)
    reps = LANES // head_dim
    cos = jnp.tile(jnp.concatenate([cos, cos], axis=-1), (1, reps))
    sin = jnp.tile(jnp.concatenate([-sin, sin], axis=-1), (1, reps))
    return cos, sin


def _group_mean_matrix(group, width):
    g = np.arange(width) // group
    return jnp.asarray((g[:, None] == g[None, :]).astype(np.float32) / group).astype(BF16)


def _rope(yn, cos, sin, head_dim):
    half = head_dim // 2
    if head_dim == LANES:
        partner = pltpu.roll(yn, half, 1)
    else:
        lane = lax.broadcasted_iota(jnp.int32, yn.shape, 1)
        partner = jnp.where(lane % head_dim < half, pltpu.roll(yn, LANES - half, 1), pltpu.roll(yn, half, 1))
    return yn * cos + partner * sin


def _qkv_kernel(x_ref, ng_ref, shc, shl, scc, scl, w_ref, qg_ref, kg_ref, cos_ref, sin_ref, gm_ref,
                q_ref, k_ref, vt_ref, *, head_dim, tiles_per_batch, n_ctx):
    chunks, pick = _ctx_chunks(x_ref.shape[0], tiles_per_batch, n_ctx)
    gm = gm_ref[...]
    nq, nk = q_ref.shape[-1], k_ref.shape[-1]

    def normed(rs, ic):
        return _norm_mod(x_ref[rs, :], ng_ref[...], pick(ic, shc, shl), pick(ic, scc, scl)).astype(BF16)

    def project(rs, h):
        cos, sin = cos_ref[rs, :], sin_ref[rs, :]
        for o_ref, g_ref, col0, n in ((q_ref, qg_ref, 0, nq), (k_ref, kg_ref, nq, nk)):
            y = _dot(h, w_ref[:, col0:col0 + n])
            for c0 in range(0, n, MXU_DIM):
                yb = y[:, c0:c0 + MXU_DIM]
                yn = yb * lax.rsqrt(_dot((yb * yb).astype(BF16), gm) + EPS) * g_ref[...]
                for l0 in range(0, MXU_DIM, LANES):
                    o_ref[rs, c0 + l0:c0 + l0 + LANES] = _rope(yn[:, l0:l0 + LANES], cos, sin, head_dim).astype(BF16)
        vt_ref[:, rs] = _dot(h, w_ref[:, nq + nk:]).astype(BF16).T

    h = normed(*chunks[0])
    for k, (rs, _) in enumerate(chunks):
        h_next = normed(*chunks[k + 1]) if k + 1 < len(chunks) else None
        project(rs, h)
        h = h_next


def _qkv_project(tok, x_all, mod, layer, norm_gain, w_in, q_gain, k_gain, head_dim, nq, nk):
    n_rows, d = x_all.shape
    nv = w_in.shape[1] - nq - nk
    assert nq % MXU_DIM == 0 and nk % MXU_DIM == 0
    cos, sin = _rope_tables(tok, head_dim)
    tm = _row_tile(tok.t, PROJ_MAX_ROWS)
    per_b = tok.t // tm
    row_spec = lambda n: pl.BlockSpec((tm, n), lambda i: (i, 0))
    tab_spec = pl.BlockSpec((tm, LANES), lambda i: (i % per_b, 0))
    gain2 = lambda g: jnp.tile(g, (1, MXU_DIM // LANES))
    return pl.pallas_call(
        functools.partial(_qkv_kernel, head_dim=head_dim, tiles_per_batch=per_b, n_ctx=tok.n_ctx),
        grid=(n_rows // tm,),
        in_specs=[row_spec(d), _vec_spec(d)] + tok.mod_specs(d, layer, 0, per_b) + tok.mod_specs(d, layer, 1, per_b)
                 + [_resident(w_in.shape), _vec_spec(MXU_DIM), _vec_spec(MXU_DIM), tab_spec, tab_spec,
                    _resident((MXU_DIM, MXU_DIM))],
        out_specs=[row_spec(nq), row_spec(nk), pl.BlockSpec((None, nv, tm), lambda i: (i // per_b, 0, i % per_b))],
        out_shape=[jax.ShapeDtypeStruct((n_rows, nq), BF16), jax.ShapeDtypeStruct((n_rows, nk), BF16),
                   jax.ShapeDtypeStruct((tok.batch, nv, tok.t), BF16)],
        compiler_params=_cparams("parallel"),
        name=f"qkv_project_hd{head_dim}",
    )(x_all, norm_gain.reshape(1, d), mod, mod, mod, mod, w_in.astype(BF16), gain2(q_gain), gain2(k_gain),
      cos, sin, _group_mean_matrix(head_dim, MXU_DIM))


def _softmax_parts(q, k):
    s = _dot_nt(q, k)
    p = jnp.exp2(s - jnp.max(s, axis=-1, keepdims=True))
    return p, jnp.sum(p, axis=-1, keepdims=True)


def _softmax_parts_bounded(k, q, vt):
    p = jnp.exp2(_dot_nt(k, q))
    l = jnp.sum(p, axis=0, keepdims=True)
    return _dot(vt, p.astype(BF16)) * (1.0 / l)


def _logit_bound(qg_ref, kg_ref, head_dim):
    return BF16_SLACK * head_dim * jnp.max(jnp.abs(qg_ref[...])) * jnp.max(jnp.abs(kg_ref[...]))


def _diff_attn_kernel(lp_ref, sg_ref, qg_ref, kg_ref, q_ref, k_ref, vt, o_ref, *, lam_init, n_ctx):
    lp = lp_ref[...]
    lam = (jnp.exp(jnp.sum(lp[0:1] * lp[1:2], axis=-1, keepdims=True))
           - jnp.exp(jnp.sum(lp[2:3] * lp[3:4], axis=-1, keepdims=True)) + lam_init)
    t = k_ref.shape[0]
    lane = lax.broadcasted_iota(jnp.int32, (TM, LANES), 1)
    bounded = _logit_bound(qg_ref, kg_ref, DIFF_HEAD_DIM) <= EXP2_SAFE_LOGIT

    def split_q(rows):
        q = q_ref[rows, :]
        zero = jnp.zeros_like(q)
        return jnp.where(lane < DIFF_HEAD_DIM, q, zero), jnp.where(lane >= DIFF_HEAD_DIM, q, zero)

    def tile_bounded(rows, nk):
        o12 = _softmax_parts_bounded(k_ref[0:nk, :], jnp.concatenate(split_q(rows), axis=0), vt[:, 0:nk])
        ot = o12[:, :TM] - lam * o12[:, TM:]
        ms = jnp.mean(ot * ot, axis=0, keepdims=True)
        o_ref[rows, :] = ((ot * lax.rsqrt(ms + EPS)).T * sg_ref[...]).astype(BF16)

    def tile_exact(rows, nk):
        k = k_ref[0:nk, :]
        q1, q2 = split_q(rows)
        p1, l1 = _softmax_parts(q1, k)
        p2, l2 = _softmax_parts(q2, k)
        w = p1 * (1.0 / l1) - p2 * (lam / l2)
        o = _dot(w.astype(BF16), vt[:, 0:nk].T)
        ms = jnp.mean(o * o, axis=-1, keepdims=True)
        o_ref[rows, :] = (o * lax.rsqrt(ms + EPS) * sg_ref[...]).astype(BF16)

    @pl.when(bounded)
    def _():
        _for_query_tiles(tile_bounded, n_ctx, t, 0, DIFF_ATTN_UNROLL)

    @pl.when(jnp.logical_not(bounded))
    def _():
        _for_query_tiles(tile_exact, n_ctx, t, 0, 1)


def _for_query_tiles(tile, n_ctx, t, first_row, unroll):
    for r0 in range(first_row, n_ctx, TM):
        tile(pl.ds(r0, TM), n_ctx)
    lat0 = max(first_row, n_ctx)
    n_lat = (t - lat0) // TM
    unroll = unroll if n_lat % unroll == 0 else 1

    def group(i, _):
        for u in range(unroll):
            tile(pl.ds(pl.multiple_of(lat0 + (i * unroll + u) * TM, TM), TM), t)
        return 0

    lax.fori_loop(0, n_lat // unroll, group, 0)


def _diff_attention(tok, q, k, vt, q_gain, k_gain, lam_par, subln_gain, lam_init):
    n_rows, d = q.shape
    heads = d // LANES
    t = tok.t
    as3 = lambda a: a.reshape(tok.batch, t, d)
    spec = pl.BlockSpec((None, t, LANES), lambda b, h: (b, 0, h))
    vt_spec = pl.BlockSpec((None, LANES, t), lambda b, h: (b, h, 0))
    vec = pl.BlockSpec((1, LANES), lambda b, h: (0, 0))
    out = pl.pallas_call(
        functools.partial(_diff_attn_kernel, lam_init=lam_init, n_ctx=tok.n_ctx),
        grid=(tok.batch, heads),
        in_specs=[pl.BlockSpec(lam_par.shape, lambda b, h: (0, 0)), vec, vec, vec, spec, spec, vt_spec],
        out_specs=spec,
        out_shape=jax.ShapeDtypeStruct((tok.batch, t, d), BF16),
        compiler_params=_cparams("parallel", "parallel"),
        name="diff_attention",
    )(lam_par, (subln_gain * (1.0 - lam_init)).reshape(1, LANES), q_gain, k_gain, as3(q), as3(k), vt)
    return out.reshape(n_rows, d)


def _gqa_attn_kernel(qg_ref, kg_ref, q_ref, k_ref, vt, o_ref, *, n_ctx, first_row):
    t = k_ref.shape[0]
    bounded = _logit_bound(qg_ref, kg_ref, GQA_HEAD_DIM) <= EXP2_SAFE_LOGIT

    def out_rows(rows):
        if first_row == 0:
            return rows
        start = rows.start - first_row
        return pl.ds(start if isinstance(start, int) else pl.multiple_of(start, TM), TM)

    def tile_bounded(rows, nk):
        k, vt_k = k_ref[0:nk, :], vt[:, 0:nk]
        for g in range(0, GQA_GROUP, 2):
            q2h = jnp.concatenate([q_ref[rows, g * LANES:(g + 1) * LANES],
                                   q_ref[rows, (g + 1) * LANES:(g + 2) * LANES]], axis=0)
            o2h = _softmax_parts_bounded(k, q2h, vt_k)
            o_ref[out_rows(rows), g * LANES:(g + 1) * LANES] = o2h[:, :TM].T.astype(BF16)
            o_ref[out_rows(rows), (g + 1) * LANES:(g + 2) * LANES] = o2h[:, TM:].T.astype(BF16)

    def tile_exact(rows, nk):
        k, v = k_ref[0:nk, :], vt[:, 0:nk].T
        for g in range(GQA_GROUP):
            cols = slice(g * LANES, (g + 1) * LANES)
            p, l = _softmax_parts(q_ref[rows, cols], k)
            o_ref[out_rows(rows), cols] = (_dot(p.astype(BF16), v) * (1.0 / l)).astype(BF16)

    @pl.when(bounded)
    def _():
        _for_query_tiles(tile_bounded, n_ctx, t, first_row, GQA_ATTN_UNROLL)

    @pl.when(jnp.logical_not(bounded))
    def _():
        _for_query_tiles(tile_exact, n_ctx, t, first_row, 1)


def _gqa_attention(tok, q, k, vt, q_gain, k_gain, need_ctx):
    d = q.shape[1]
    kv_heads = k.shape[1] // LANES
    t = tok.t
    first_row = 0 if need_ctx else tok.n_ctx
    as3 = lambda a: a.reshape(tok.batch, t, a.shape[1])
    k_spec = pl.BlockSpec((None, t, LANES), lambda b, h: (b, 0, h))
    vt_spec = pl.BlockSpec((None, LANES, t), lambda b, h: (b, h, 0))
    q_spec = pl.BlockSpec((None, t, GQA_GROUP * LANES), lambda b, h: (b, 0, h))
    o_spec = pl.BlockSpec((None, t - first_row, GQA_GROUP * LANES), lambda b, h: (b, 0, h))
    vec = pl.BlockSpec((1, LANES), lambda b, h: (0, 0))
    out = pl.pallas_call(
        functools.partial(_gqa_attn_kernel, n_ctx=tok.n_ctx, first_row=first_row),
        grid=(tok.batch, kv_heads),
        in_specs=[vec, vec, q_spec, k_spec, vt_spec],
        out_specs=o_spec,
        out_shape=jax.ShapeDtypeStruct((tok.batch, t - first_row, d), BF16),
        compiler_params=_cparams("parallel", "parallel"),
        name="gqa_attention",
    )(q_gain, k_gain, as3(q), as3(k), vt)
    return out.reshape(tok.batch * (t - first_row), d)


def _hgrn_proj_kernel(x_ref, ng_ref, shc, shl, scc, scl, w_ref, lb_ref, q_ref, lff_ref, lfb_ref, v_ref, g_ref,
                      *, layer_idx, depth, tiles_per_batch, n_ctx):
    d = x_ref.shape[-1]
    chunks, pick = _ctx_chunks(x_ref.shape[0], tiles_per_batch, n_ctx)
    lbs = []
    for direction in range(2):
        rows = [lb_ref[direction * depth + i:direction * depth + i + 1, :] for i in range(depth)]
        m = functools.reduce(jnp.maximum, rows)
        e = [jnp.exp(r - m) for r in rows]
        lbs.append(sum(e[1:layer_idx + 1]) / sum(e) if layer_idx > 0 else jnp.zeros_like(m))

    def normed(rs, ic):
        return _norm_mod(x_ref[rs, :], ng_ref[...], pick(ic, shc, shl), pick(ic, scc, scl)).astype(BF16)

    def project(rs, h):
        q_ref[rs, :] = _silu(_dot(h, w_ref[:, 0:d]))
        for direction, o_ref in enumerate((lff_ref, lfb_ref)):
            z = _dot(h, w_ref[:, (1 + direction) * d:(2 + direction) * d])
            lb = lbs[direction]
            o_ref[rs, :] = jnp.log2(lb + (1.0 - lb) * jax.nn.sigmoid(z))
        v_ref[rs, :] = _dot(h, w_ref[:, 3 * d:4 * d]).astype(BF16)
        g_ref[rs, :] = _dot(h, w_ref[:, 4 * d:5 * d])

    h = normed(*chunks[0])
    for k, (rs, _) in enumerate(chunks):
        h_next = normed(*chunks[k + 1]) if k + 1 < len(chunks) else None
        project(rs, h)
        h = h_next


def _hgrn_project(tok, x_all, mod, layer, norm_gain, w_in, lower_bound):
    n_rows, d = x_all.shape
    depth = lower_bound.shape[1]
    tm = _row_tile(tok.t, HGRN_PROJ_MAX_ROWS)
    per_b = tok.t // tm
    row_spec = pl.BlockSpec((tm, d), lambda i: (i, 0))
    return pl.pallas_call(
        functools.partial(_hgrn_proj_kernel, layer_idx=layer, depth=depth, tiles_per_batch=per_b, n_ctx=tok.n_ctx),
        grid=(n_rows // tm,),
        in_specs=[row_spec, _vec_spec(d)] + tok.mod_specs(d, layer, 0, per_b) + tok.mod_specs(d, layer, 1, per_b)
                 + [_resident(w_in.shape), pl.BlockSpec((2 * depth, d), lambda i: (0, 0))],
        out_specs=[row_spec] * 5,
        out_shape=[jax.ShapeDtypeStruct((n_rows, d), dt) for dt in (F32, F32, F32, BF16, F32)],
        compiler_params=_cparams("parallel"),
        name="hgrn_project",
    )(x_all, norm_gain.reshape(1, d), mod, mod, mod, mod, w_in.astype(BF16), lower_bound.reshape(2 * depth, d))


def _split2(x):
    hi = x.astype(BF16)
    return hi, (x - hi.astype(F32)).astype(BF16)


def _hgrn_scan_kernel(q_ref, lff_ref, lfb_ref, v_ref, g_ref, ng_ref, o_ref,
                      acc, qs, xs, st, ds, cum, kk, *, n_ctx):
    t, dk = q_ref.shape
    c, hh = HGRN_CHUNK, HGRN_HALF
    n_chunks, ctx_chunks = t // c, n_ctx // c
    cpt = TM // c
    row = lax.broadcasted_iota(jnp.int32, (TM, TM), 0)
    col = lax.broadcasted_iota(jnp.int32, (TM, TM), 1)
    same_chunk = (row // c) == (col // c)
    same_half = (row // hh) == (col // hh)
    cross_half = same_chunk & jnp.logical_not(same_half)
    cum_mat = jnp.where(same_chunk & (col <= row), 1.0, 0.0).astype(BF16)
    in_chunk = lax.broadcasted_iota(jnp.int32, (cpt, c, dk), 1)
    row_c = lax.broadcasted_iota(jnp.int32, (c, c), 0)
    col_c = lax.broadcasted_iota(jnp.int32, (c, c), 1)

    def seg_min(ref):
        return jnp.min(ref[...].reshape(t // hh, hh, dk).sum(axis=1))
    unsafe = jnp.minimum(seg_min(lff_ref), seg_min(lfb_ref)) < HGRN_SAFE_LOG2_DECAY

    as4 = lambda a: a.reshape(cpt, c, dk)
    as8 = lambda a: a.reshape(2 * cpt, hh, dk)
    flat = lambda a: a.reshape(TM, dk)
    half_idx = lax.broadcasted_iota(jnp.int32, (2 * cpt, 1, 1), 0)
    per_half = lambda a: jnp.concatenate([a[i // 2:i // 2 + 1] for i in range(2 * cpt)], axis=0)
    n_tiles = t // TM
    unroll = next(u for u in (HGRN_TILE_UNROLL, 3, 2, 1) if n_tiles % u == 0)

    def cumulative(ti):
        rows = pl.ds(pl.multiple_of(ti * TM, TM), TM)
        lf_f, lf_b = lff_ref[rows, :], lfb_ref[rows, :]
        hi, lo = _split2(jnp.concatenate([lf_f, lf_b], axis=1))
        pre = _dot(cum_mat, hi) + _dot(cum_mat, lo)
        pre_b = as4(pre[:, dk:])
        bc_b = pre_b[:, c - 1:c, :] - pre_b + as4(lf_b)
        return rows, (as4(lf_f), as4(lf_b)), (as4(pre[:, :dk]), bc_b)

    def tile_operands(cumulated):
        rows, lfs, bcs = cumulated
        q8, v = as8(q_ref[rows, :]), v_ref[rows, :]
        q_d2, k_d2, q_s, k_bar, decay, q_o, k_o = [], [], [], [], [], [], []
        for lf4, bc4, reverse in zip(lfs, bcs, (False, True)):
            edge, last = (hh, 0) if reverse else (hh - 1, c - 1)
            second = (half_idx % 2 == 0) if reverse else (half_idx % 2 == 1)
            b_last, b_edge = bc4[:, last:last + 1, :], bc4[:, edge:edge + 1, :]
            b_last8, b_edge8 = per_half(b_last), per_half(b_edge)
            bc8, k8 = as8(bc4), 1.0 - jnp.exp2(as8(lf4))
            ref = bc8[:, hh // 2:hh // 2 + 1, :]
            rel = bc8 - ref
            q_d, k_d = q8 * jnp.exp2(rel), k8 * jnp.exp2(-rel)
            q_d2.append(flat(q_d).astype(BF16))
            k_d2.append(flat(k_d).astype(BF16))
            q_s.append(flat(q_d * jnp.exp2(ref)))
            k_bar.append(flat(k_d * jnp.exp2(b_last8 - ref)))
            decay.append(jnp.exp2(b_last))
            q_o.append(flat(q_d * jnp.where(second, jnp.exp2(ref - b_edge8), 0.0)))
            k_o.append(flat(k_d * jnp.where(second, 0.0, jnp.exp2(b_edge8 - ref))))
        stack = lambda parts: jnp.concatenate(parts, axis=1).astype(BF16)
        return rows, v, q_d2, k_d2, stack(q_o), stack(k_o), stack(q_s), stack(k_bar), jnp.concatenate(decay, axis=2)

    def tile_matmuls(ti, operands):
        rows, v, q_d2, k_d2, q_o2, k_o2, q_s2, k_bar2, decay2 = operands
        a_same = (jnp.where(col <= row, _dot_nt(q_d2[0], k_d2[0]), 0.0)
                  + jnp.where(col >= row, _dot_nt(q_d2[1], k_d2[1]), 0.0))
        a = jnp.where(same_half, a_same, jnp.where(cross_half, _dot_nt(q_o2, k_o2), 0.0))
        acc[rows, :] = _dot(a.astype(BF16), v)
        store_state_inputs(ti, rows, v, q_s2, k_bar2, decay2)

    def store_state_inputs(ti, rows, v, q_s2, k_bar2, decay2):
        qs[rows, :] = q_s2
        for cc in range(cpt):
            ci = ti * cpt + cc
            xs[ci] = _dot_tn(v[cc * c:(cc + 1) * c, :], k_bar2[cc * c:(cc + 1) * c, :])
            ds[ci] = jnp.broadcast_to(decay2[cc], ds.shape[1:])

    def tile_exact(ti, _):
        rows, lfs, bcs = cumulative(ti)
        q4, v = as4(q_ref[rows, :]), v_ref[rows, :]
        q_s, k_bar, decay = [], [], []
        for lf4, bc4, reverse in zip(lfs, bcs, (False, True)):
            b_last = bc4[:, 0:1, :] if reverse else bc4[:, c - 1:c, :]
            q_s.append(flat(q4 * jnp.exp2(bc4)))
            k_bar.append(flat((1.0 - jnp.exp2(lf4)) * jnp.exp2(b_last - bc4)))
            decay.append(jnp.exp2(b_last))
        stack = lambda parts: jnp.concatenate(parts, axis=1).astype(BF16)
        store_state_inputs(ti, rows, v, stack(q_s), stack(k_bar), jnp.concatenate(decay, axis=2))
        for cc in range(cpt):
            a = jnp.zeros((c, c), F32)
            q_c = q4[cc]
            for lf4, bc4, reverse in zip(lfs, bcs, (False, True)):
                bc_c = bc4[cc]
                cum[...] = bc_c
                kk[...] = 1.0 - jnp.exp2(lf4[cc])

                def col_step(s, a_):
                    d_s = jnp.exp2(jnp.minimum(bc_c - cum[pl.ds(s, 1), :], 0.0))
                    w = jnp.sum(q_c * kk[pl.ds(s, 1), :] * d_s, axis=-1, keepdims=True)
                    return jnp.where(col_c == s, w, a_)
                a_dir = lax.fori_loop(0, c, col_step, jnp.zeros((c, c), F32))
                a = a + jnp.where((col_c >= row_c) if reverse else (col_c <= row_c), a_dir, 0.0)
            crow = pl.ds(pl.multiple_of(ti * TM + cc * c, c), c)
            acc[crow, :] = _dot(a.astype(BF16), v[cc * c:(cc + 1) * c, :])
        return 0

    def tile_group(i, _):
        tiles = [i * unroll + u for u in range(unroll)]
        cums = {0: cumulative(tiles[0])}
        if unroll > 1:
            cums[1] = cumulative(tiles[1])
        ops = {0: tile_operands(cums[0])}
        for u in range(unroll):
            if u + 2 < unroll:
                cums[u + 2] = cumulative(tiles[u + 2])
            if u + 1 < unroll:
                ops[u + 1] = tile_operands(cums[u + 1])
            tile_matmuls(tiles[u], ops[u])
        return 0

    lax.fori_loop(0, n_tiles // unroll, tile_group, 0)

    @pl.when(unsafe)
    def _():
        lax.fori_loop(0, n_tiles, tile_exact, 0)

    def state_step(i, carry):
        s_f, s_b = carry
        cf = i
        cb = jnp.where(i < ctx_chunks, ctx_chunks - 1 - i, n_chunks - 1 + ctx_chunks - i)
        st[cf, :, 0:dk] = s_f.astype(BF16)
        st[cb, :, dk:2 * dk] = s_b.astype(BF16)
        s_f = s_f * ds[cf, 0:1, 0:dk] + xs[cf, :, 0:dk]
        s_b = s_b * ds[cb, 0:1, dk:2 * dk] + xs[cb, :, dk:2 * dk]
        return s_f, s_b

    zero = jnp.zeros((dk, dk), F32)
    lax.fori_loop(0, n_chunks, state_step, (zero, zero))

    def finish_step(ti):
        rows = pl.ds(pl.multiple_of(ti * TM, TM), TM)
        inter = [_dot_nt(qs[pl.ds(pl.multiple_of(ti * TM + cc * c, c), c), :], st[ti * cpt + cc]) for cc in range(cpt)]
        o = acc[rows, :] + jnp.concatenate(inter, axis=0)
        ms = jnp.mean(o * o, axis=-1, keepdims=True)
        o_ref[rows, :] = (o * lax.rsqrt(ms + EPS) * ng_ref[...] * _silu(g_ref[rows, :])).astype(BF16)

    def finish_group(i, _):
        for u in range(unroll):
            finish_step(i * unroll + u)
        return 0

    lax.fori_loop(0, n_tiles // unroll, finish_group, 0)


def _hgrn_scan(tok, q, lff, lfb, v, g, norm_gain):
    n_rows, d = q.shape
    heads = d // LANES
    t = tok.t
    n_chunks = t // HGRN_CHUNK
    spec = pl.BlockSpec((None, t, LANES), lambda b, h: (b, 0, h))
    as3 = lambda a: a.reshape(tok.batch, t, d)
    out = pl.pallas_call(
        functools.partial(_hgrn_scan_kernel, n_ctx=tok.n_ctx),
        grid=(tok.batch, heads),
        in_specs=[spec] * 5 + [pl.BlockSpec((1, LANES), lambda b, h: (0, 0))],
        out_specs=spec,
        out_shape=jax.ShapeDtypeStruct((tok.batch, t, d), BF16),
        scratch_shapes=[pltpu.VMEM((t, LANES), F32), pltpu.VMEM((t, 2 * LANES), BF16),
                        pltpu.VMEM((n_chunks, LANES, 2 * LANES), F32), pltpu.VMEM((n_chunks, LANES, 2 * LANES), BF16),
                        pltpu.VMEM((n_chunks, SUBLANES, 2 * LANES), F32),
                        pltpu.VMEM((HGRN_CHUNK, LANES), F32), pltpu.VMEM((HGRN_CHUNK, LANES), F32)],
        compiler_params=_cparams("parallel", "parallel"),
        name="hgrn_scan",
    )(as3(q), as3(lff), as3(lfb), as3(v), as3(g), norm_gain.reshape(1, LANES))
    return out.reshape(n_rows, d)


def _ffn_kernel(x_ref, o_ref, wo_ref, bo_ref, ng_ref, g1c, g1l, shc, shl, scc, scl, g2c, g2l, wi_ref, wf_ref, y_ref,
                *, tiles_per_batch, n_ctx):
    chunks, pick = _ctx_chunks(x_ref.shape[0], tiles_per_batch, n_ctx)
    x1 = [x_ref[rs, :] + pick(ic, g1c, g1l) * (_dot(o_ref[rs, :], wo_ref[...]) + bo_ref[...]) for rs, ic in chunks]
    h = jnp.concatenate([_norm_mod(x1_c, ng_ref[...], pick(ic, shc, shl), pick(ic, scc, scl)).astype(BF16)
                         for x1_c, (_, ic) in zip(x1, chunks)], axis=0)
    d_ff = wf_ref.shape[0]
    act = []
    for c0 in range(0, d_ff, FF_CHUNK):
        gate = _dot(h, wi_ref[:, c0:c0 + FF_CHUNK])
        up = _dot(h, wi_ref[:, d_ff + c0:d_ff + c0 + FF_CHUNK])
        act.append((_silu(gate) * up).astype(BF16))
    ff = _dot(jnp.concatenate(act, axis=1), wf_ref[...])
    for x1_c, (rs, ic) in zip(x1, chunks):
        y_ref[rs, :] = x1_c + pick(ic, g2c, g2l) * ff[rs, :]


def _resident_layer(stacked, layer):
    return pl.BlockSpec((None,) + stacked.shape[1:], lambda *_: (layer, 0, 0), pipeline_mode=pl.Buffered(1))


def _out_proj_ffn(tok, x_all, o_all, mod, layer, w_o, b_o, norm_gain, w_in_all, w_out_all, lat_only):
    n_rows, d = x_all.shape
    d_ff = w_out_all.shape[1]
    assert d_ff % FF_CHUNK == 0
    rows_b = tok.seq if lat_only else tok.t
    tm = _row_tile(math.gcd(rows_b, tok.n_ctx) if lat_only else rows_b, FFN_MAX_ROWS)
    per_b = rows_b // tm
    first = tok.n_ctx // tm if lat_only else 0
    stream_row = pl.BlockSpec((tm, d), lambda i: ((i // per_b) * (tok.t // tm) + first + i % per_b, 0))
    out_row = pl.BlockSpec((tm, d), lambda i: (i, 0))
    o_row = stream_row if o_all.shape[0] == n_rows else out_row
    mods = [s for chunk in (2, 3, 4, 5) for s in tok.mod_specs(d, layer, chunk, per_b)]
    return pl.pallas_call(
        functools.partial(_ffn_kernel, tiles_per_batch=per_b, n_ctx=0 if lat_only else tok.n_ctx),
        grid=(tok.batch * per_b,),
        in_specs=[stream_row, o_row, _resident(w_o.shape), _vec_spec(d), _vec_spec(d)] + mods
                 + [_resident_layer(w_in_all, layer), _resident_layer(w_out_all, layer)],
        out_specs=out_row,
        out_shape=jax.ShapeDtypeStruct((tok.batch * rows_b, d), F32),
        compiler_params=_cparams("parallel"),
        name="out_proj_ffn",
    )(x_all, o_all, w_o.astype(BF16), b_o.reshape(1, d), norm_gain.reshape(1, d), *([mod] * 8),
      w_in_all, w_out_all)


def kernel(x, c, ctx, c_ctx, w_ada, b_ada, norm_gain, ffn_w_in, ffn_w_out, fnet_w_out, fnet_b_out,
           diff_w_in, diff_q_gain, diff_k_gain, diff_lambda, diff_subln_gain, diff_w_out,
           hgrn_w_in, hgrn_lower_bound, hgrn_norm_gain, hgrn_w_out,
           gqa_w_in, gqa_q_gain, gqa_k_gain, gqa_w_out):
    batch, seq, d = x.shape
    n_ctx = ctx.shape[1]
    depth = w_ada.shape[0]
    assert batch + 1 <= COND_ROWS and d % LANES == 0
    tok = _Tokens(batch, n_ctx, seq)

    cond = jnp.zeros((COND_ROWS, d), F32).at[0].set(c_ctx).at[1:1 + batch].set(c)
    mod = _ada_modulation(cond, w_ada, b_ada)
    x_all = None
    zero_bias = jnp.zeros((d,), F32)
    ffn_w_in_bf16, ffn_w_out_bf16 = ffn_w_in.astype(BF16), ffn_w_out.astype(BF16)

    for i in range(depth):
        m, j = i % N_MIXERS, i // N_MIXERS
        need_ctx = i < depth - 1
        if m == 0:
            if x_all is not None:
                stream = x_all.reshape(batch, tok.t, d)
                ctx, x = stream[:, :n_ctx], stream[:, n_ctx:]
            o, x_all = _fnet_mix(tok, ctx, x, mod, i, norm_gain[i, 0], d // FNET_GROUPS)
            w_o, b_o = fnet_w_out[j], fnet_b_out[j]
        elif m == 1:
            lam_init = 0.8 - 0.6 * math.exp(-0.3 * i)
            qg = diff_q_gain[j].reshape(1, LANES) * (DIFF_HEAD_DIM ** -0.5 * LOG2_E)
            kg = diff_k_gain[j].reshape(1, LANES)
            q, k, v = _qkv_project(tok, x_all, mod, i, norm_gain[i, 0], diff_w_in[j], qg, kg,
                                   DIFF_HEAD_DIM, d, d)
            o = _diff_attention(tok, q, k, v, qg, kg, diff_lambda[j], diff_subln_gain[j], lam_init)
            w_o, b_o = diff_w_out[j], zero_bias
        elif m == 2:
            q, lff, lfb, v, g = _hgrn_project(tok, x_all, mod, i, norm_gain[i, 0], hgrn_w_in[j], hgrn_lower_bound)
            o = _hgrn_scan(tok, q, lff, lfb, v, g, hgrn_norm_gain[j])
            w_o, b_o = hgrn_w_out[j], zero_bias
        else:
            kv = (gqa_w_in.shape[-1] - d) // 2
            qg = gqa_q_gain[j].reshape(1, LANES) * (GQA_HEAD_DIM ** -0.5 * LOG2_E)
            kg = gqa_k_gain[j].reshape(1, LANES)
            q, k, v = _qkv_project(tok, x_all, mod, i, norm_gain[i, 0], gqa_w_in[j], qg, kg, GQA_HEAD_DIM, d, kv)
            o = _gqa_attention(tok, q, k, v, qg, kg, need_ctx)
            w_o, b_o = gqa_w_out[j], zero_bias
        x_all = _out_proj_ffn(tok, x_all, o, mod, i, w_o, b_o, norm_gain[i, 1], ffn_w_in_bf16, ffn_w_out_bf16,
                              lat_only=not need_ctx)
    return x_all.reshape(batch, seq, d)
```

```python
import functools
import math

import numpy as np
import jax
import jax.numpy as jnp
from jax import lax
from jax.experimental import pallas as pl
from jax.experimental.pallas import tpu as pltpu

F32 = jnp.float32
BF16 = jnp.bfloat16

EPS = 1e-6
GRID_W = 64
ROPE_THETA = 10000.0
N_MIXERS = 4

LANES = 128
SUBLANES = 8
BF16_SUBLANES = 16
TM = 256
COND_ROWS = 16
VMEM_LIMIT = 56 * 1024 * 1024

FNET_GROUPS = 8
DIFF_HEAD_DIM = 64
GQA_HEAD_DIM = 128
GQA_GROUP = 4
HGRN_CHUNK = 64
HGRN_HALF = 32
HGRN_TILE_UNROLL = 9
HGRN_SAFE_LOG2_DECAY = -115.0
FF_CHUNK = 256
FFN_MAX_ROWS = 768
PROJ_MAX_ROWS = 768
HGRN_PROJ_MAX_ROWS = 768
MXU_DIM = 256
LOG2_E = math.log2(math.e)
EXP2_SAFE_LOGIT = 100.0
BF16_SLACK = 1.02
DIFF_ATTN_UNROLL = 8
GQA_ATTN_UNROLL = 8


def _cparams(*sem):
    return pltpu.CompilerParams(dimension_semantics=sem, vmem_limit_bytes=VMEM_LIMIT)


def _resident(shape):
    nd = len(shape)
    return pl.BlockSpec(shape, lambda *_: (0,) * nd, pipeline_mode=pl.Buffered(1))


def _silu(x):
    return x * jax.nn.sigmoid(x)


def _norm_mod(x, gain, shift, scale):
    ms = jnp.mean(x * x, axis=-1, keepdims=True)
    y = x * lax.rsqrt(ms + EPS) * gain
    return y * (1.0 + scale) + shift


def _dot(a, b):
    return jnp.dot(a, b, preferred_element_type=F32)


def _dot_nt(a, b):
    return lax.dot_general(a, b, (((1,), (1,)), ((), ())), preferred_element_type=F32)


def _dot_tn(a, b):
    return lax.dot_general(a, b, (((0,), (0,)), ((), ())), preferred_element_type=F32)


def _ada_kernel(cond_ref, w_ref, b_ref, o_ref):
    a = _silu(cond_ref[...]).astype(BF16)
    o_ref[...] = _dot(a, w_ref[...].astype(BF16)) + b_ref[...]


def _ada_modulation(cond, w_ada, b_ada):
    depth, d, n6 = w_ada.shape
    bn = n6 // 4
    out = pl.pallas_call(
        _ada_kernel,
        grid=(depth, n6 // bn),
        in_specs=[
            pl.BlockSpec((COND_ROWS, d), lambda i, j: (0, 0)),
            pl.BlockSpec((None, d, bn), lambda i, j: (i, 0, j)),
            pl.BlockSpec((None, 1, bn), lambda i, j: (i, 0, j)),
        ],
        out_specs=pl.BlockSpec((None, COND_ROWS, bn), lambda i, j: (i, 0, j)),
        out_shape=jax.ShapeDtypeStruct((depth, COND_ROWS, n6), F32),
        compiler_params=_cparams("parallel", "parallel"),
        name="ada_modulation",
    )(cond, w_ada, b_ada.reshape(depth, 1, n6))
    return out.reshape(depth * COND_ROWS * 6, 1, d)


class _Tokens:
    def __init__(self, batch, n_ctx, seq):
        assert n_ctx % TM == 0 and seq % TM == 0
        self.batch, self.n_ctx, self.seq = batch, n_ctx, seq
        self.t = n_ctx + seq
        self.ctx_tiles = n_ctx // TM
        self.tiles = self.t // TM

    def mod_specs(self, d, layer, chunk, tiles_per_batch):
        ctx = pl.BlockSpec((None, 1, d), lambda i: (layer * COND_ROWS * 6 + chunk, 0, 0))
        lat = pl.BlockSpec((None, 1, d), lambda i: ((layer * COND_ROWS + 1 + i // tiles_per_batch) * 6 + chunk, 0, 0))
        return [ctx, lat]


def _ctx_chunks(rows, tiles_per_batch, n_ctx):
    row0 = (pl.program_id(0) % tiles_per_batch) * rows
    chunks = [(slice(r, r + TM), row0 + r < n_ctx) for r in range(0, rows, TM)]
    if n_ctx == 0:
        return chunks, lambda is_ctx, c_ref, l_ref: l_ref[...]
    return chunks, lambda is_ctx, c_ref, l_ref: jnp.where(is_ctx, c_ref[...], l_ref[...])


def _row_tile(n, cap, step=TM):
    return max(r for r in range(step, cap + 1, step) if n % r == 0)


def _vec_spec(n):
    return pl.BlockSpec((1, n), lambda *_: (0, 0))


class _SideCasts:
    def __init__(self, stacks, layer, grid):
        n0, n1 = grid
        self.stacks, self.in_specs, self.out_specs, self.out_shapes = list(stacks), [], [], []
        for w in self.stacks:
            _, r, c = w.shape
            rows = r // (n0 * n1)
            assert r % (n0 * n1) == 0 and rows % BF16_SUBLANES == 0 and c % LANES == 0
            self.in_specs.append(pl.BlockSpec((None, rows, c), lambda a, b: (layer, a * n1 + b, 0)))
            self.out_specs.append(pl.BlockSpec((rows, c), lambda a, b: (a * n1 + b, 0)))
            self.out_shapes.append(jax.ShapeDtypeStruct((r, c), BF16))

    def wrap(self, kernel_fn, n_in, n_out):
        n = len(self.stacks)

        def wrapped(*refs):
            for src, dst in zip(refs[n_in:n_in + n], refs[n_in + n + n_out:n_in + 2 * n + n_out]):
                dst[...] = src[...].astype(BF16)
            kernel_fn(*refs[:n_in], *refs[n_in + n:n_in + n + n_out], *refs[n_in + 2 * n + n_out:])
        return wrapped


def _dft_tables(n_ctx, seq, group_dim):
    def cs(n):
        k = np.arange(n, dtype=np.int64)
        ang = 2.0 * np.pi * ((k[:, None] * k[None, :]) % n).astype(np.float64) / n
        return np.cos(ang), np.sin(ang)
    cc, sc = cs(group_dim)
    chan = np.concatenate([cc, sc], axis=1)
    cx, sx = cs(n_ctx)
    pos_ctx = np.concatenate([cx, -sx], axis=1)
    cl, sl = cs(seq)
    pt = _half_spectrum_tile(seq)
    rows = (np.arange(seq // 2 // pt)[:, None] * pt + np.arange(pt + BF16_SUBLANES)[None, :]) % seq
    rev = np.zeros((pt, pt + BF16_SUBLANES), np.float32)
    rev[np.arange(pt), pt - np.arange(pt)] = 1.0
    as_bf16 = lambda a: jnp.asarray(a.astype(np.float32)).astype(BF16)
    return as_bf16(chan), as_bf16(pos_ctx), as_bf16(cl[rows]), as_bf16(sl[rows]), as_bf16(rev)


def _half_spectrum_tile(seq):
    return min(TM, seq // 2)


def _fnet_kernel(ctx_ref, lat_ref, ng_ref, shc_ref, scc_ref, shl_ref, scl_ref, chan_ref, pctx_ref, cos_ref, sin_ref,
                 rev_ref, y_ref, xs_ref, ab_ctx, ab_lat, *, group_dim):
    j = pl.program_id(1)
    n_ctx, d = ctx_ref.shape
    seq = lat_ref.shape[0]
    ctx_tiles = n_ctx // TM
    pt = _half_spectrum_tile(seq)
    p_tiles = seq // 2 // pt

    @pl.when(j == 0)
    def _():
        for src, sh, sc, dst, n in ((ctx_ref, shc_ref, scc_ref, ab_ctx, n_ctx), (lat_ref, shl_ref, scl_ref, ab_lat, seq)):
            for row0 in range(0, n, TM):
                h = _norm_mod(src[row0:row0 + TM, :], ng_ref[...], sh[...], sc[...]).astype(BF16)
                for g in range(d // group_dim):
                    cols = slice(g * group_dim, (g + 1) * group_dim)
                    ab = _dot(h[:, cols], chan_ref[...])
                    dst[row0:row0 + TM, cols] = ab[:, :group_dim].astype(BF16)
                    dst[n + row0:n + row0 + TM, cols] = ab[:, group_dim:].astype(BF16)

    @pl.when(j < ctx_tiles)
    def _():
        rows = pl.ds(pl.multiple_of(j * TM, TM), TM)
        y = _dot(pctx_ref[rows, :], ab_ctx[...]) * (1.0 / math.sqrt(n_ctx * group_dim))
        y_ref[rows, :] = y.astype(BF16)
        xs_ref[...] = ctx_ref[rows, :]

    @pl.when(j >= ctx_tiles)
    def _():
        xs_ref[...] = lat_ref[pl.ds(pl.multiple_of((j - ctx_tiles) * TM, TM), TM), :]

    first_p_step = pl.num_programs(1) - p_tiles

    @pl.when(j >= first_p_step)
    def _():
        tile = j - first_p_step
        scale = 1.0 / math.sqrt(seq * group_dim)
        p_part = _dot(cos_ref[...], ab_lat[0:seq, :])
        q_part = _dot(sin_ref[...], ab_lat[seq:2 * seq, :])
        y_ref[pl.ds(pl.multiple_of(n_ctx + tile * pt, pt), pt), :] = ((p_part - q_part)[0:pt] * scale).astype(BF16)
        mirrored = _dot(rev_ref[...], ((p_part + q_part) * scale).astype(BF16))
        y_ref[pl.ds(pl.multiple_of(n_ctx + seq - (tile + 1) * pt, pt), pt), :] = mirrored.astype(BF16)


def _fnet_mix(tok, ctx, x, mod, layer, norm_gain, group_dim):
    b, t, d = tok.batch, tok.t, x.shape[-1]
    chan, pos_ctx, cos_lat, sin_lat, rev = _dft_tables(tok.n_ctx, tok.seq, group_dim)
    ct, p_tiles = tok.ctx_tiles, cos_lat.shape[0]
    assert p_tiles < tok.tiles

    def mod_spec(chunk, is_ctx):
        return pl.BlockSpec((None, 1, d), lambda i, j: ((layer * COND_ROWS + (0 if is_ctx else 1 + i)) * 6 + chunk, 0, 0))

    table_spec = pl.BlockSpec((None,) + cos_lat.shape[1:],
                              lambda i, j: (jnp.maximum(j - (tok.tiles - p_tiles), 0), 0, 0))
    y, xs = pl.pallas_call(
        functools.partial(_fnet_kernel, group_dim=group_dim),
        grid=(b, tok.tiles),
        in_specs=[
            pl.BlockSpec((None, tok.n_ctx, d), lambda i, j: (i, 0, 0)),
            pl.BlockSpec((None, tok.seq, d), lambda i, j: (i, 0, 0)),
            pl.BlockSpec((1, d), lambda i, j: (0, 0)),
            mod_spec(0, True), mod_spec(1, True), mod_spec(0, False), mod_spec(1, False),
            _resident(chan.shape), _resident(pos_ctx.shape), table_spec, table_spec, _resident(rev.shape),
        ],
        out_specs=[pl.BlockSpec((None, t, d), lambda i, j: (i, 0, 0)),
                   pl.BlockSpec((TM, d), lambda i, j: (i * tok.tiles + j, 0))],
        out_shape=[jax.ShapeDtypeStruct((b, t, d), BF16), jax.ShapeDtypeStruct((b * t, d), F32)],
        scratch_shapes=[pltpu.VMEM((2 * tok.n_ctx, d), BF16), pltpu.VMEM((2 * tok.seq, d), BF16)],
        compiler_params=_cparams("parallel", "arbitrary"),
        name="fnet_mix",
    )(ctx, x, norm_gain.reshape(1, d), mod, mod, mod, mod, chan, pos_ctx, cos_lat, sin_lat, rev)
    return y.reshape(b * t, d), xs


def _rope_tables(tok, head_dim):
    rows = tok.seq // GRID_W
    row = jnp.repeat(jnp.arange(rows, dtype=F32), GRID_W)
    col = jnp.tile(jnp.arange(GRID_W, dtype=F32), rows)
    n_freq = head_dim // 4
    inv_freq = ROPE_THETA ** (-jnp.arange(n_freq, dtype=F32) / n_freq)
    ang = jnp.concatenate([row[:, None] * inv_freq, col[:, None] * inv_freq], axis=-1)
    cos = jnp.concatenate([jnp.ones((tok.n_ctx, head_dim // 2), F32), jnp.cos(ang)], axis=0)
    sin = jnp.concatenate([jnp.zeros((tok.n_ctx, head_dim // 2), F32), jnp.sin(ang)], axis=0)
    reps = LANES // head_dim
    cos = jnp.tile(jnp.concatenate([cos, cos], axis=-1), (1, reps))
    sin = jnp.tile(jnp.concatenate([-sin, sin], axis=-1), (1, reps))
    return cos, sin


def _group_mean_matrix(group, width):
    g = np.arange(width) // group
    return jnp.asarray((g[:, None] == g[None, :]).astype(np.float32) / group).astype(BF16)


def _rope(yn, cos, sin, head_dim):
    half = head_dim // 2
    if head_dim == LANES:
        partner = pltpu.roll(yn, half, 1)
    else:
        lane = lax.broadcasted_iota(jnp.int32, yn.shape, 1)
        partner = jnp.where(lane % head_dim < half, pltpu.roll(yn, LANES - half, 1), pltpu.roll(yn, half, 1))
    return yn * cos + partner * sin


def _qkv_kernel(x_ref, ng_ref, shc, shl, scc, scl, w_ref, qg_ref, kg_ref, cos_ref, sin_ref, gm_ref,
                q_ref, k_ref, vt_ref, *, head_dim, tiles_per_batch, n_ctx):
    chunks, pick = _ctx_chunks(x_ref.shape[0], tiles_per_batch, n_ctx)
    h = jnp.concatenate([_norm_mod(x_ref[rs, :], ng_ref[...], pick(ic, shc, shl), pick(ic, scc, scl)).astype(BF16)
                         for rs, ic in chunks], axis=0)
    cos, sin, gm = cos_ref[...], sin_ref[...], gm_ref[...]
    nq, nk = q_ref.shape[-1], k_ref.shape[-1]
    for o_ref, g_ref, col0, n in ((q_ref, qg_ref, 0, nq), (k_ref, kg_ref, nq, nk)):
        y = _dot(h, w_ref[:, col0:col0 + n])
        for c0 in range(0, n, MXU_DIM):
            yb = y[:, c0:c0 + MXU_DIM]
            yn = yb * lax.rsqrt(_dot((yb * yb).astype(BF16), gm) + EPS) * g_ref[...]
            for l0 in range(0, MXU_DIM, LANES):
                o_ref[:, c0 + l0:c0 + l0 + LANES] = _rope(yn[:, l0:l0 + LANES], cos, sin, head_dim).astype(BF16)
    vt_ref[...] = _dot(h, w_ref[:, nq + nk:]).astype(BF16).T


def _qkv_project(tok, x_all, mod, layer, norm_gain, w_in, q_gain, k_gain, head_dim, nq, nk):
    n_rows, d = x_all.shape
    nv = w_in.shape[1] - nq - nk
    assert nq % MXU_DIM == 0 and nk % MXU_DIM == 0
    cos, sin = _rope_tables(tok, head_dim)
    tm = _row_tile(tok.t, PROJ_MAX_ROWS)
    per_b = tok.t // tm
    row_spec = lambda n: pl.BlockSpec((tm, n), lambda i: (i, 0))
    tab_spec = pl.BlockSpec((tm, LANES), lambda i: (i % per_b, 0))
    gain2 = lambda g: jnp.tile(g, (1, MXU_DIM // LANES))
    return pl.pallas_call(
        functools.partial(_qkv_kernel, head_dim=head_dim, tiles_per_batch=per_b, n_ctx=tok.n_ctx),
        grid=(n_rows // tm,),
        in_specs=[row_spec(d), _vec_spec(d)] + tok.mod_specs(d, layer, 0, per_b) + tok.mod_specs(d, layer, 1, per_b)
                 + [_resident(w_in.shape), _vec_spec(MXU_DIM), _vec_spec(MXU_DIM), tab_spec, tab_spec,
                    _resident((MXU_DIM, MXU_DIM))],
        out_specs=[row_spec(nq), row_spec(nk), pl.BlockSpec((None, nv, tm), lambda i: (i // per_b, 0, i % per_b))],
        out_shape=[jax.ShapeDtypeStruct((n_rows, nq), BF16), jax.ShapeDtypeStruct((n_rows, nk), BF16),
                   jax.ShapeDtypeStruct((tok.batch, nv, tok.t), BF16)],
        compiler_params=_cparams("parallel"),
        name=f"qkv_project_hd{head_dim}",
    )(x_all, norm_gain.reshape(1, d), mod, mod, mod, mod, w_in.astype(BF16), gain2(q_gain), gain2(k_gain),
      cos, sin, _group_mean_matrix(head_dim, MXU_DIM))


def _softmax_parts(q, k):
    s = _dot_nt(q, k)
    p = jnp.exp2(s - jnp.max(s, axis=-1, keepdims=True))
    return p, jnp.sum(p, axis=-1, keepdims=True)


def _softmax_parts_bounded(k, q, vt):
    p = jnp.exp2(_dot_nt(k, q))
    l = jnp.sum(p, axis=0, keepdims=True)
    return _dot(vt, p.astype(BF16)) * (1.0 / l)


def _logit_bound(qg_ref, kg_ref, head_dim):
    return BF16_SLACK * head_dim * jnp.max(jnp.abs(qg_ref[...])) * jnp.max(jnp.abs(kg_ref[...]))


def _diff_attn_kernel(lp_ref, sg_ref, qg_ref, kg_ref, q_ref, k_ref, vt, o_ref, *, lam_init, n_ctx):
    lp = lp_ref[...]
    lam = (jnp.exp(jnp.sum(lp[0:1] * lp[1:2], axis=-1, keepdims=True))
           - jnp.exp(jnp.sum(lp[2:3] * lp[3:4], axis=-1, keepdims=True)) + lam_init)
    t = k_ref.shape[0]
    lane = lax.broadcasted_iota(jnp.int32, (TM, LANES), 1)
    bounded = _logit_bound(qg_ref, kg_ref, DIFF_HEAD_DIM) <= EXP2_SAFE_LOGIT

    def split_q(rows):
        q = q_ref[rows, :]
        zero = jnp.zeros_like(q)
        return jnp.where(lane < DIFF_HEAD_DIM, q, zero), jnp.where(lane >= DIFF_HEAD_DIM, q, zero)

    def tile_bounded(rows, nk):
        o12 = _softmax_parts_bounded(k_ref[0:nk, :], jnp.concatenate(split_q(rows), axis=0), vt[:, 0:nk])
        ot = o12[:, :TM] - lam * o12[:, TM:]
        ms = jnp.mean(ot * ot, axis=0, keepdims=True)
        o_ref[rows, :] = ((ot * lax.rsqrt(ms + EPS)).T * sg_ref[...]).astype(BF16)

    def tile_exact(rows, nk):
        k = k_ref[0:nk, :]
        q1, q2 = split_q(rows)
        p1, l1 = _softmax_parts(q1, k)
        p2, l2 = _softmax_parts(q2, k)
        w = p1 * (1.0 / l1) - p2 * (lam / l2)
        o = _dot(w.astype(BF16), vt[:, 0:nk].T)
        ms = jnp.mean(o * o, axis=-1, keepdims=True)
        o_ref[rows, :] = (o * lax.rsqrt(ms + EPS) * sg_ref[...]).astype(BF16)

    @pl.when(bounded)
    def _():
        _for_query_tiles(tile_bounded, n_ctx, t, 0, DIFF_ATTN_UNROLL)

    @pl.when(jnp.logical_not(bounded))
    def _():
        _for_query_tiles(tile_exact, n_ctx, t, 0, 1)


def _for_query_tiles(tile, n_ctx, t, first_row, unroll):
    for r0 in range(first_row, n_ctx, TM):
        tile(pl.ds(r0, TM), n_ctx)
    lat0 = max(first_row, n_ctx)
    n_lat = (t - lat0) // TM
    unroll = unroll if n_lat % unroll == 0 else 1

    def group(i, _):
        for u in range(unroll):
            tile(pl.ds(pl.multiple_of(lat0 + (i * unroll + u) * TM, TM), TM), t)
        return 0

    lax.fori_loop(0, n_lat // unroll, group, 0)


def _diff_attention(tok, q, k, vt, q_gain, k_gain, lam_par, subln_gain, lam_init, cast_stacks, cast_layer):
    n_rows, d = q.shape
    heads = d // LANES
    t = tok.t
    grid = (tok.batch, heads)
    side = _SideCasts(cast_stacks, cast_layer, grid)
    as3 = lambda a: a.reshape(tok.batch, t, d)
    spec = pl.BlockSpec((None, t, LANES), lambda b, h: (b, 0, h))
    vt_spec = pl.BlockSpec((None, LANES, t), lambda b, h: (b, h, 0))
    vec = pl.BlockSpec((1, LANES), lambda b, h: (0, 0))
    in_specs = [pl.BlockSpec(lam_par.shape, lambda b, h: (0, 0)), vec, vec, vec, spec, spec, vt_spec]
    out, *casts = pl.pallas_call(
        side.wrap(functools.partial(_diff_attn_kernel, lam_init=lam_init, n_ctx=tok.n_ctx), len(in_specs), 1),
        grid=grid,
        in_specs=in_specs + side.in_specs,
        out_specs=[spec] + side.out_specs,
        out_shape=[jax.ShapeDtypeStruct((tok.batch, t, d), BF16)] + side.out_shapes,
        compiler_params=_cparams("parallel", "parallel"),
        name="diff_attention",
    )(lam_par, (subln_gain * (1.0 - lam_init)).reshape(1, LANES), q_gain, k_gain, as3(q), as3(k), vt, *side.stacks)
    return out.reshape(n_rows, d), casts


def _gqa_attn_kernel(qg_ref, kg_ref, q_ref, k_ref, vt, o_ref, *, n_ctx, first_row):
    t = k_ref.shape[0]
    bounded = _logit_bound(qg_ref, kg_ref, GQA_HEAD_DIM) <= EXP2_SAFE_LOGIT

    def out_rows(rows):
        if first_row == 0:
            return rows
        start = rows.start - first_row
        return pl.ds(start if isinstance(start, int) else pl.multiple_of(start, TM), TM)

    def tile_bounded(rows, nk):
        k, vt_k = k_ref[0:nk, :], vt[:, 0:nk]
        for g in range(0, GQA_GROUP, 2):
            q2h = jnp.concatenate([q_ref[rows, g * LANES:(g + 1) * LANES],
                                   q_ref[rows, (g + 1) * LANES:(g + 2) * LANES]], axis=0)
            o2h = _softmax_parts_bounded(k, q2h, vt_k)
            o_ref[out_rows(rows), g * LANES:(g + 1) * LANES] = o2h[:, :TM].T.astype(BF16)
            o_ref[out_rows(rows), (g + 1) * LANES:(g + 2) * LANES] = o2h[:, TM:].T.astype(BF16)

    def tile_exact(rows, nk):
        k, v = k_ref[0:nk, :], vt[:, 0:nk].T
        for g in range(GQA_GROUP):
            cols = slice(g * LANES, (g + 1) * LANES)
            p, l = _softmax_parts(q_ref[rows, cols], k)
            o_ref[out_rows(rows), cols] = (_dot(p.astype(BF16), v) * (1.0 / l)).astype(BF16)

    @pl.when(bounded)
    def _():
        _for_query_tiles(tile_bounded, n_ctx, t, first_row, GQA_ATTN_UNROLL)

    @pl.when(jnp.logical_not(bounded))
    def _():
        _for_query_tiles(tile_exact, n_ctx, t, first_row, 1)


def _gqa_attention(tok, q, k, vt, q_gain, k_gain, need_ctx, cast_stacks, cast_layer):
    d = q.shape[1]
    kv_heads = k.shape[1] // LANES
    t = tok.t
    grid = (tok.batch, kv_heads)
    side = _SideCasts(cast_stacks, cast_layer, grid)
    first_row = 0 if need_ctx else tok.n_ctx
    as3 = lambda a: a.reshape(tok.batch, t, a.shape[1])
    k_spec = pl.BlockSpec((None, t, LANES), lambda b, h: (b, 0, h))
    vt_spec = pl.BlockSpec((None, LANES, t), lambda b, h: (b, h, 0))
    q_spec = pl.BlockSpec((None, t, GQA_GROUP * LANES), lambda b, h: (b, 0, h))
    o_spec = pl.BlockSpec((None, t - first_row, GQA_GROUP * LANES), lambda b, h: (b, 0, h))
    vec = pl.BlockSpec((1, LANES), lambda b, h: (0, 0))
    in_specs = [vec, vec, q_spec, k_spec, vt_spec]
    out, *casts = pl.pallas_call(
        side.wrap(functools.partial(_gqa_attn_kernel, n_ctx=tok.n_ctx, first_row=first_row), len(in_specs), 1),
        grid=grid,
        in_specs=in_specs + side.in_specs,
        out_specs=[o_spec] + side.out_specs,
        out_shape=[jax.ShapeDtypeStruct((tok.batch, t - first_row, d), BF16)] + side.out_shapes,
        compiler_params=_cparams("parallel", "parallel"),
        name="gqa_attention",
    )(q_gain, k_gain, as3(q), as3(k), vt, *side.stacks)
    return out.reshape(tok.batch * (t - first_row), d), casts


def _hgrn_proj_kernel(x_ref, ng_ref, shc, shl, scc, scl, w_ref, lb_ref, q_ref, lff_ref, lfb_ref, v_ref, g_ref,
                      *, layer_idx, depth, tiles_per_batch, n_ctx):
    d = x_ref.shape[-1]
    chunks, pick = _ctx_chunks(x_ref.shape[0], tiles_per_batch, n_ctx)
    lbs = []
    for direction in range(2):
        rows = [lb_ref[direction * depth + i:direction * depth + i + 1, :] for i in range(depth)]
        m = functools.reduce(jnp.maximum, rows)
        e = [jnp.exp(r - m) for r in rows]
        lbs.append(sum(e[1:layer_idx + 1]) / sum(e) if layer_idx > 0 else jnp.zeros_like(m))

    def normed(rs, ic):
        return _norm_mod(x_ref[rs, :], ng_ref[...], pick(ic, shc, shl), pick(ic, scc, scl)).astype(BF16)

    def project(rs, h):
        q_ref[rs, :] = _silu(_dot(h, w_ref[:, 0:d]))
        for direction, o_ref in enumerate((lff_ref, lfb_ref)):
            z = _dot(h, w_ref[:, (1 + direction) * d:(2 + direction) * d])
            lb = lbs[direction]
            o_ref[rs, :] = jnp.log2(lb + (1.0 - lb) * jax.nn.sigmoid(z))
        v_ref[rs, :] = _dot(h, w_ref[:, 3 * d:4 * d]).astype(BF16)
        g_ref[rs, :] = _dot(h, w_ref[:, 4 * d:5 * d])

    h = normed(*chunks[0])
    for k, (rs, _) in enumerate(chunks):
        h_next = normed(*chunks[k + 1]) if k + 1 < len(chunks) else None
        project(rs, h)
        h = h_next


def _hgrn_project(tok, x_all, mod, layer, norm_gain, w_in, lower_bound):
    n_rows, d = x_all.shape
    depth = lower_bound.shape[1]
    tm = _row_tile(tok.t, HGRN_PROJ_MAX_ROWS)
    per_b = tok.t // tm
    row_spec = pl.BlockSpec((tm, d), lambda i: (i, 0))
    return pl.pallas_call(
        functools.partial(_hgrn_proj_kernel, layer_idx=layer, depth=depth, tiles_per_batch=per_b, n_ctx=tok.n_ctx),
        grid=(n_rows // tm,),
        in_specs=[row_spec, _vec_spec(d)] + tok.mod_specs(d, layer, 0, per_b) + tok.mod_specs(d, layer, 1, per_b)
                 + [_resident(w_in.shape), pl.BlockSpec((2 * depth, d), lambda i: (0, 0))],
        out_specs=[row_spec] * 5,
        out_shape=[jax.ShapeDtypeStruct((n_rows, d), dt) for dt in (F32, F32, F32, BF16, F32)],
        compiler_params=_cparams("parallel"),
        name="hgrn_project",
    )(x_all, norm_gain.reshape(1, d), mod, mod, mod, mod, w_in.astype(BF16), lower_bound.reshape(2 * depth, d))


def _split2(x):
    hi = x.astype(BF16)
    return hi, (x - hi.astype(F32)).astype(BF16)


def _hgrn_scan_kernel(q_ref, lff_ref, lfb_ref, v_ref, g_ref, ng_ref, o_ref,
                      acc, qs, xs, st, ds, cum, kk, *, n_ctx):
    t, dk = q_ref.shape
    c, hh = HGRN_CHUNK, HGRN_HALF
    n_chunks, ctx_chunks = t // c, n_ctx // c
    cpt = TM // c
    row = lax.broadcasted_iota(jnp.int32, (TM, TM), 0)
    col = lax.broadcasted_iota(jnp.int32, (TM, TM), 1)
    same_chunk = (row // c) == (col // c)
    same_half = (row // hh) == (col // hh)
    cross_half = same_chunk & jnp.logical_not(same_half)
    cum_mat = jnp.where(same_chunk & (col <= row), 1.0, 0.0).astype(BF16)
    in_chunk = lax.broadcasted_iota(jnp.int32, (cpt, c, dk), 1)
    row_c = lax.broadcasted_iota(jnp.int32, (c, c), 0)
    col_c = lax.broadcasted_iota(jnp.int32, (c, c), 1)

    def seg_min(ref):
        return jnp.min(ref[...].reshape(t // hh, hh, dk).sum(axis=1))
    unsafe = jnp.minimum(seg_min(lff_ref), seg_min(lfb_ref)) < HGRN_SAFE_LOG2_DECAY

    as4 = lambda a: a.reshape(cpt, c, dk)
    as8 = lambda a: a.reshape(2 * cpt, hh, dk)
    flat = lambda a: a.reshape(TM, dk)
    half_idx = lax.broadcasted_iota(jnp.int32, (2 * cpt, 1, 1), 0)
    per_half = lambda a: jnp.concatenate([a[i // 2:i // 2 + 1] for i in range(2 * cpt)], axis=0)
    n_tiles = t // TM
    unroll = next(u for u in (HGRN_TILE_UNROLL, 3, 2, 1) if n_tiles % u == 0)

    def cumulative(ti):
        rows = pl.ds(pl.multiple_of(ti * TM, TM), TM)
        lf_f, lf_b = lff_ref[rows, :], lfb_ref[rows, :]
        hi, lo = _split2(jnp.concatenate([lf_f, lf_b], axis=1))
        pre = _dot(cum_mat, hi) + _dot(cum_mat, lo)
        pre_b = as4(pre[:, dk:])
        bc_b = pre_b[:, c - 1:c, :] - pre_b + as4(lf_b)
        return rows, (as4(lf_f), as4(lf_b)), (as4(pre[:, :dk]), bc_b)

    def tile_operands(cumulated):
        rows, lfs, bcs = cumulated
        q8, v = as8(q_ref[rows, :]), v_ref[rows, :]
        q_d2, k_d2, q_s, k_bar, decay, q_o, k_o = [], [], [], [], [], [], []
        for lf4, bc4, reverse in zip(lfs, bcs, (False, True)):
            edge, last = (hh, 0) if reverse else (hh - 1, c - 1)
            second = (half_idx % 2 == 0) if reverse else (half_idx % 2 == 1)
            b_last, b_edge = bc4[:, last:last + 1, :], bc4[:, edge:edge + 1, :]
            b_last8, b_edge8 = per_half(b_last), per_half(b_edge)
            bc8, k8 = as8(bc4), 1.0 - jnp.exp2(as8(lf4))
            ref = bc8[:, hh // 2:hh // 2 + 1, :]
            rel = bc8 - ref
            q_d, k_d = q8 * jnp.exp2(rel), k8 * jnp.exp2(-rel)
            q_d2.append(flat(q_d).astype(BF16))
            k_d2.append(flat(k_d).astype(BF16))
            q_s.append(flat(q_d * jnp.exp2(ref)))
            k_bar.append(flat(k_d * jnp.exp2(b_last8 - ref)))
            decay.append(jnp.exp2(b_last))
            q_o.append(flat(q_d * jnp.where(second, jnp.exp2(ref - b_edge8), 0.0)))
            k_o.append(flat(k_d * jnp.where(second, 0.0, jnp.exp2(b_edge8 - ref))))
        stack = lambda parts: jnp.concatenate(parts, axis=1).astype(BF16)
        return rows, v, q_d2, k_d2, stack(q_o), stack(k_o), stack(q_s), stack(k_bar), jnp.concatenate(decay, axis=2)

    def tile_matmuls(ti, operands):
        rows, v, q_d2, k_d2, q_o2, k_o2, q_s2, k_bar2, decay2 = operands
        a_same = (jnp.where(col <= row, _dot_nt(q_d2[0], k_d2[0]), 0.0)
                  + jnp.where(col >= row, _dot_nt(q_d2[1], k_d2[1]), 0.0))
        a = jnp.where(same_half, a_same, jnp.where(cross_half, _dot_nt(q_o2, k_o2), 0.0))
        acc[rows, :] = _dot(a.astype(BF16), v)
        store_state_inputs(ti, rows, v, q_s2, k_bar2, decay2)

    def store_state_inputs(ti, rows, v, q_s2, k_bar2, decay2):
        qs[rows, :] = q_s2
        for cc in range(cpt):
            ci = ti * cpt + cc
            xs[ci] = _dot_tn(v[cc * c:(cc + 1) * c, :], k_bar2[cc * c:(cc + 1) * c, :])
            ds[ci] = jnp.broadcast_to(decay2[cc], ds.shape[1:])

    def tile_exact(ti, _):
        rows, lfs, bcs = cumulative(ti)
        q4, v = as4(q_ref[rows, :]), v_ref[rows, :]
        q_s, k_bar, decay = [], [], []
        for lf4, bc4, reverse in zip(lfs, bcs, (False, True)):
            b_last = bc4[:, 0:1, :] if reverse else bc4[:, c - 1:c, :]
            q_s.append(flat(q4 * jnp.exp2(bc4)))
            k_bar.append(flat((1.0 - jnp.exp2(lf4)) * jnp.exp2(b_last - bc4)))
            decay.append(jnp.exp2(b_last))
        stack = lambda parts: jnp.concatenate(parts, axis=1).astype(BF16)
        store_state_inputs(ti, rows, v, stack(q_s), stack(k_bar), jnp.concatenate(decay, axis=2))
        for cc in range(cpt):
            a = jnp.zeros((c, c), F32)
            q_c = q4[cc]
            for lf4, bc4, reverse in zip(lfs, bcs, (False, True)):
                bc_c = bc4[cc]
                cum[...] = bc_c
                kk[...] = 1.0 - jnp.exp2(lf4[cc])

                def col_step(s, a_):
                    d_s = jnp.exp2(jnp.minimum(bc_c - cum[pl.ds(s, 1), :], 0.0))
                    w = jnp.sum(q_c * kk[pl.ds(s, 1), :] * d_s, axis=-1, keepdims=True)
                    return jnp.where(col_c == s, w, a_)
                a_dir = lax.fori_loop(0, c, col_step, jnp.zeros((c, c), F32))
                a = a + jnp.where((col_c >= row_c) if reverse else (col_c <= row_c), a_dir, 0.0)
            crow = pl.ds(pl.multiple_of(ti * TM + cc * c, c), c)
            acc[crow, :] = _dot(a.astype(BF16), v[cc * c:(cc + 1) * c, :])
        return 0

    def tile_group(i, _):
        tiles = [i * unroll + u for u in range(unroll)]
        cums = {0: cumulative(tiles[0])}
        if unroll > 1:
            cums[1] = cumulative(tiles[1])
        ops = {0: tile_operands(cums[0])}
        for u in range(unroll):
            if u + 2 < unroll:
                cums[u + 2] = cumulative(tiles[u + 2])
            if u + 1 < unroll:
                ops[u + 1] = tile_operands(cums[u + 1])
            tile_matmuls(tiles[u], ops[u])
        return 0

    lax.fori_loop(0, n_tiles // unroll, tile_group, 0)

    @pl.when(unsafe)
    def _():
        lax.fori_loop(0, n_tiles, tile_exact, 0)

    def state_step(i, carry):
        s_f, s_b = carry
        cf = i
        cb = jnp.where(i < ctx_chunks, ctx_chunks - 1 - i, n_chunks - 1 + ctx_chunks - i)
        st[cf, :, 0:dk] = s_f.astype(BF16)
        st[cb, :, dk:2 * dk] = s_b.astype(BF16)
        s_f = s_f * ds[cf, 0:1, 0:dk] + xs[cf, :, 0:dk]
        s_b = s_b * ds[cb, 0:1, dk:2 * dk] + xs[cb, :, dk:2 * dk]
        return s_f, s_b

    zero = jnp.zeros((dk, dk), F32)
    lax.fori_loop(0, n_chunks, state_step, (zero, zero))

    def finish_step(ti):
        rows = pl.ds(pl.multiple_of(ti * TM, TM), TM)
        inter = [_dot_nt(qs[pl.ds(pl.multiple_of(ti * TM + cc * c, c), c), :], st[ti * cpt + cc]) for cc in range(cpt)]
        o = acc[rows, :] + jnp.concatenate(inter, axis=0)
        ms = jnp.mean(o * o, axis=-1, keepdims=True)
        o_ref[rows, :] = (o * lax.rsqrt(ms + EPS) * ng_ref[...] * _silu(g_ref[rows, :])).astype(BF16)

    def finish_group(i, _):
        for u in range(unroll):
            finish_step(i * unroll + u)
        return 0

    lax.fori_loop(0, n_tiles // unroll, finish_group, 0)


def _hgrn_scan(tok, q, lff, lfb, v, g, norm_gain, cast_stacks, cast_layer):
    n_rows, d = q.shape
    heads = d // LANES
    t = tok.t
    n_chunks = t // HGRN_CHUNK
    grid = (tok.batch, heads)
    side = _SideCasts(cast_stacks, cast_layer, grid)
    spec = pl.BlockSpec((None, t, LANES), lambda b, h: (b, 0, h))
    as3 = lambda a: a.reshape(tok.batch, t, d)
    in_specs = [spec] * 5 + [pl.BlockSpec((1, LANES), lambda b, h: (0, 0))]
    out, *casts = pl.pallas_call(
        side.wrap(functools.partial(_hgrn_scan_kernel, n_ctx=tok.n_ctx), len(in_specs), 1),
        grid=grid,
        in_specs=in_specs + side.in_specs,
        out_specs=[spec] + side.out_specs,
        out_shape=[jax.ShapeDtypeStruct((tok.batch, t, d), BF16)] + side.out_shapes,
        scratch_shapes=[pltpu.VMEM((t, LANES), F32), pltpu.VMEM((t, 2 * LANES), BF16),
                        pltpu.VMEM((n_chunks, LANES, 2 * LANES), F32), pltpu.VMEM((n_chunks, LANES, 2 * LANES), BF16),
                        pltpu.VMEM((n_chunks, SUBLANES, 2 * LANES), F32),
                        pltpu.VMEM((HGRN_CHUNK, LANES), F32), pltpu.VMEM((HGRN_CHUNK, LANES), F32)],
        compiler_params=_cparams("parallel", "parallel"),
        name="hgrn_scan",
    )(as3(q), as3(lff), as3(lfb), as3(v), as3(g), norm_gain.reshape(1, LANES), *side.stacks)
    return out.reshape(n_rows, d), casts


def _ffn_kernel(x_ref, o_ref, wo_ref, bo_ref, ng_ref, g1c, g1l, shc, shl, scc, scl, g2c, g2l, wi_ref, wf_ref, y_ref,
                *, tiles_per_batch, n_ctx):
    chunks, pick = _ctx_chunks(x_ref.shape[0], tiles_per_batch, n_ctx)
    x1 = [x_ref[rs, :] + pick(ic, g1c, g1l) * (_dot(o_ref[rs, :], wo_ref[...]) + bo_ref[...]) for rs, ic in chunks]
    h = jnp.concatenate([_norm_mod(x1_c, ng_ref[...], pick(ic, shc, shl), pick(ic, scc, scl)).astype(BF16)
                         for x1_c, (_, ic) in zip(x1, chunks)], axis=0)
    d_ff = wf_ref.shape[0]
    act = []
    for c0 in range(0, d_ff, FF_CHUNK):
        gate = _dot(h, wi_ref[:, c0:c0 + FF_CHUNK])
        up = _dot(h, wi_ref[:, d_ff + c0:d_ff + c0 + FF_CHUNK])
        act.append((_silu(gate) * up).astype(BF16))
    ff = _dot(jnp.concatenate(act, axis=1), wf_ref[...])
    for x1_c, (rs, ic) in zip(x1, chunks):
        y_ref[rs, :] = x1_c + pick(ic, g2c, g2l) * ff[rs, :]


def _out_proj_ffn(tok, x_all, o_all, mod, layer, w_o, b_o, norm_gain, w_in, w_out, lat_only):
    n_rows, d = x_all.shape
    d_ff = w_out.shape[0]
    assert d_ff % FF_CHUNK == 0
    rows_b = tok.seq if lat_only else tok.t
    tm = _row_tile(math.gcd(rows_b, tok.n_ctx) if lat_only else rows_b, FFN_MAX_ROWS)
    per_b = rows_b // tm
    first = tok.n_ctx // tm if lat_only else 0
    stream_row = pl.BlockSpec((tm, d), lambda i: ((i // per_b) * (tok.t // tm) + first + i % per_b, 0))
    out_row = pl.BlockSpec((tm, d), lambda i: (i, 0))
    o_row = stream_row if o_all.shape[0] == n_rows else out_row
    mods = [s for chunk in (2, 3, 4, 5) for s in tok.mod_specs(d, layer, chunk, per_b)]
    return pl.pallas_call(
        functools.partial(_ffn_kernel, tiles_per_batch=per_b, n_ctx=0 if lat_only else tok.n_ctx),
        grid=(tok.batch * per_b,),
        in_specs=[stream_row, o_row, _resident(w_o.shape), _vec_spec(d), _vec_spec(d)] + mods
                 + [_resident(w_in.shape), _resident(w_out.shape)],
        out_specs=out_row,
        out_shape=jax.ShapeDtypeStruct((tok.batch * rows_b, d), F32),
        compiler_params=_cparams("parallel"),
        name="out_proj_ffn",
    )(x_all, o_all, w_o.astype(BF16), b_o.reshape(1, d), norm_gain.reshape(1, d), *([mod] * 8), w_in, w_out)


def kernel(x, c, ctx, c_ctx, w_ada, b_ada, norm_gain, ffn_w_in, ffn_w_out, fnet_w_out, fnet_b_out,
           diff_w_in, diff_q_gain, diff_k_gain, diff_lambda, diff_subln_gain, diff_w_out,
           hgrn_w_in, hgrn_lower_bound, hgrn_norm_gain, hgrn_w_out,
           gqa_w_in, gqa_q_gain, gqa_k_gain, gqa_w_out):
    batch, seq, d = x.shape
    n_ctx = ctx.shape[1]
    depth = w_ada.shape[0]
    assert batch + 1 <= COND_ROWS and d % LANES == 0
    tok = _Tokens(batch, n_ctx, seq)

    cond = jnp.zeros((COND_ROWS, d), F32).at[0].set(c_ctx).at[1:1 + batch].set(c)
    mod = _ada_modulation(cond, w_ada, b_ada)
    x_all = None
    zero_bias = jnp.zeros((d,), F32)
    d_ff = ffn_w_out.shape[1]
    ffn_stacks = [ffn_w_in, ffn_w_out.reshape(depth, d, d_ff)]

    for i in range(depth):
        m, j = i % N_MIXERS, i // N_MIXERS
        need_ctx = i < depth - 1
        if m == 0:
            if x_all is not None:
                stream = x_all.reshape(batch, tok.t, d)
                ctx, x = stream[:, :n_ctx], stream[:, n_ctx:]
            o, x_all = _fnet_mix(tok, ctx, x, mod, i, norm_gain[i, 0], d // FNET_GROUPS)
            ffn_w = [w[i].astype(BF16) for w in ffn_stacks]
            w_o, b_o = fnet_w_out[j], fnet_b_out[j]
        elif m == 1:
            lam_init = 0.8 - 0.6 * math.exp(-0.3 * i)
            qg = diff_q_gain[j].reshape(1, LANES) * (DIFF_HEAD_DIM ** -0.5 * LOG2_E)
            kg = diff_k_gain[j].reshape(1, LANES)
            q, k, v = _qkv_project(tok, x_all, mod, i, norm_gain[i, 0], diff_w_in[j], qg, kg,
                                   DIFF_HEAD_DIM, d, d)
            o, ffn_w = _diff_attention(tok, q, k, v, qg, kg, diff_lambda[j], diff_subln_gain[j], lam_init,
                                       ffn_stacks, i)
            w_o, b_o = diff_w_out[j], zero_bias
        elif m == 2:
            q, lff, lfb, v, g = _hgrn_project(tok, x_all, mod, i, norm_gain[i, 0], hgrn_w_in[j], hgrn_lower_bound)
            o, ffn_w = _hgrn_scan(tok, q, lff, lfb, v, g, hgrn_norm_gain[j], ffn_stacks, i)
            w_o, b_o = hgrn_w_out[j], zero_bias
        else:
            kv = (gqa_w_in.shape[-1] - d) // 2
            qg = gqa_q_gain[j].reshape(1, LANES) * (GQA_HEAD_DIM ** -0.5 * LOG2_E)
            kg = gqa_k_gain[j].reshape(1, LANES)
            q, k, v = _qkv_project(tok, x_all, mod, i, norm_gain[i, 0], gqa_w_in[j], qg, kg, GQA_HEAD_DIM, d, kv)
            o, ffn_w = _gqa_attention(tok, q, k, v, qg, kg, need_ctx, ffn_stacks, i)
            w_o, b_o = gqa_w_out[j], zero_bias
        x_all = _out_proj_ffn(tok, x_all, o, mod, i, w_o, b_o, norm_gain[i, 1], ffn_w[0], ffn_w[1].reshape(d_ff, d),
                              lat_only=not need_ctx)
    return x_all.reshape(batch, seq, d)
```

```python
import functools
import math

import numpy as np
import jax
import jax.numpy as jnp
from jax import lax
from jax.experimental import pallas as pl
from jax.experimental.pallas import tpu as pltpu

F32 = jnp.float32
BF16 = jnp.bfloat16

EPS = 1e-6
GRID_W = 64
ROPE_THETA = 10000.0
N_MIXERS = 4

LANES = 128
SUBLANES = 8
BF16_SUBLANES = 16
TM = 256
COND_ROWS = 16
VMEM_LIMIT = 56 * 1024 * 1024

FNET_GROUPS = 8
DIFF_HEAD_DIM = 64
GQA_HEAD_DIM = 128
GQA_GROUP = 4
HGRN_CHUNK = 64
HGRN_HALF = 32
HGRN_TILE_UNROLL = 9
HGRN_SAFE_LOG2_DECAY = -115.0
FF_CHUNK = 256
FFN_MAX_ROWS = 768
PROJ_MAX_ROWS = 768
HGRN_PROJ_MAX_ROWS = 768
MXU_DIM = 256
LOG2_E = math.log2(math.e)
EXP2_SAFE_LOGIT = 100.0
BF16_SLACK = 1.02
DIFF_ATTN_UNROLL = 8
GQA_ATTN_UNROLL = 8


def _cparams(*sem):
    return pltpu.CompilerParams(dimension_semantics=sem, vmem_limit_bytes=VMEM_LIMIT)


def _resident(shape):
    nd = len(shape)
    return pl.BlockSpec(shape, lambda *_: (0,) * nd, pipeline_mode=pl.Buffered(1))


def _silu(x):
    return x * jax.nn.sigmoid(x)


def _norm_mod(x, gain, shift, scale):
    ms = jnp.mean(x * x, axis=-1, keepdims=True)
    y = x * lax.rsqrt(ms + EPS) * gain
    return y * (1.0 + scale) + shift


def _dot(a, b):
    return jnp.dot(a, b, preferred_element_type=F32)


def _dot_nt(a, b):
    return lax.dot_general(a, b, (((1,), (1,)), ((), ())), preferred_element_type=F32)


def _dot_tn(a, b):
    return lax.dot_general(a, b, (((0,), (0,)), ((), ())), preferred_element_type=F32)


def _ada_kernel(cond_ref, w_ref, b_ref, o_ref):
    a = _silu(cond_ref[...]).astype(BF16)
    o_ref[...] = _dot(a, w_ref[...].astype(BF16)) + b_ref[...]


def _ada_modulation(cond, w_ada, b_ada):
    depth, d, n6 = w_ada.shape
    bn = n6 // 4
    out = pl.pallas_call(
        _ada_kernel,
        grid=(depth, n6 // bn),
        in_specs=[
            pl.BlockSpec((COND_ROWS, d), lambda i, j: (0, 0)),
            pl.BlockSpec((None, d, bn), lambda i, j: (i, 0, j)),
            pl.BlockSpec((None, 1, bn), lambda i, j: (i, 0, j)),
        ],
        out_specs=pl.BlockSpec((None, COND_ROWS, bn), lambda i, j: (i, 0, j)),
        out_shape=jax.ShapeDtypeStruct((depth, COND_ROWS, n6), F32),
        compiler_params=_cparams("parallel", "parallel"),
        name="ada_modulation",
    )(cond, w_ada, b_ada.reshape(depth, 1, n6))
    return out.reshape(depth * COND_ROWS * 6, 1, d)


class _Tokens:
    def __init__(self, batch, n_ctx, seq):
        assert n_ctx % TM == 0 and seq % TM == 0
        self.batch, self.n_ctx, self.seq = batch, n_ctx, seq
        self.t = n_ctx + seq
        self.ctx_tiles = n_ctx // TM
        self.tiles = self.t // TM

    def mod_specs(self, d, layer, chunk, tiles_per_batch):
        ctx = pl.BlockSpec((None, 1, d), lambda i: (layer * COND_ROWS * 6 + chunk, 0, 0))
        lat = pl.BlockSpec((None, 1, d), lambda i: ((layer * COND_ROWS + 1 + i // tiles_per_batch) * 6 + chunk, 0, 0))
        return [ctx, lat]


def _ctx_chunks(rows, tiles_per_batch, n_ctx):
    row0 = (pl.program_id(0) % tiles_per_batch) * rows
    chunks = [(slice(r, r + TM), row0 + r < n_ctx) for r in range(0, rows, TM)]
    if n_ctx == 0:
        return chunks, lambda is_ctx, c_ref, l_ref: l_ref[...]
    return chunks, lambda is_ctx, c_ref, l_ref: jnp.where(is_ctx, c_ref[...], l_ref[...])


def _row_tile(n, cap, step=TM):
    return max(r for r in range(step, cap + 1, step) if n % r == 0)


def _vec_spec(n):
    return pl.BlockSpec((1, n), lambda *_: (0, 0))


class _SideCasts:
    def __init__(self, stacks, layer, grid):
        n0, n1 = grid
        self.stacks, self.in_specs, self.out_specs, self.out_shapes = list(stacks), [], [], []
        for w in self.stacks:
            _, r, c = w.shape
            share = next(s for s in range(1, n1 + 1)
                         if n1 % s == 0 and r % (n0 * n1 // s) == 0 and (r * s // (n0 * n1)) % BF16_SUBLANES == 0)
            rows = r * share // (n0 * n1)
            assert c % LANES == 0
            self.in_specs.append(self._spec((None, rows, c), lambda blk, layer=layer: (layer, blk, 0), n1, share))
            self.out_specs.append(self._spec((rows, c), lambda blk: (blk, 0), n1, share))
            self.out_shapes.append(jax.ShapeDtypeStruct((r, c), BF16))

    @staticmethod
    def _spec(shape, index, n1, share):
        return pl.BlockSpec(shape, lambda a, b: index((a * n1 + b) // share))

    def wrap(self, kernel_fn, n_in, n_out):
        n = len(self.stacks)

        def wrapped(*refs):
            for src, dst in zip(refs[n_in:n_in + n], refs[n_in + n + n_out:n_in + 2 * n + n_out]):
                dst[...] = src[...].astype(BF16)
            kernel_fn(*refs[:n_in], *refs[n_in + n:n_in + n + n_out], *refs[n_in + 2 * n + n_out:])
        return wrapped


def _dft_tables(n_ctx, seq, group_dim):
    def cs(n):
        k = np.arange(n, dtype=np.int64)
        ang = 2.0 * np.pi * ((k[:, None] * k[None, :]) % n).astype(np.float64) / n
        return np.cos(ang), np.sin(ang)
    cc, sc = cs(group_dim)
    chan = np.concatenate([cc, sc], axis=1)
    cx, sx = cs(n_ctx)
    pos_ctx = np.concatenate([cx, -sx], axis=1)
    cl, sl = cs(seq)
    pt = _half_spectrum_tile(seq)
    rows = (np.arange(seq // 2 // pt)[:, None] * pt + np.arange(pt + BF16_SUBLANES)[None, :]) % seq
    rev = np.zeros((pt, pt + BF16_SUBLANES), np.float32)
    rev[np.arange(pt), pt - np.arange(pt)] = 1.0
    as_bf16 = lambda a: jnp.asarray(a.astype(np.float32)).astype(BF16)
    return as_bf16(chan), as_bf16(pos_ctx), as_bf16(cl[rows]), as_bf16(sl[rows]), as_bf16(rev)


def _half_spectrum_tile(seq):
    return min(TM, seq // 2)


def _fnet_kernel(ctx_ref, lat_ref, ng_ref, shc_ref, scc_ref, shl_ref, scl_ref, chan_ref, pctx_ref, cos_ref, sin_ref,
                 rev_ref, y_ref, xs_ref, ab_ctx, ab_lat, *, group_dim):
    j = pl.program_id(1)
    n_ctx, d = ctx_ref.shape
    seq = lat_ref.shape[0]
    ctx_tiles = n_ctx // TM
    pt = _half_spectrum_tile(seq)
    p_tiles = seq // 2 // pt

    @pl.when(j == 0)
    def _():
        for src, sh, sc, dst, n in ((ctx_ref, shc_ref, scc_ref, ab_ctx, n_ctx), (lat_ref, shl_ref, scl_ref, ab_lat, seq)):
            for row0 in range(0, n, TM):
                h = _norm_mod(src[row0:row0 + TM, :], ng_ref[...], sh[...], sc[...]).astype(BF16)
                for g in range(d // group_dim):
                    cols = slice(g * group_dim, (g + 1) * group_dim)
                    ab = _dot(h[:, cols], chan_ref[...])
                    dst[row0:row0 + TM, cols] = ab[:, :group_dim].astype(BF16)
                    dst[n + row0:n + row0 + TM, cols] = ab[:, group_dim:].astype(BF16)

    @pl.when(j < ctx_tiles)
    def _():
        rows = pl.ds(pl.multiple_of(j * TM, TM), TM)
        y = _dot(pctx_ref[rows, :], ab_ctx[...]) * (1.0 / math.sqrt(n_ctx * group_dim))
        y_ref[rows, :] = y.astype(BF16)
        xs_ref[...] = ctx_ref[rows, :]

    @pl.when(j >= ctx_tiles)
    def _():
        xs_ref[...] = lat_ref[pl.ds(pl.multiple_of((j - ctx_tiles) * TM, TM), TM), :]

    first_p_step = pl.num_programs(1) - p_tiles

    @pl.when(j >= first_p_step)
    def _():
        tile = j - first_p_step
        scale = 1.0 / math.sqrt(seq * group_dim)
        p_part = _dot(cos_ref[...], ab_lat[0:seq, :])
        q_part = _dot(sin_ref[...], ab_lat[seq:2 * seq, :])
        y_ref[pl.ds(pl.multiple_of(n_ctx + tile * pt, pt), pt), :] = ((p_part - q_part)[0:pt] * scale).astype(BF16)
        mirrored = _dot(rev_ref[...], ((p_part + q_part) * scale).astype(BF16))
        y_ref[pl.ds(pl.multiple_of(n_ctx + seq - (tile + 1) * pt, pt), pt), :] = mirrored.astype(BF16)


def _fnet_mix(tok, ctx, x, mod, layer, norm_gain, group_dim):
    b, t, d = tok.batch, tok.t, x.shape[-1]
    chan, pos_ctx, cos_lat, sin_lat, rev = _dft_tables(tok.n_ctx, tok.seq, group_dim)
    ct, p_tiles = tok.ctx_tiles, cos_lat.shape[0]
    assert p_tiles < tok.tiles

    def mod_spec(chunk, is_ctx):
        return pl.BlockSpec((None, 1, d), lambda i, j: ((layer * COND_ROWS + (0 if is_ctx else 1 + i)) * 6 + chunk, 0, 0))

    table_spec = pl.BlockSpec((None,) + cos_lat.shape[1:],
                              lambda i, j: (jnp.maximum(j - (tok.tiles - p_tiles), 0), 0, 0))
    y, xs = pl.pallas_call(
        functools.partial(_fnet_kernel, group_dim=group_dim),
        grid=(b, tok.tiles),
        in_specs=[
            pl.BlockSpec((None, tok.n_ctx, d), lambda i, j: (i, 0, 0)),
            pl.BlockSpec((None, tok.seq, d), lambda i, j: (i, 0, 0)),
            pl.BlockSpec((1, d), lambda i, j: (0, 0)),
            mod_spec(0, True), mod_spec(1, True), mod_spec(0, False), mod_spec(1, False),
            _resident(chan.shape), _resident(pos_ctx.shape), table_spec, table_spec, _resident(rev.shape),
        ],
        out_specs=[pl.BlockSpec((None, t, d), lambda i, j: (i, 0, 0)),
                   pl.BlockSpec((TM, d), lambda i, j: (i * tok.tiles + j, 0))],
        out_shape=[jax.ShapeDtypeStruct((b, t, d), BF16), jax.ShapeDtypeStruct((b * t, d), F32)],
        scratch_shapes=[pltpu.VMEM((2 * tok.n_ctx, d), BF16), pltpu.VMEM((2 * tok.seq, d), BF16)],
        compiler_params=_cparams("parallel", "arbitrary"),
        name="fnet_mix",
    )(ctx, x, norm_gain.reshape(1, d), mod, mod, mod, mod, chan, pos_ctx, cos_lat, sin_lat, rev)
    return y.reshape(b * t, d), xs


def _rope_tables(tok, head_dim):
    rows = tok.seq // GRID_W
    row = jnp.repeat(jnp.arange(rows, dtype=F32), GRID_W)
    col = jnp.tile(jnp.arange(GRID_W, dtype=F32), rows)
    n_freq = head_dim // 4
    inv_freq = ROPE_THETA ** (-jnp.arange(n_freq, dtype=F32) / n_freq)
    ang = jnp.concatenate([row[:, None] * inv_freq, col[:, None] * inv_freq], axis=-1)
    cos = jnp.concatenate([jnp.ones((tok.n_ctx, head_dim // 2), F32), jnp.cos(ang)], axis=0)
    sin = jnp.concatenate([jnp.zeros((tok.n_ctx, head_dim // 2), F32), jnp.sin(ang)], axis=0)
    reps = LANES // head_dim
    cos = jnp.tile(jnp.concatenate([cos, cos], axis=-1), (1, reps))
    sin = jnp.tile(jnp.concatenate([-sin, sin], axis=-1), (1, reps))
    return cos, sin


def _group_mean_matrix(group, width):
    g = np.arange(width) // group
    return jnp.asarray((g[:, None] == g[None, :]).astype(np.float32) / group).astype(BF16)


def _rope(yn, cos, sin, head_dim):
    half = head_dim // 2
    if head_dim == LANES:
        partner = pltpu.roll(yn, half, 1)
    else:
        lane = lax.broadcasted_iota(jnp.int32, yn.shape, 1)
        partner = jnp.where(lane % head_dim < half, pltpu.roll(yn, LANES - half, 1), pltpu.roll(yn, half, 1))
    return yn * cos + partner * sin


def _qkv_kernel(x_ref, ng_ref, shc, shl, scc, scl, w_ref, qg_ref, kg_ref, cos_ref, sin_ref, gm_ref,
                q_ref, k_ref, vt_ref, *, head_dim, tiles_per_batch, n_ctx):
    chunks, pick = _ctx_chunks(x_ref.shape[0], tiles_per_batch, n_ctx)
    h = jnp.concatenate([_norm_mod(x_ref[rs, :], ng_ref[...], pick(ic, shc, shl), pick(ic, scc, scl)).astype(BF16)
                         for rs, ic in chunks], axis=0)
    cos, sin, gm = cos_ref[...], sin_ref[...], gm_ref[...]
    nq, nk = q_ref.shape[-1], k_ref.shape[-1]
    for o_ref, g_ref, col0, n in ((q_ref, qg_ref, 0, nq), (k_ref, kg_ref, nq, nk)):
        y = _dot(h, w_ref[:, col0:col0 + n])
        for c0 in range(0, n, MXU_DIM):
            yb = y[:, c0:c0 + MXU_DIM]
            yn = yb * lax.rsqrt(_dot((yb * yb).astype(BF16), gm) + EPS) * g_ref[...]
            for l0 in range(0, MXU_DIM, LANES):
                o_ref[:, c0 + l0:c0 + l0 + LANES] = _rope(yn[:, l0:l0 + LANES], cos, sin, head_dim).astype(BF16)
    vt_ref[...] = _dot(h, w_ref[:, nq + nk:]).astype(BF16).T


def _qkv_project(tok, x_all, mod, layer, norm_gain, w_in, q_gain, k_gain, head_dim, nq, nk):
    n_rows, d = x_all.shape
    nv = w_in.shape[1] - nq - nk
    assert nq % MXU_DIM == 0 and nk % MXU_DIM == 0
    cos, sin = _rope_tables(tok, head_dim)
    tm = _row_tile(tok.t, PROJ_MAX_ROWS)
    per_b = tok.t // tm
    row_spec = lambda n: pl.BlockSpec((tm, n), lambda i: (i, 0))
    tab_spec = pl.BlockSpec((tm, LANES), lambda i: (i % per_b, 0))
    gain2 = lambda g: jnp.tile(g, (1, MXU_DIM // LANES))
    return pl.pallas_call(
        functools.partial(_qkv_kernel, head_dim=head_dim, tiles_per_batch=per_b, n_ctx=tok.n_ctx),
        grid=(n_rows // tm,),
        in_specs=[row_spec(d), _vec_spec(d)] + tok.mod_specs(d, layer, 0, per_b) + tok.mod_specs(d, layer, 1, per_b)
                 + [_resident(w_in.shape), _vec_spec(MXU_DIM), _vec_spec(MXU_DIM), tab_spec, tab_spec,
                    _resident((MXU_DIM, MXU_DIM))],
        out_specs=[row_spec(nq), row_spec(nk), pl.BlockSpec((None, nv, tm), lambda i: (i // per_b, 0, i % per_b))],
        out_shape=[jax.ShapeDtypeStruct((n_rows, nq), BF16), jax.ShapeDtypeStruct((n_rows, nk), BF16),
                   jax.ShapeDtypeStruct((tok.batch, nv, tok.t), BF16)],
        compiler_params=_cparams("parallel"),
        name=f"qkv_project_hd{head_dim}",
    )(x_all, norm_gain.reshape(1, d), mod, mod, mod, mod, w_in.astype(BF16), gain2(q_gain), gain2(k_gain),
      cos, sin, _group_mean_matrix(head_dim, MXU_DIM))


def _softmax_parts(q, k):
    s = _dot_nt(q, k)
    p = jnp.exp2(s - jnp.max(s, axis=-1, keepdims=True))
    return p, jnp.sum(p, axis=-1, keepdims=True)


def _softmax_parts_bounded(k, q, vt):
    p = jnp.exp2(_dot_nt(k, q))
    l = jnp.sum(p, axis=0, keepdims=True)
    return _dot(vt, p.astype(BF16)) * (1.0 / l)


def _logit_bound(qg_ref, kg_ref, head_dim):
    return BF16_SLACK * head_dim * jnp.max(jnp.abs(qg_ref[...])) * jnp.max(jnp.abs(kg_ref[...]))


def _diff_attn_kernel(lp_ref, sg_ref, qg_ref, kg_ref, q_ref, k_ref, vt, o_ref, *, lam_init, n_ctx):
    lp = lp_ref[...]
    lam = (jnp.exp(jnp.sum(lp[0:1] * lp[1:2], axis=-1, keepdims=True))
           - jnp.exp(jnp.sum(lp[2:3] * lp[3:4], axis=-1, keepdims=True)) + lam_init)
    t = k_ref.shape[0]
    lane = lax.broadcasted_iota(jnp.int32, (TM, LANES), 1)
    bounded = _logit_bound(qg_ref, kg_ref, DIFF_HEAD_DIM) <= EXP2_SAFE_LOGIT

    def split_q(rows):
        q = q_ref[rows, :]
        zero = jnp.zeros_like(q)
        return jnp.where(lane < DIFF_HEAD_DIM, q, zero), jnp.where(lane >= DIFF_HEAD_DIM, q, zero)

    def tile_bounded(rows, nk):
        o12 = _softmax_parts_bounded(k_ref[0:nk, :], jnp.concatenate(split_q(rows), axis=0), vt[:, 0:nk])
        ot = o12[:, :TM] - lam * o12[:, TM:]
        ms = jnp.mean(ot * ot, axis=0, keepdims=True)
        o_ref[rows, :] = ((ot * lax.rsqrt(ms + EPS)).T * sg_ref[...]).astype(BF16)

    def tile_exact(rows, nk):
        k = k_ref[0:nk, :]
        q1, q2 = split_q(rows)
        p1, l1 = _softmax_parts(q1, k)
        p2, l2 = _softmax_parts(q2, k)
        w = p1 * (1.0 / l1) - p2 * (lam / l2)
        o = _dot(w.astype(BF16), vt[:, 0:nk].T)
        ms = jnp.mean(o * o, axis=-1, keepdims=True)
        o_ref[rows, :] = (o * lax.rsqrt(ms + EPS) * sg_ref[...]).astype(BF16)

    @pl.when(bounded)
    def _():
        _for_query_tiles(tile_bounded, n_ctx, t, 0, DIFF_ATTN_UNROLL)

    @pl.when(jnp.logical_not(bounded))
    def _():
        _for_query_tiles(tile_exact, n_ctx, t, 0, 1)


def _for_query_tiles(tile, n_ctx, t, first_row, unroll):
    for r0 in range(first_row, n_ctx, TM):
        tile(pl.ds(r0, TM), n_ctx)
    lat0 = max(first_row, n_ctx)
    n_lat = (t - lat0) // TM
    unroll = unroll if n_lat % unroll == 0 else 1

    def group(i, _):
        for u in range(unroll):
            tile(pl.ds(pl.multiple_of(lat0 + (i * unroll + u) * TM, TM), TM), t)
        return 0

    lax.fori_loop(0, n_lat // unroll, group, 0)


def _diff_attention(tok, q, k, vt, q_gain, k_gain, lam_par, subln_gain, lam_init, cast_stacks, cast_layer):
    n_rows, d = q.shape
    heads = d // LANES
    t = tok.t
    grid = (tok.batch, heads)
    side = _SideCasts(cast_stacks, cast_layer, grid)
    as3 = lambda a: a.reshape(tok.batch, t, d)
    spec = pl.BlockSpec((None, t, LANES), lambda b, h: (b, 0, h))
    vt_spec = pl.BlockSpec((None, LANES, t), lambda b, h: (b, h, 0))
    vec = pl.BlockSpec((1, LANES), lambda b, h: (0, 0))
    in_specs = [pl.BlockSpec(lam_par.shape, lambda b, h: (0, 0)), vec, vec, vec, spec, spec, vt_spec]
    out, *casts = pl.pallas_call(
        side.wrap(functools.partial(_diff_attn_kernel, lam_init=lam_init, n_ctx=tok.n_ctx), len(in_specs), 1),
        grid=grid,
        in_specs=in_specs + side.in_specs,
        out_specs=[spec] + side.out_specs,
        out_shape=[jax.ShapeDtypeStruct((tok.batch, t, d), BF16)] + side.out_shapes,
        compiler_params=_cparams("parallel", "arbitrary"),
        name="diff_attention",
    )(lam_par, (subln_gain * (1.0 - lam_init)).reshape(1, LANES), q_gain, k_gain, as3(q), as3(k), vt, *side.stacks)
    return out.reshape(n_rows, d), casts


def _gqa_attn_kernel(qg_ref, kg_ref, q_ref, k_ref, vt, o_ref, *, n_ctx, first_row):
    t = k_ref.shape[0]
    bounded = _logit_bound(qg_ref, kg_ref, GQA_HEAD_DIM) <= EXP2_SAFE_LOGIT

    def out_rows(rows):
        if first_row == 0:
            return rows
        start = rows.start - first_row
        return pl.ds(start if isinstance(start, int) else pl.multiple_of(start, TM), TM)

    def tile_bounded(rows, nk):
        k, vt_k = k_ref[0:nk, :], vt[:, 0:nk]
        for g in range(0, GQA_GROUP, 2):
            q2h = jnp.concatenate([q_ref[rows, g * LANES:(g + 1) * LANES],
                                   q_ref[rows, (g + 1) * LANES:(g + 2) * LANES]], axis=0)
            o2h = _softmax_parts_bounded(k, q2h, vt_k)
            o_ref[out_rows(rows), g * LANES:(g + 1) * LANES] = o2h[:, :TM].T.astype(BF16)
            o_ref[out_rows(rows), (g + 1) * LANES:(g + 2) * LANES] = o2h[:, TM:].T.astype(BF16)

    def tile_exact(rows, nk):
        k, v = k_ref[0:nk, :], vt[:, 0:nk].T
        for g in range(GQA_GROUP):
            cols = slice(g * LANES, (g + 1) * LANES)
            p, l = _softmax_parts(q_ref[rows, cols], k)
            o_ref[out_rows(rows), cols] = (_dot(p.astype(BF16), v) * (1.0 / l)).astype(BF16)

    @pl.when(bounded)
    def _():
        _for_query_tiles(tile_bounded, n_ctx, t, first_row, GQA_ATTN_UNROLL)

    @pl.when(jnp.logical_not(bounded))
    def _():
        _for_query_tiles(tile_exact, n_ctx, t, first_row, 1)


def _gqa_attention(tok, q, k, vt, q_gain, k_gain, need_ctx, cast_stacks, cast_layer):
    d = q.shape[1]
    kv_heads = k.shape[1] // LANES
    t = tok.t
    grid = (tok.batch, kv_heads)
    side = _SideCasts(cast_stacks, cast_layer, grid)
    first_row = 0 if need_ctx else tok.n_ctx
    as3 = lambda a: a.reshape(tok.batch, t, a.shape[1])
    k_spec = pl.BlockSpec((None, t, LANES), lambda b, h: (b, 0, h))
    vt_spec = pl.BlockSpec((None, LANES, t), lambda b, h: (b, h, 0))
    q_spec = pl.BlockSpec((None, t, GQA_GROUP * LANES), lambda b, h: (b, 0, h))
    o_spec = pl.BlockSpec((None, t - first_row, GQA_GROUP * LANES), lambda b, h: (b, 0, h))
    vec = pl.BlockSpec((1, LANES), lambda b, h: (0, 0))
    in_specs = [vec, vec, q_spec, k_spec, vt_spec]
    out, *casts = pl.pallas_call(
        side.wrap(functools.partial(_gqa_attn_kernel, n_ctx=tok.n_ctx, first_row=first_row), len(in_specs), 1),
        grid=grid,
        in_specs=in_specs + side.in_specs,
        out_specs=[o_spec] + side.out_specs,
        out_shape=[jax.ShapeDtypeStruct((tok.batch, t - first_row, d), BF16)] + side.out_shapes,
        compiler_params=_cparams("parallel", "arbitrary"),
        name="gqa_attention",
    )(q_gain, k_gain, as3(q), as3(k), vt, *side.stacks)
    return out.reshape(tok.batch * (t - first_row), d), casts


def _hgrn_proj_kernel(x_ref, ng_ref, shc, shl, scc, scl, w_ref, lb_ref, q_ref, lff_ref, lfb_ref, v_ref, g_ref,
                      *, layer_idx, depth, tiles_per_batch, n_ctx):
    d = x_ref.shape[-1]
    chunks, pick = _ctx_chunks(x_ref.shape[0], tiles_per_batch, n_ctx)
    lbs = []
    for direction in range(2):
        rows = [lb_ref[direction * depth + i:direction * depth + i + 1, :] for i in range(depth)]
        m = functools.reduce(jnp.maximum, rows)
        e = [jnp.exp(r - m) for r in rows]
        lbs.append(sum(e[1:layer_idx + 1]) / sum(e) if layer_idx > 0 else jnp.zeros_like(m))

    def normed(rs, ic):
        return _norm_mod(x_ref[rs, :], ng_ref[...], pick(ic, shc, shl), pick(ic, scc, scl)).astype(BF16)

    def project(rs, h):
        q_ref[rs, :] = _silu(_dot(h, w_ref[:, 0:d]))
        for direction, o_ref in enumerate((lff_ref, lfb_ref)):
            z = _dot(h, w_ref[:, (1 + direction) * d:(2 + direction) * d])
            lb = lbs[direction]
            o_ref[rs, :] = jnp.log2(lb + (1.0 - lb) * jax.nn.sigmoid(z))
        v_ref[rs, :] = _dot(h, w_ref[:, 3 * d:4 * d]).astype(BF16)
        g_ref[rs, :] = _dot(h, w_ref[:, 4 * d:5 * d])

    h = normed(*chunks[0])
    for k, (rs, _) in enumerate(chunks):
        h_next = normed(*chunks[k + 1]) if k + 1 < len(chunks) else None
        project(rs, h)
        h = h_next


def _hgrn_project(tok, x_all, mod, layer, norm_gain, w_in, lower_bound):
    n_rows, d = x_all.shape
    depth = lower_bound.shape[1]
    tm = _row_tile(tok.t, HGRN_PROJ_MAX_ROWS)
    per_b = tok.t // tm
    row_spec = pl.BlockSpec((tm, d), lambda i: (i, 0))
    return pl.pallas_call(
        functools.partial(_hgrn_proj_kernel, layer_idx=layer, depth=depth, tiles_per_batch=per_b, n_ctx=tok.n_ctx),
        grid=(n_rows // tm,),
        in_specs=[row_spec, _vec_spec(d)] + tok.mod_specs(d, layer, 0, per_b) + tok.mod_specs(d, layer, 1, per_b)
                 + [_resident(w_in.shape), pl.BlockSpec((2 * depth, d), lambda i: (0, 0))],
        out_specs=[row_spec] * 5,
        out_shape=[jax.ShapeDtypeStruct((n_rows, d), dt) for dt in (F32, F32, F32, BF16, F32)],
        compiler_params=_cparams("parallel"),
        name="hgrn_project",
    )(x_all, norm_gain.reshape(1, d), mod, mod, mod, mod, w_in.astype(BF16), lower_bound.reshape(2 * depth, d))


def _split2(x):
    hi = x.astype(BF16)
    return hi, (x - hi.astype(F32)).astype(BF16)


def _hgrn_scan_kernel(q_ref, lff_ref, lfb_ref, v_ref, g_ref, ng_ref, o_ref,
                      acc, qs, xs, st, ds, cum, kk, *, n_ctx):
    t, dk = q_ref.shape
    c, hh = HGRN_CHUNK, HGRN_HALF
    n_chunks, ctx_chunks = t // c, n_ctx // c
    cpt = TM // c
    row = lax.broadcasted_iota(jnp.int32, (TM, TM), 0)
    col = lax.broadcasted_iota(jnp.int32, (TM, TM), 1)
    same_chunk = (row // c) == (col // c)
    same_half = (row // hh) == (col // hh)
    cross_half = same_chunk & jnp.logical_not(same_half)
    cum_mat = jnp.where(same_chunk & (col <= row), 1.0, 0.0).astype(BF16)
    in_chunk = lax.broadcasted_iota(jnp.int32, (cpt, c, dk), 1)
    row_c = lax.broadcasted_iota(jnp.int32, (c, c), 0)
    col_c = lax.broadcasted_iota(jnp.int32, (c, c), 1)

    def seg_min(ref):
        return jnp.min(ref[...].reshape(t // hh, hh, dk).sum(axis=1))
    unsafe = jnp.minimum(seg_min(lff_ref), seg_min(lfb_ref)) < HGRN_SAFE_LOG2_DECAY

    as4 = lambda a: a.reshape(cpt, c, dk)
    as8 = lambda a: a.reshape(2 * cpt, hh, dk)
    flat = lambda a: a.reshape(TM, dk)
    half_idx = lax.broadcasted_iota(jnp.int32, (2 * cpt, 1, 1), 0)
    per_half = lambda a: jnp.concatenate([a[i // 2:i // 2 + 1] for i in range(2 * cpt)], axis=0)
    n_tiles = t // TM
    unroll = next(u for u in (HGRN_TILE_UNROLL, 3, 2, 1) if n_tiles % u == 0)

    def cumulative(ti):
        rows = pl.ds(pl.multiple_of(ti * TM, TM), TM)
        lf_f, lf_b = lff_ref[rows, :], lfb_ref[rows, :]
        hi, lo = _split2(jnp.concatenate([lf_f, lf_b], axis=1))
        pre = _dot(cum_mat, hi) + _dot(cum_mat, lo)
        pre_b = as4(pre[:, dk:])
        bc_b = pre_b[:, c - 1:c, :] - pre_b + as4(lf_b)
        return rows, (as4(lf_f), as4(lf_b)), (as4(pre[:, :dk]), bc_b)

    def tile_operands(cumulated):
        rows, lfs, bcs = cumulated
        q8, v = as8(q_ref[rows, :]), v_ref[rows, :]
        q_d2, k_d2, q_s, k_bar, decay, q_o, k_o = [], [], [], [], [], [], []
        for lf4, bc4, reverse in zip(lfs, bcs, (False, True)):
            edge, last = (hh, 0) if reverse else (hh - 1, c - 1)
            second = (half_idx % 2 == 0) if reverse else (half_idx % 2 == 1)
            b_last, b_edge = bc4[:, last:last + 1, :], bc4[:, edge:edge + 1, :]
            b_last8, b_edge8 = per_half(b_last), per_half(b_edge)
            bc8, k8 = as8(bc4), 1.0 - jnp.exp2(as8(lf4))
            ref = bc8[:, hh // 2:hh // 2 + 1, :]
            rel = bc8 - ref
            q_d, k_d = q8 * jnp.exp2(rel), k8 * jnp.exp2(-rel)
            q_d2.append(flat(q_d).astype(BF16))
            k_d2.append(flat(k_d).astype(BF16))
            q_s.append(flat(q_d * jnp.exp2(ref)))
            k_bar.append(flat(k_d * jnp.exp2(b_last8 - ref)))
            decay.append(jnp.exp2(b_last))
            q_o.append(flat(q_d * jnp.where(second, jnp.exp2(ref - b_edge8), 0.0)))
            k_o.append(flat(k_d * jnp.where(second, 0.0, jnp.exp2(b_edge8 - ref))))
        stack = lambda parts: jnp.concatenate(parts, axis=1).astype(BF16)
        return rows, v, q_d2, k_d2, stack(q_o), stack(k_o), stack(q_s), stack(k_bar), jnp.concatenate(decay, axis=2)

    def tile_matmuls(ti, operands):
        rows, v, q_d2, k_d2, q_o2, k_o2, q_s2, k_bar2, decay2 = operands
        a_same = (jnp.where(col <= row, _dot_nt(q_d2[0], k_d2[0]), 0.0)
                  + jnp.where(col >= row, _dot_nt(q_d2[1], k_d2[1]), 0.0))
        a = jnp.where(same_half, a_same, jnp.where(cross_half, _dot_nt(q_o2, k_o2), 0.0))
        acc[rows, :] = _dot(a.astype(BF16), v)
        store_state_inputs(ti, rows, v, q_s2, k_bar2, decay2)

    def store_state_inputs(ti, rows, v, q_s2, k_bar2, decay2):
        qs[rows, :] = q_s2
        for cc in range(cpt):
            ci = ti * cpt + cc
            xs[ci] = _dot_tn(v[cc * c:(cc + 1) * c, :], k_bar2[cc * c:(cc + 1) * c, :])
            ds[ci] = jnp.broadcast_to(decay2[cc], ds.shape[1:])

    def tile_exact(ti, _):
        rows, lfs, bcs = cumulative(ti)
        q4, v = as4(q_ref[rows, :]), v_ref[rows, :]
        q_s, k_bar, decay = [], [], []
        for lf4, bc4, reverse in zip(lfs, bcs, (False, True)):
            b_last = bc4[:, 0:1, :] if reverse else bc4[:, c - 1:c, :]
            q_s.append(flat(q4 * jnp.exp2(bc4)))
            k_bar.append(flat((1.0 - jnp.exp2(lf4)) * jnp.exp2(b_last - bc4)))
            decay.append(jnp.exp2(b_last))
        stack = lambda parts: jnp.concatenate(parts, axis=1).astype(BF16)
        store_state_inputs(ti, rows, v, stack(q_s), stack(k_bar), jnp.concatenate(decay, axis=2))
        for cc in range(cpt):
            a = jnp.zeros((c, c), F32)
            q_c = q4[cc]
            for lf4, bc4, reverse in zip(lfs, bcs, (False, True)):
                bc_c = bc4[cc]
                cum[...] = bc_c
                kk[...] = 1.0 - jnp.exp2(lf4[cc])

                def col_step(s, a_):
                    d_s = jnp.exp2(jnp.minimum(bc_c - cum[pl.ds(s, 1), :], 0.0))
                    w = jnp.sum(q_c * kk[pl.ds(s, 1), :] * d_s, axis=-1, keepdims=True)
                    return jnp.where(col_c == s, w, a_)
                a_dir = lax.fori_loop(0, c, col_step, jnp.zeros((c, c), F32))
                a = a + jnp.where((col_c >= row_c) if reverse else (col_c <= row_c), a_dir, 0.0)
            crow = pl.ds(pl.multiple_of(ti * TM + cc * c, c), c)
            acc[crow, :] = _dot(a.astype(BF16), v[cc * c:(cc + 1) * c, :])
        return 0

    def tile_group(i, _):
        tiles = [i * unroll + u for u in range(unroll)]
        cums = {0: cumulative(tiles[0])}
        if unroll > 1:
            cums[1] = cumulative(tiles[1])
        ops = {0: tile_operands(cums[0])}
        for u in range(unroll):
            if u + 2 < unroll:
                cums[u + 2] = cumulative(tiles[u + 2])
            if u + 1 < unroll:
                ops[u + 1] = tile_operands(cums[u + 1])
            tile_matmuls(tiles[u], ops[u])
        return 0

    lax.fori_loop(0, n_tiles // unroll, tile_group, 0)

    @pl.when(unsafe)
    def _():
        lax.fori_loop(0, n_tiles, tile_exact, 0)

    def state_step(i, carry):
        s_f, s_b = carry
        cf = i
        cb = jnp.where(i < ctx_chunks, ctx_chunks - 1 - i, n_chunks - 1 + ctx_chunks - i)
        st[cf, :, 0:dk] = s_f.astype(BF16)
        st[cb, :, dk:2 * dk] = s_b.astype(BF16)
        s_f = s_f * ds[cf, 0:1, 0:dk] + xs[cf, :, 0:dk]
        s_b = s_b * ds[cb, 0:1, dk:2 * dk] + xs[cb, :, dk:2 * dk]
        return s_f, s_b

    zero = jnp.zeros((dk, dk), F32)
    lax.fori_loop(0, n_chunks, state_step, (zero, zero))

    def finish_step(ti):
        rows = pl.ds(pl.multiple_of(ti * TM, TM), TM)
        inter = [_dot_nt(qs[pl.ds(pl.multiple_of(ti * TM + cc * c, c), c), :], st[ti * cpt + cc]) for cc in range(cpt)]
        o = acc[rows, :] + jnp.concatenate(inter, axis=0)
        ms = jnp.mean(o * o, axis=-1, keepdims=True)
        o_ref[rows, :] = (o * lax.rsqrt(ms + EPS) * ng_ref[...] * _silu(g_ref[rows, :])).astype(BF16)

    def finish_group(i, _):
        for u in range(unroll):
            finish_step(i * unroll + u)
        return 0

    lax.fori_loop(0, n_tiles // unroll, finish_group, 0)


def _hgrn_scan(tok, q, lff, lfb, v, g, norm_gain, cast_stacks, cast_layer):
    n_rows, d = q.shape
    heads = d // LANES
    t = tok.t
    n_chunks = t // HGRN_CHUNK
    grid = (tok.batch, heads)
    side = _SideCasts(cast_stacks, cast_layer, grid)
    spec = pl.BlockSpec((None, t, LANES), lambda b, h: (b, 0, h))
    as3 = lambda a: a.reshape(tok.batch, t, d)
    in_specs = [spec] * 5 + [pl.BlockSpec((1, LANES), lambda b, h: (0, 0))]
    out, *casts = pl.pallas_call(
        side.wrap(functools.partial(_hgrn_scan_kernel, n_ctx=tok.n_ctx), len(in_specs), 1),
        grid=grid,
        in_specs=in_specs + side.in_specs,
        out_specs=[spec] + side.out_specs,
        out_shape=[jax.ShapeDtypeStruct((tok.batch, t, d), BF16)] + side.out_shapes,
        scratch_shapes=[pltpu.VMEM((t, LANES), F32), pltpu.VMEM((t, 2 * LANES), BF16),
                        pltpu.VMEM((n_chunks, LANES, 2 * LANES), F32), pltpu.VMEM((n_chunks, LANES, 2 * LANES), BF16),
                        pltpu.VMEM((n_chunks, SUBLANES, 2 * LANES), F32),
                        pltpu.VMEM((HGRN_CHUNK, LANES), F32), pltpu.VMEM((HGRN_CHUNK, LANES), F32)],
        compiler_params=_cparams("parallel", "arbitrary"),
        name="hgrn_scan",
    )(as3(q), as3(lff), as3(lfb), as3(v), as3(g), norm_gain.reshape(1, LANES), *side.stacks)
    return out.reshape(n_rows, d), casts


def _ffn_kernel(x_ref, o_ref, wo_ref, bo_ref, ng_ref, g1c, g1l, shc, shl, scc, scl, g2c, g2l, wi_ref, wf_ref, y_ref,
                *, tiles_per_batch, n_ctx):
    chunks, pick = _ctx_chunks(x_ref.shape[0], tiles_per_batch, n_ctx)
    x1 = [x_ref[rs, :] + pick(ic, g1c, g1l) * (_dot(o_ref[rs, :], wo_ref[...]) + bo_ref[...]) for rs, ic in chunks]
    h = jnp.concatenate([_norm_mod(x1_c, ng_ref[...], pick(ic, shc, shl), pick(ic, scc, scl)).astype(BF16)
                         for x1_c, (_, ic) in zip(x1, chunks)], axis=0)
    d_ff = wf_ref.shape[0]
    act = []
    for c0 in range(0, d_ff, FF_CHUNK):
        gate = _dot(h, wi_ref[:, c0:c0 + FF_CHUNK])
        up = _dot(h, wi_ref[:, d_ff + c0:d_ff + c0 + FF_CHUNK])
        act.append((_silu(gate) * up).astype(BF16))
    ff = _dot(jnp.concatenate(act, axis=1), wf_ref[...])
    for x1_c, (rs, ic) in zip(x1, chunks):
        y_ref[rs, :] = x1_c + pick(ic, g2c, g2l) * ff[rs, :]


def _out_proj_ffn(tok, x_all, o_all, mod, layer, w_o, b_o, norm_gain, w_in, w_out, lat_only):
    n_rows, d = x_all.shape
    d_ff = w_out.shape[0]
    assert d_ff % FF_CHUNK == 0
    rows_b = tok.seq if lat_only else tok.t
    tm = _row_tile(math.gcd(rows_b, tok.n_ctx) if lat_only else rows_b, FFN_MAX_ROWS)
    per_b = rows_b // tm
    first = tok.n_ctx // tm if lat_only else 0
    stream_row = pl.BlockSpec((tm, d), lambda i: ((i // per_b) * (tok.t // tm) + first + i % per_b, 0))
    out_row = pl.BlockSpec((tm, d), lambda i: (i, 0))
    o_row = stream_row if o_all.shape[0] == n_rows else out_row
    mods = [s for chunk in (2, 3, 4, 5) for s in tok.mod_specs(d, layer, chunk, per_b)]
    return pl.pallas_call(
        functools.partial(_ffn_kernel, tiles_per_batch=per_b, n_ctx=0 if lat_only else tok.n_ctx),
        grid=(tok.batch * per_b,),
        in_specs=[stream_row, o_row, _resident(w_o.shape), _vec_spec(d), _vec_spec(d)] + mods
                 + [_resident(w_in.shape), _resident(w_out.shape)],
        out_specs=out_row,
        out_shape=jax.ShapeDtypeStruct((tok.batch * rows_b, d), F32),
        compiler_params=_cparams("parallel"),
        name="out_proj_ffn",
    )(x_all, o_all, w_o.astype(BF16), b_o.reshape(1, d), norm_gain.reshape(1, d), *([mod] * 8), w_in, w_out)


def kernel(x, c, ctx, c_ctx, w_ada, b_ada, norm_gain, ffn_w_in, ffn_w_out, fnet_w_out, fnet_b_out,
           diff_w_in, diff_q_gain, diff_k_gain, diff_lambda, diff_subln_gain, diff_w_out,
           hgrn_w_in, hgrn_lower_bound, hgrn_norm_gain, hgrn_w_out,
           gqa_w_in, gqa_q_gain, gqa_k_gain, gqa_w_out):
    batch, seq, d = x.shape
    n_ctx = ctx.shape[1]
    depth = w_ada.shape[0]
    assert batch + 1 <= COND_ROWS and d % LANES == 0
    tok = _Tokens(batch, n_ctx, seq)

    cond = jnp.zeros((COND_ROWS, d), F32).at[0].set(c_ctx).at[1:1 + batch].set(c)
    mod = _ada_modulation(cond, w_ada, b_ada)
    x_all = None
    zero_bias = jnp.zeros((d,), F32)
    ffn_stacks = [ffn_w_in, ffn_w_out]

    for i in range(depth):
        m, j = i % N_MIXERS, i // N_MIXERS
        need_ctx = i < depth - 1
        if m == 0:
            if x_all is not None:
                stream = x_all.reshape(batch, tok.t, d)
                ctx, x = stream[:, :n_ctx], stream[:, n_ctx:]
            o, x_all = _fnet_mix(tok, ctx, x, mod, i, norm_gain[i, 0], d // FNET_GROUPS)
            ffn_w = [w[i].astype(BF16) for w in ffn_stacks]
            w_o, b_o = fnet_w_out[j], fnet_b_out[j]
        elif m == 1:
            lam_init = 0.8 - 0.6 * math.exp(-0.3 * i)
            qg = diff_q_gain[j].reshape(1, LANES) * (DIFF_HEAD_DIM ** -0.5 * LOG2_E)
            kg = diff_k_gain[j].reshape(1, LANES)
            q, k, v = _qkv_project(tok, x_all, mod, i, norm_gain[i, 0], diff_w_in[j], qg, kg,
                                   DIFF_HEAD_DIM, d, d)
            o, ffn_w = _diff_attention(tok, q, k, v, qg, kg, diff_lambda[j], diff_subln_gain[j], lam_init,
                                       ffn_stacks, i)
            w_o, b_o = diff_w_out[j], zero_bias
        elif m == 2:
            q, lff, lfb, v, g = _hgrn_project(tok, x_all, mod, i, norm_gain[i, 0], hgrn_w_in[j], hgrn_lower_bound)
            o, ffn_w = _hgrn_scan(tok, q, lff, lfb, v, g, hgrn_norm_gain[j], ffn_stacks, i)
            w_o, b_o = hgrn_w_out[j], zero_bias
        else:
            kv = (gqa_w_in.shape[-1] - d) // 2
            qg = gqa_q_gain[j].reshape(1, LANES) * (GQA_HEAD_DIM ** -0.5 * LOG2_E)
            kg = gqa_k_gain[j].reshape(1, LANES)
            q, k, v = _qkv_project(tok, x_all, mod, i, norm_gain[i, 0], gqa_w_in[j], qg, kg, GQA_HEAD_DIM, d, kv)
            o, ffn_w = _gqa_attention(tok, q, k, v, qg, kg, need_ctx, ffn_stacks, i)
            w_o, b_o = gqa_w_out[j], zero_bias
        x_all = _out_proj_ffn(tok, x_all, o, mod, i, w_o, b_o, norm_gain[i, 1], ffn_w[0], ffn_w[1],
                              lat_only=not need_ctx)
    return x_all.reshape(batch, seq, d)
```

```python
import functools
import math

import numpy as np
import jax
import jax.numpy as jnp
from jax import lax
from jax.experimental import pallas as pl
from jax.experimental.pallas import tpu as pltpu

F32 = jnp.float32
BF16 = jnp.bfloat16

EPS = 1e-6
GRID_W = 64
ROPE_THETA = 10000.0
N_MIXERS = 4

LANES = 128
SUBLANES = 8
BF16_SUBLANES = 16
TM = 256
COND_ROWS = 16
VMEM_LIMIT = 56 * 1024 * 1024

FNET_GROUPS = 8
DIFF_HEAD_DIM = 64
GQA_HEAD_DIM = 128
GQA_GROUP = 4
HGRN_CHUNK = 64
HGRN_HALF = 32
HGRN_TILE_UNROLL = 9
HGRN_SAFE_LOG2_DECAY = -115.0
FF_CHUNK = 256
FFN_MAX_ROWS = 768
PROJ_MAX_ROWS = 768
HGRN_PROJ_MAX_ROWS = 768
MXU_DIM = 256
LOG2_E = math.log2(math.e)
EXP2_SAFE_LOGIT = 100.0
BF16_SLACK = 1.02
DIFF_ATTN_UNROLL = 8
GQA_ATTN_UNROLL = 8


def _cparams(*sem):
    return pltpu.CompilerParams(dimension_semantics=sem, vmem_limit_bytes=VMEM_LIMIT)


def _resident(shape):
    nd = len(shape)
    return pl.BlockSpec(shape, lambda *_: (0,) * nd, pipeline_mode=pl.Buffered(1))


def _silu(x):
    return x * jax.nn.sigmoid(x)


def _norm_mod(x, gain, shift, scale):
    ms = jnp.mean(x * x, axis=-1, keepdims=True)
    y = x * lax.rsqrt(ms + EPS) * gain
    return y * (1.0 + scale) + shift


def _dot(a, b):
    return jnp.dot(a, b, preferred_element_type=F32)


def _dot_nt(a, b):
    return lax.dot_general(a, b, (((1,), (1,)), ((), ())), preferred_element_type=F32)


def _dot_tn(a, b):
    return lax.dot_general(a, b, (((0,), (0,)), ((), ())), preferred_element_type=F32)


def _ada_kernel(cond_ref, w_ref, b_ref, o_ref):
    a = _silu(cond_ref[...]).astype(BF16)
    o_ref[...] = _dot(a, w_ref[...].astype(BF16)) + b_ref[...]


def _ada_modulation(cond, w_ada, b_ada):
    depth, d, n6 = w_ada.shape
    bn = n6 // 4
    out = pl.pallas_call(
        _ada_kernel,
        grid=(depth, n6 // bn),
        in_specs=[
            pl.BlockSpec((COND_ROWS, d), lambda i, j: (0, 0)),
            pl.BlockSpec((None, d, bn), lambda i, j: (i, 0, j)),
            pl.BlockSpec((None, 1, bn), lambda i, j: (i, 0, j)),
        ],
        out_specs=pl.BlockSpec((None, COND_ROWS, bn), lambda i, j: (i, 0, j)),
        out_shape=jax.ShapeDtypeStruct((depth, COND_ROWS, n6), F32),
        compiler_params=_cparams("parallel", "parallel"),
        name="ada_modulation",
    )(cond, w_ada, b_ada.reshape(depth, 1, n6))
    return out.reshape(depth * COND_ROWS * 6, 1, d)


class _Tokens:
    def __init__(self, batch, n_ctx, seq):
        assert n_ctx % TM == 0 and seq % TM == 0
        self.batch, self.n_ctx, self.seq = batch, n_ctx, seq
        self.t = n_ctx + seq
        self.ctx_tiles = n_ctx // TM
        self.tiles = self.t // TM

    def mod_specs(self, d, layer, chunk, tiles_per_batch):
        ctx = pl.BlockSpec((None, 1, d), lambda i: (layer * COND_ROWS * 6 + chunk, 0, 0))
        lat = pl.BlockSpec((None, 1, d), lambda i: ((layer * COND_ROWS + 1 + i // tiles_per_batch) * 6 + chunk, 0, 0))
        return [ctx, lat]


def _ctx_chunks(rows, tiles_per_batch, n_ctx):
    row0 = (pl.program_id(0) % tiles_per_batch) * rows
    chunks = [(slice(r, r + TM), row0 + r < n_ctx) for r in range(0, rows, TM)]
    if n_ctx == 0:
        return chunks, lambda is_ctx, c_ref, l_ref: l_ref[...]
    return chunks, lambda is_ctx, c_ref, l_ref: jnp.where(is_ctx, c_ref[...], l_ref[...])


def _row_tile(n, cap, step=TM):
    return max(r for r in range(step, cap + 1, step) if n % r == 0)


def _vec_spec(n):
    return pl.BlockSpec((1, n), lambda *_: (0, 0))


class _SideCasts:
    def __init__(self, stacks, layer, grid):
        n0, n1 = (1,) * (2 - len(grid)) + tuple(grid)
        self.stacks, self.in_specs, self.out_specs, self.out_shapes = list(stacks), [], [], []
        for w in self.stacks:
            _, r, c = w.shape
            share = next(s for s in range(1, n1 + 1)
                         if n1 % s == 0 and r % (n0 * n1 // s) == 0 and (r * s // (n0 * n1)) % BF16_SUBLANES == 0)
            rows = r * share // (n0 * n1)
            assert c % LANES == 0
            self.in_specs.append(self._spec((None, rows, c), lambda blk, layer=layer: (layer, blk, 0), n1, share))
            self.out_specs.append(self._spec((rows, c), lambda blk: (blk, 0), n1, share))
            self.out_shapes.append(jax.ShapeDtypeStruct((r, c), BF16))

    @staticmethod
    def _spec(shape, index, n1, share):
        return pl.BlockSpec(shape, lambda *ids: index((sum(ids[:-1]) * n1 + ids[-1]) // share))

    def wrap(self, kernel_fn, n_in, n_out):
        n = len(self.stacks)

        def wrapped(*refs):
            for src, dst in zip(refs[n_in:n_in + n], refs[n_in + n + n_out:n_in + 2 * n + n_out]):
                dst[...] = src[...].astype(BF16)
            kernel_fn(*refs[:n_in], *refs[n_in + n:n_in + n + n_out], *refs[n_in + 2 * n + n_out:])
        return wrapped


def _dft_tables(n_ctx, seq, group_dim):
    def cs(n):
        k = np.arange(n, dtype=np.int64)
        ang = 2.0 * np.pi * ((k[:, None] * k[None, :]) % n).astype(np.float64) / n
        return np.cos(ang), np.sin(ang)
    cc, sc = cs(group_dim)
    chan = np.concatenate([cc, sc], axis=1)
    cx, sx = cs(n_ctx)
    pos_ctx = np.concatenate([cx, -sx], axis=1)
    cl, sl = cs(seq)
    pt = _half_spectrum_tile(seq)
    rows = (np.arange(seq // 2 // pt)[:, None] * pt + np.arange(pt + BF16_SUBLANES)[None, :]) % seq
    rev = np.zeros((pt, pt + BF16_SUBLANES), np.float32)
    rev[np.arange(pt), pt - np.arange(pt)] = 1.0
    as_bf16 = lambda a: jnp.asarray(a.astype(np.float32)).astype(BF16)
    return as_bf16(chan), as_bf16(pos_ctx), as_bf16(cl[rows]), as_bf16(sl[rows]), as_bf16(rev)


def _half_spectrum_tile(seq):
    return min(TM, seq // 2)


def _fnet_kernel(ctx_ref, lat_ref, ng_ref, shc_ref, scc_ref, shl_ref, scl_ref, chan_ref, pctx_ref, cos_ref, sin_ref,
                 rev_ref, y_ref, xs_ref, ab_ctx, ab_lat, *, group_dim):
    j = pl.program_id(1)
    n_ctx, d = ctx_ref.shape
    seq = lat_ref.shape[0]
    ctx_tiles = n_ctx // TM
    pt = _half_spectrum_tile(seq)
    p_tiles = seq // 2 // pt

    @pl.when(j == 0)
    def _():
        for src, sh, sc, dst, n in ((ctx_ref, shc_ref, scc_ref, ab_ctx, n_ctx), (lat_ref, shl_ref, scl_ref, ab_lat, seq)):
            for row0 in range(0, n, TM):
                h = _norm_mod(src[row0:row0 + TM, :], ng_ref[...], sh[...], sc[...]).astype(BF16)
                for g in range(d // group_dim):
                    cols = slice(g * group_dim, (g + 1) * group_dim)
                    ab = _dot(h[:, cols], chan_ref[...])
                    dst[row0:row0 + TM, cols] = ab[:, :group_dim].astype(BF16)
                    dst[n + row0:n + row0 + TM, cols] = ab[:, group_dim:].astype(BF16)

    @pl.when(j < ctx_tiles)
    def _():
        rows = pl.ds(pl.multiple_of(j * TM, TM), TM)
        y = _dot(pctx_ref[rows, :], ab_ctx[...]) * (1.0 / math.sqrt(n_ctx * group_dim))
        y_ref[rows, :] = y.astype(BF16)
        xs_ref[...] = ctx_ref[rows, :]

    @pl.when(j >= ctx_tiles)
    def _():
        xs_ref[...] = lat_ref[pl.ds(pl.multiple_of((j - ctx_tiles) * TM, TM), TM), :]

    first_p_step = pl.num_programs(1) - p_tiles

    @pl.when(j >= first_p_step)
    def _():
        tile = j - first_p_step
        scale = 1.0 / math.sqrt(seq * group_dim)
        p_part = _dot(cos_ref[...], ab_lat[0:seq, :])
        q_part = _dot(sin_ref[...], ab_lat[seq:2 * seq, :])
        y_ref[pl.ds(pl.multiple_of(n_ctx + tile * pt, pt), pt), :] = ((p_part - q_part)[0:pt] * scale).astype(BF16)
        mirrored = _dot(rev_ref[...], ((p_part + q_part) * scale).astype(BF16))
        y_ref[pl.ds(pl.multiple_of(n_ctx + seq - (tile + 1) * pt, pt), pt), :] = mirrored.astype(BF16)


def _fnet_mix(tok, ctx, x, mod, layer, norm_gain, group_dim):
    b, t, d = tok.batch, tok.t, x.shape[-1]
    chan, pos_ctx, cos_lat, sin_lat, rev = _dft_tables(tok.n_ctx, tok.seq, group_dim)
    ct, p_tiles = tok.ctx_tiles, cos_lat.shape[0]
    assert p_tiles < tok.tiles

    def mod_spec(chunk, is_ctx):
        return pl.BlockSpec((None, 1, d), lambda i, j: ((layer * COND_ROWS + (0 if is_ctx else 1 + i)) * 6 + chunk, 0, 0))

    table_spec = pl.BlockSpec((None,) + cos_lat.shape[1:],
                              lambda i, j: (jnp.maximum(j - (tok.tiles - p_tiles), 0), 0, 0))
    y, xs = pl.pallas_call(
        functools.partial(_fnet_kernel, group_dim=group_dim),
        grid=(b, tok.tiles),
        in_specs=[
            pl.BlockSpec((None, tok.n_ctx, d), lambda i, j: (i, 0, 0)),
            pl.BlockSpec((None, tok.seq, d), lambda i, j: (i, 0, 0)),
            pl.BlockSpec((1, d), lambda i, j: (0, 0)),
            mod_spec(0, True), mod_spec(1, True), mod_spec(0, False), mod_spec(1, False),
            _resident(chan.shape), _resident(pos_ctx.shape), table_spec, table_spec, _resident(rev.shape),
        ],
        out_specs=[pl.BlockSpec((None, t, d), lambda i, j: (i, 0, 0)),
                   pl.BlockSpec((TM, d), lambda i, j: (i * tok.tiles + j, 0))],
        out_shape=[jax.ShapeDtypeStruct((b, t, d), BF16), jax.ShapeDtypeStruct((b * t, d), F32)],
        scratch_shapes=[pltpu.VMEM((2 * tok.n_ctx, d), BF16), pltpu.VMEM((2 * tok.seq, d), BF16)],
        compiler_params=_cparams("parallel", "arbitrary"),
        name="fnet_mix",
    )(ctx, x, norm_gain.reshape(1, d), mod, mod, mod, mod, chan, pos_ctx, cos_lat, sin_lat, rev)
    return y.reshape(b * t, d), xs


def _rope_tables(tok, head_dim):
    rows = tok.seq // GRID_W
    row = jnp.repeat(jnp.arange(rows, dtype=F32), GRID_W)
    col = jnp.tile(jnp.arange(GRID_W, dtype=F32), rows)
    n_freq = head_dim // 4
    inv_freq = ROPE_THETA ** (-jnp.arange(n_freq, dtype=F32) / n_freq)
    ang = jnp.concatenate([row[:, None] * inv_freq, col[:, None] * inv_freq], axis=-1)
    cos = jnp.concatenate([jnp.ones((tok.n_ctx, head_dim // 2), F32), jnp.cos(ang)], axis=0)
    sin = jnp.concatenate([jnp.zeros((tok.n_ctx, head_dim // 2), F32), jnp.sin(ang)], axis=0)
    reps = LANES // head_dim
    cos = jnp.tile(jnp.concatenate([cos, cos], axis=-1), (1, reps))
    sin = jnp.tile(jnp.concatenate([-sin, sin], axis=-1), (1, reps))
    return cos, sin


def _group_mean_matrix(group, width):
    g = np.arange(width) // group
    return jnp.asarray((g[:, None] == g[None, :]).astype(np.float32) / group).astype(BF16)


def _rope(yn, cos, sin, head_dim):
    half = head_dim // 2
    if head_dim == LANES:
        partner = pltpu.roll(yn, half, 1)
    else:
        lane = lax.broadcasted_iota(jnp.int32, yn.shape, 1)
        partner = jnp.where(lane % head_dim < half, pltpu.roll(yn, LANES - half, 1), pltpu.roll(yn, half, 1))
    return yn * cos + partner * sin


def _qkv_kernel(x_ref, ng_ref, shc, shl, scc, scl, w_ref, qg_ref, kg_ref, cos_ref, sin_ref, gm_ref,
                q_ref, k_ref, vt_ref, *, head_dim, tiles_per_batch, n_ctx):
    chunks, pick = _ctx_chunks(x_ref.shape[0], tiles_per_batch, n_ctx)
    h = jnp.concatenate([_norm_mod(x_ref[rs, :], ng_ref[...], pick(ic, shc, shl), pick(ic, scc, scl)).astype(BF16)
                         for rs, ic in chunks], axis=0)
    cos, sin, gm = cos_ref[...], sin_ref[...], gm_ref[...]
    nq, nk = q_ref.shape[-1], k_ref.shape[-1]
    for o_ref, g_ref, col0, n in ((q_ref, qg_ref, 0, nq), (k_ref, kg_ref, nq, nk)):
        y = _dot(h, w_ref[:, col0:col0 + n])
        for c0 in range(0, n, MXU_DIM):
            yb = y[:, c0:c0 + MXU_DIM]
            yn = yb * lax.rsqrt(_dot((yb * yb).astype(BF16), gm) + EPS) * g_ref[...]
            for l0 in range(0, MXU_DIM, LANES):
                o_ref[:, c0 + l0:c0 + l0 + LANES] = _rope(yn[:, l0:l0 + LANES], cos, sin, head_dim).astype(BF16)
    vt_ref[...] = _dot(h, w_ref[:, nq + nk:]).astype(BF16).T


def _qkv_project(tok, x_all, mod, layer, norm_gain, w_in, q_gain, k_gain, head_dim, nq, nk):
    n_rows, d = x_all.shape
    nv = w_in.shape[1] - nq - nk
    assert nq % MXU_DIM == 0 and nk % MXU_DIM == 0
    cos, sin = _rope_tables(tok, head_dim)
    tm = _row_tile(tok.t, PROJ_MAX_ROWS)
    per_b = tok.t // tm
    row_spec = lambda n: pl.BlockSpec((tm, n), lambda i: (i, 0))
    tab_spec = pl.BlockSpec((tm, LANES), lambda i: (i % per_b, 0))
    gain2 = lambda g: jnp.tile(g, (1, MXU_DIM // LANES))
    return pl.pallas_call(
        functools.partial(_qkv_kernel, head_dim=head_dim, tiles_per_batch=per_b, n_ctx=tok.n_ctx),
        grid=(n_rows // tm,),
        in_specs=[row_spec(d), _vec_spec(d)] + tok.mod_specs(d, layer, 0, per_b) + tok.mod_specs(d, layer, 1, per_b)
                 + [_resident(w_in.shape), _vec_spec(MXU_DIM), _vec_spec(MXU_DIM), tab_spec, tab_spec,
                    _resident((MXU_DIM, MXU_DIM))],
        out_specs=[row_spec(nq), row_spec(nk), pl.BlockSpec((None, nv, tm), lambda i: (i // per_b, 0, i % per_b))],
        out_shape=[jax.ShapeDtypeStruct((n_rows, nq), BF16), jax.ShapeDtypeStruct((n_rows, nk), BF16),
                   jax.ShapeDtypeStruct((tok.batch, nv, tok.t), BF16)],
        compiler_params=_cparams("parallel"),
        name=f"qkv_project_hd{head_dim}",
    )(x_all, norm_gain.reshape(1, d), mod, mod, mod, mod, w_in, gain2(q_gain), gain2(k_gain),
      cos, sin, _group_mean_matrix(head_dim, MXU_DIM))


def _softmax_parts(q, k):
    s = _dot_nt(q, k)
    p = jnp.exp2(s - jnp.max(s, axis=-1, keepdims=True))
    return p, jnp.sum(p, axis=-1, keepdims=True)


def _softmax_parts_bounded(k, q, vt):
    p = jnp.exp2(_dot_nt(k, q))
    l = jnp.sum(p, axis=0, keepdims=True)
    return _dot(vt, p.astype(BF16)) * (1.0 / l)


def _logit_bound(qg_ref, kg_ref, head_dim):
    return BF16_SLACK * head_dim * jnp.max(jnp.abs(qg_ref[...])) * jnp.max(jnp.abs(kg_ref[...]))


def _diff_attn_kernel(lp_ref, sg_ref, qg_ref, kg_ref, q_ref, k_ref, vt, o_ref, *, lam_init, n_ctx):
    lp = lp_ref[...]
    lam = (jnp.exp(jnp.sum(lp[0:1] * lp[1:2], axis=-1, keepdims=True))
           - jnp.exp(jnp.sum(lp[2:3] * lp[3:4], axis=-1, keepdims=True)) + lam_init)
    t = k_ref.shape[0]
    lane = lax.broadcasted_iota(jnp.int32, (TM, LANES), 1)
    bounded = _logit_bound(qg_ref, kg_ref, DIFF_HEAD_DIM) <= EXP2_SAFE_LOGIT

    def split_q(rows):
        q = q_ref[rows, :]
        zero = jnp.zeros_like(q)
        return jnp.where(lane < DIFF_HEAD_DIM, q, zero), jnp.where(lane >= DIFF_HEAD_DIM, q, zero)

    def tile_bounded(rows, nk):
        o12 = _softmax_parts_bounded(k_ref[0:nk, :], jnp.concatenate(split_q(rows), axis=0), vt[:, 0:nk])
        ot = o12[:, :TM] - lam * o12[:, TM:]
        ms = jnp.mean(ot * ot, axis=0, keepdims=True)
        o_ref[rows, :] = ((ot * lax.rsqrt(ms + EPS)).T * sg_ref[...]).astype(BF16)

    def tile_exact(rows, nk):
        k = k_ref[0:nk, :]
        q1, q2 = split_q(rows)
        p1, l1 = _softmax_parts(q1, k)
        p2, l2 = _softmax_parts(q2, k)
        w = p1 * (1.0 / l1) - p2 * (lam / l2)
        o = _dot(w.astype(BF16), vt[:, 0:nk].T)
        ms = jnp.mean(o * o, axis=-1, keepdims=True)
        o_ref[rows, :] = (o * lax.rsqrt(ms + EPS) * sg_ref[...]).astype(BF16)

    @pl.when(bounded)
    def _():
        _for_query_tiles(tile_bounded, n_ctx, t, 0, DIFF_ATTN_UNROLL)

    @pl.when(jnp.logical_not(bounded))
    def _():
        _for_query_tiles(tile_exact, n_ctx, t, 0, 1)


def _for_query_tiles(tile, n_ctx, t, first_row, unroll):
    for r0 in range(first_row, n_ctx, TM):
        tile(pl.ds(r0, TM), n_ctx)
    lat0 = max(first_row, n_ctx)
    n_lat = (t - lat0) // TM
    unroll = unroll if n_lat % unroll == 0 else 1

    def group(i, _):
        for u in range(unroll):
            tile(pl.ds(pl.multiple_of(lat0 + (i * unroll + u) * TM, TM), TM), t)
        return 0

    lax.fori_loop(0, n_lat // unroll, group, 0)


def _diff_attention(tok, q, k, vt, q_gain, k_gain, lam_par, subln_gain, lam_init, cast_stacks, cast_layer):
    n_rows, d = q.shape
    heads = d // LANES
    t = tok.t
    grid = (tok.batch, heads)
    side = _SideCasts(cast_stacks, cast_layer, grid)
    as3 = lambda a: a.reshape(tok.batch, t, d)
    spec = pl.BlockSpec((None, t, LANES), lambda b, h: (b, 0, h))
    vt_spec = pl.BlockSpec((None, LANES, t), lambda b, h: (b, h, 0))
    vec = pl.BlockSpec((1, LANES), lambda b, h: (0, 0))
    in_specs = [pl.BlockSpec(lam_par.shape, lambda b, h: (0, 0)), vec, vec, vec, spec, spec, vt_spec]
    out, *casts = pl.pallas_call(
        side.wrap(functools.partial(_diff_attn_kernel, lam_init=lam_init, n_ctx=tok.n_ctx), len(in_specs), 1),
        grid=grid,
        in_specs=in_specs + side.in_specs,
        out_specs=[spec] + side.out_specs,
        out_shape=[jax.ShapeDtypeStruct((tok.batch, t, d), BF16)] + side.out_shapes,
        compiler_params=_cparams("parallel", "arbitrary"),
        name="diff_attention",
    )(lam_par, (subln_gain * (1.0 - lam_init)).reshape(1, LANES), q_gain, k_gain, as3(q), as3(k), vt, *side.stacks)
    return out.reshape(n_rows, d), casts


def _gqa_attn_kernel(qg_ref, kg_ref, q_ref, k_ref, vt, o_ref, *, n_ctx, first_row):
    t = k_ref.shape[0]
    bounded = _logit_bound(qg_ref, kg_ref, GQA_HEAD_DIM) <= EXP2_SAFE_LOGIT

    def out_rows(rows):
        if first_row == 0:
            return rows
        start = rows.start - first_row
        return pl.ds(start if isinstance(start, int) else pl.multiple_of(start, TM), TM)

    def tile_bounded(rows, nk):
        k, vt_k = k_ref[0:nk, :], vt[:, 0:nk]
        for g in range(0, GQA_GROUP, 2):
            q2h = jnp.concatenate([q_ref[rows, g * LANES:(g + 1) * LANES],
                                   q_ref[rows, (g + 1) * LANES:(g + 2) * LANES]], axis=0)
            o2h = _softmax_parts_bounded(k, q2h, vt_k)
            o_ref[out_rows(rows), g * LANES:(g + 1) * LANES] = o2h[:, :TM].T.astype(BF16)
            o_ref[out_rows(rows), (g + 1) * LANES:(g + 2) * LANES] = o2h[:, TM:].T.astype(BF16)

    def tile_exact(rows, nk):
        k, v = k_ref[0:nk, :], vt[:, 0:nk].T
        for g in range(GQA_GROUP):
            cols = slice(g * LANES, (g + 1) * LANES)
            p, l = _softmax_parts(q_ref[rows, cols], k)
            o_ref[out_rows(rows), cols] = (_dot(p.astype(BF16), v) * (1.0 / l)).astype(BF16)

    @pl.when(bounded)
    def _():
        _for_query_tiles(tile_bounded, n_ctx, t, first_row, GQA_ATTN_UNROLL)

    @pl.when(jnp.logical_not(bounded))
    def _():
        _for_query_tiles(tile_exact, n_ctx, t, first_row, 1)


def _gqa_attention(tok, q, k, vt, q_gain, k_gain, need_ctx, cast_stacks, cast_layer):
    d = q.shape[1]
    kv_heads = k.shape[1] // LANES
    t = tok.t
    grid = (tok.batch, kv_heads)
    side = _SideCasts(cast_stacks, cast_layer, grid)
    first_row = 0 if need_ctx else tok.n_ctx
    as3 = lambda a: a.reshape(tok.batch, t, a.shape[1])
    k_spec = pl.BlockSpec((None, t, LANES), lambda b, h: (b, 0, h))
    vt_spec = pl.BlockSpec((None, LANES, t), lambda b, h: (b, h, 0))
    q_spec = pl.BlockSpec((None, t, GQA_GROUP * LANES), lambda b, h: (b, 0, h))
    o_spec = pl.BlockSpec((None, t - first_row, GQA_GROUP * LANES), lambda b, h: (b, 0, h))
    vec = pl.BlockSpec((1, LANES), lambda b, h: (0, 0))
    in_specs = [vec, vec, q_spec, k_spec, vt_spec]
    out, *casts = pl.pallas_call(
        side.wrap(functools.partial(_gqa_attn_kernel, n_ctx=tok.n_ctx, first_row=first_row), len(in_specs), 1),
        grid=grid,
        in_specs=in_specs + side.in_specs,
        out_specs=[o_spec] + side.out_specs,
        out_shape=[jax.ShapeDtypeStruct((tok.batch, t - first_row, d), BF16)] + side.out_shapes,
        compiler_params=_cparams("parallel", "arbitrary"),
        name="gqa_attention",
    )(q_gain, k_gain, as3(q), as3(k), vt, *side.stacks)
    return out.reshape(tok.batch * (t - first_row), d), casts


def _hgrn_proj_kernel(x_ref, ng_ref, shc, shl, scc, scl, w_ref, lb_ref, q_ref, lff_ref, lfb_ref, v_ref, g_ref,
                      *, layer_idx, depth, tiles_per_batch, n_ctx):
    d = x_ref.shape[-1]
    chunks, pick = _ctx_chunks(x_ref.shape[0], tiles_per_batch, n_ctx)
    lbs = []
    for direction in range(2):
        rows = [lb_ref[direction * depth + i:direction * depth + i + 1, :] for i in range(depth)]
        m = functools.reduce(jnp.maximum, rows)
        e = [jnp.exp(r - m) for r in rows]
        lbs.append(sum(e[1:layer_idx + 1]) / sum(e) if layer_idx > 0 else jnp.zeros_like(m))

    def normed(rs, ic):
        return _norm_mod(x_ref[rs, :], ng_ref[...], pick(ic, shc, shl), pick(ic, scc, scl)).astype(BF16)

    def project(rs, h):
        q_ref[rs, :] = _silu(_dot(h, w_ref[:, 0:d]))
        for direction, o_ref in enumerate((lff_ref, lfb_ref)):
            z = _dot(h, w_ref[:, (1 + direction) * d:(2 + direction) * d])
            lb = lbs[direction]
            o_ref[rs, :] = jnp.log2(lb + (1.0 - lb) * jax.nn.sigmoid(z))
        v_ref[rs, :] = _dot(h, w_ref[:, 3 * d:4 * d]).astype(BF16)
        g_ref[rs, :] = _dot(h, w_ref[:, 4 * d:5 * d])

    h = normed(*chunks[0])
    for k, (rs, _) in enumerate(chunks):
        h_next = normed(*chunks[k + 1]) if k + 1 < len(chunks) else None
        project(rs, h)
        h = h_next


def _hgrn_project(tok, x_all, mod, layer, norm_gain, w_in, lower_bound):
    n_rows, d = x_all.shape
    depth = lower_bound.shape[1]
    tm = _row_tile(tok.t, HGRN_PROJ_MAX_ROWS)
    per_b = tok.t // tm
    row_spec = pl.BlockSpec((tm, d), lambda i: (i, 0))
    return pl.pallas_call(
        functools.partial(_hgrn_proj_kernel, layer_idx=layer, depth=depth, tiles_per_batch=per_b, n_ctx=tok.n_ctx),
        grid=(n_rows // tm,),
        in_specs=[row_spec, _vec_spec(d)] + tok.mod_specs(d, layer, 0, per_b) + tok.mod_specs(d, layer, 1, per_b)
                 + [_resident(w_in.shape), pl.BlockSpec((2 * depth, d), lambda i: (0, 0))],
        out_specs=[row_spec] * 5,
        out_shape=[jax.ShapeDtypeStruct((n_rows, d), dt) for dt in (F32, F32, F32, BF16, F32)],
        compiler_params=_cparams("parallel"),
        name="hgrn_project",
    )(x_all, norm_gain.reshape(1, d), mod, mod, mod, mod, w_in, lower_bound.reshape(2 * depth, d))


def _split2(x):
    hi = x.astype(BF16)
    return hi, (x - hi.astype(F32)).astype(BF16)


def _hgrn_scan_kernel(q_ref, lff_ref, lfb_ref, v_ref, g_ref, ng_ref, o_ref,
                      acc, qs, xs, st, ds, cum, kk, *, n_ctx):
    t, dk = q_ref.shape
    c, hh = HGRN_CHUNK, HGRN_HALF
    n_chunks, ctx_chunks = t // c, n_ctx // c
    cpt = TM // c
    row = lax.broadcasted_iota(jnp.int32, (TM, TM), 0)
    col = lax.broadcasted_iota(jnp.int32, (TM, TM), 1)
    same_chunk = (row // c) == (col // c)
    same_half = (row // hh) == (col // hh)
    cross_half = same_chunk & jnp.logical_not(same_half)
    cum_mat = jnp.where(same_chunk & (col <= row), 1.0, 0.0).astype(BF16)
    in_chunk = lax.broadcasted_iota(jnp.int32, (cpt, c, dk), 1)
    row_c = lax.broadcasted_iota(jnp.int32, (c, c), 0)
    col_c = lax.broadcasted_iota(jnp.int32, (c, c), 1)

    def seg_min(ref):
        return jnp.min(ref[...].reshape(t // hh, hh, dk).sum(axis=1))
    unsafe = jnp.minimum(seg_min(lff_ref), seg_min(lfb_ref)) < HGRN_SAFE_LOG2_DECAY

    as4 = lambda a: a.reshape(cpt, c, dk)
    as8 = lambda a: a.reshape(2 * cpt, hh, dk)
    flat = lambda a: a.reshape(TM, dk)
    half_idx = lax.broadcasted_iota(jnp.int32, (2 * cpt, 1, 1), 0)
    per_half = lambda a: jnp.concatenate([a[i // 2:i // 2 + 1] for i in range(2 * cpt)], axis=0)
    n_tiles = t // TM
    unroll = next(u for u in (HGRN_TILE_UNROLL, 3, 2, 1) if n_tiles % u == 0)

    def cumulative(ti):
        rows = pl.ds(pl.multiple_of(ti * TM, TM), TM)
        lf_f, lf_b = lff_ref[rows, :], lfb_ref[rows, :]
        hi, lo = _split2(jnp.concatenate([lf_f, lf_b], axis=1))
        pre = _dot(cum_mat, hi) + _dot(cum_mat, lo)
        pre_b = as4(pre[:, dk:])
        bc_b = pre_b[:, c - 1:c, :] - pre_b + as4(lf_b)
        return rows, (as4(lf_f), as4(lf_b)), (as4(pre[:, :dk]), bc_b)

    def tile_operands(cumulated):
        rows, lfs, bcs = cumulated
        q8, v = as8(q_ref[rows, :]), v_ref[rows, :]
        q_d2, k_d2, q_s, k_bar, decay, q_o, k_o = [], [], [], [], [], [], []
        for lf4, bc4, reverse in zip(lfs, bcs, (False, True)):
            edge, last = (hh, 0) if reverse else (hh - 1, c - 1)
            second = (half_idx % 2 == 0) if reverse else (half_idx % 2 == 1)
            b_last, b_edge = bc4[:, last:last + 1, :], bc4[:, edge:edge + 1, :]
            b_last8, b_edge8 = per_half(b_last), per_half(b_edge)
            bc8, k8 = as8(bc4), 1.0 - jnp.exp2(as8(lf4))
            ref = bc8[:, hh // 2:hh // 2 + 1, :]
            rel = bc8 - ref
            q_d, k_d = q8 * jnp.exp2(rel), k8 * jnp.exp2(-rel)
            q_d2.append(flat(q_d).astype(BF16))
            k_d2.append(flat(k_d).astype(BF16))
            q_s.append(flat(q_d * jnp.exp2(ref)))
            k_bar.append(flat(k_d * jnp.exp2(b_last8 - ref)))
            decay.append(jnp.exp2(b_last))
            q_o.append(flat(q_d * jnp.where(second, jnp.exp2(ref - b_edge8), 0.0)))
            k_o.append(flat(k_d * jnp.where(second, 0.0, jnp.exp2(b_edge8 - ref))))
        stack = lambda parts: jnp.concatenate(parts, axis=1).astype(BF16)
        return rows, v, q_d2, k_d2, stack(q_o), stack(k_o), stack(q_s), stack(k_bar), jnp.concatenate(decay, axis=2)

    def tile_matmuls(ti, operands):
        rows, v, q_d2, k_d2, q_o2, k_o2, q_s2, k_bar2, decay2 = operands
        a_same = (jnp.where(col <= row, _dot_nt(q_d2[0], k_d2[0]), 0.0)
                  + jnp.where(col >= row, _dot_nt(q_d2[1], k_d2[1]), 0.0))
        a = jnp.where(same_half, a_same, jnp.where(cross_half, _dot_nt(q_o2, k_o2), 0.0))
        acc[rows, :] = _dot(a.astype(BF16), v)
        store_state_inputs(ti, rows, v, q_s2, k_bar2, decay2)

    def store_state_inputs(ti, rows, v, q_s2, k_bar2, decay2):
        qs[rows, :] = q_s2
        for cc in range(cpt):
            ci = ti * cpt + cc
            xs[ci] = _dot_tn(v[cc * c:(cc + 1) * c, :], k_bar2[cc * c:(cc + 1) * c, :])
            ds[ci] = jnp.broadcast_to(decay2[cc], ds.shape[1:])

    def tile_exact(ti, _):
        rows, lfs, bcs = cumulative(ti)
        q4, v = as4(q_ref[rows, :]), v_ref[rows, :]
        q_s, k_bar, decay = [], [], []
        for lf4, bc4, reverse in zip(lfs, bcs, (False, True)):
            b_last = bc4[:, 0:1, :] if reverse else bc4[:, c - 1:c, :]
            q_s.append(flat(q4 * jnp.exp2(bc4)))
            k_bar.append(flat((1.0 - jnp.exp2(lf4)) * jnp.exp2(b_last - bc4)))
            decay.append(jnp.exp2(b_last))
        stack = lambda parts: jnp.concatenate(parts, axis=1).astype(BF16)
        store_state_inputs(ti, rows, v, stack(q_s), stack(k_bar), jnp.concatenate(decay, axis=2))
        for cc in range(cpt):
            a = jnp.zeros((c, c), F32)
            q_c = q4[cc]
            for lf4, bc4, reverse in zip(lfs, bcs, (False, True)):
                bc_c = bc4[cc]
                cum[...] = bc_c
                kk[...] = 1.0 - jnp.exp2(lf4[cc])

                def col_step(s, a_):
                    d_s = jnp.exp2(jnp.minimum(bc_c - cum[pl.ds(s, 1), :], 0.0))
                    w = jnp.sum(q_c * kk[pl.ds(s, 1), :] * d_s, axis=-1, keepdims=True)
                    return jnp.where(col_c == s, w, a_)
                a_dir = lax.fori_loop(0, c, col_step, jnp.zeros((c, c), F32))
                a = a + jnp.where((col_c >= row_c) if reverse else (col_c <= row_c), a_dir, 0.0)
            crow = pl.ds(pl.multiple_of(ti * TM + cc * c, c), c)
            acc[crow, :] = _dot(a.astype(BF16), v[cc * c:(cc + 1) * c, :])
        return 0

    def tile_group(i, _):
        tiles = [i * unroll + u for u in range(unroll)]
        cums = {0: cumulative(tiles[0])}
        if unroll > 1:
            cums[1] = cumulative(tiles[1])
        ops = {0: tile_operands(cums[0])}
        for u in range(unroll):
            if u + 2 < unroll:
                cums[u + 2] = cumulative(tiles[u + 2])
            if u + 1 < unroll:
                ops[u + 1] = tile_operands(cums[u + 1])
            tile_matmuls(tiles[u], ops[u])
        return 0

    lax.fori_loop(0, n_tiles // unroll, tile_group, 0)

    @pl.when(unsafe)
    def _():
        lax.fori_loop(0, n_tiles, tile_exact, 0)

    def state_step(i, carry):
        s_f, s_b = carry
        cf = i
        cb = jnp.where(i < ctx_chunks, ctx_chunks - 1 - i, n_chunks - 1 + ctx_chunks - i)
        st[cf, :, 0:dk] = s_f.astype(BF16)
        st[cb, :, dk:2 * dk] = s_b.astype(BF16)
        s_f = s_f * ds[cf, 0:1, 0:dk] + xs[cf, :, 0:dk]
        s_b = s_b * ds[cb, 0:1, dk:2 * dk] + xs[cb, :, dk:2 * dk]
        return s_f, s_b

    zero = jnp.zeros((dk, dk), F32)
    lax.fori_loop(0, n_chunks, state_step, (zero, zero))

    def finish_step(ti):
        rows = pl.ds(pl.multiple_of(ti * TM, TM), TM)
        inter = [_dot_nt(qs[pl.ds(pl.multiple_of(ti * TM + cc * c, c), c), :], st[ti * cpt + cc]) for cc in range(cpt)]
        o = acc[rows, :] + jnp.concatenate(inter, axis=0)
        ms = jnp.mean(o * o, axis=-1, keepdims=True)
        o_ref[rows, :] = (o * lax.rsqrt(ms + EPS) * ng_ref[...] * _silu(g_ref[rows, :])).astype(BF16)

    def finish_group(i, _):
        for u in range(unroll):
            finish_step(i * unroll + u)
        return 0

    lax.fori_loop(0, n_tiles // unroll, finish_group, 0)


def _hgrn_scan(tok, q, lff, lfb, v, g, norm_gain, cast_stacks, cast_layer):
    n_rows, d = q.shape
    heads = d // LANES
    t = tok.t
    n_chunks = t // HGRN_CHUNK
    grid = (tok.batch, heads)
    side = _SideCasts(cast_stacks, cast_layer, grid)
    spec = pl.BlockSpec((None, t, LANES), lambda b, h: (b, 0, h))
    as3 = lambda a: a.reshape(tok.batch, t, d)
    in_specs = [spec] * 5 + [pl.BlockSpec((1, LANES), lambda b, h: (0, 0))]
    out, *casts = pl.pallas_call(
        side.wrap(functools.partial(_hgrn_scan_kernel, n_ctx=tok.n_ctx), len(in_specs), 1),
        grid=grid,
        in_specs=in_specs + side.in_specs,
        out_specs=[spec] + side.out_specs,
        out_shape=[jax.ShapeDtypeStruct((tok.batch, t, d), BF16)] + side.out_shapes,
        scratch_shapes=[pltpu.VMEM((t, LANES), F32), pltpu.VMEM((t, 2 * LANES), BF16),
                        pltpu.VMEM((n_chunks, LANES, 2 * LANES), F32), pltpu.VMEM((n_chunks, LANES, 2 * LANES), BF16),
                        pltpu.VMEM((n_chunks, SUBLANES, 2 * LANES), F32),
                        pltpu.VMEM((HGRN_CHUNK, LANES), F32), pltpu.VMEM((HGRN_CHUNK, LANES), F32)],
        compiler_params=_cparams("parallel", "arbitrary"),
        name="hgrn_scan",
    )(as3(q), as3(lff), as3(lfb), as3(v), as3(g), norm_gain.reshape(1, LANES), *side.stacks)
    return out.reshape(n_rows, d), casts


def _ffn_kernel(x_ref, o_ref, wo_ref, bo_ref, ng_ref, g1c, g1l, shc, shl, scc, scl, g2c, g2l, wi_ref, wf_ref, y_ref,
                *, tiles_per_batch, n_ctx):
    chunks, pick = _ctx_chunks(x_ref.shape[0], tiles_per_batch, n_ctx)
    x1 = [x_ref[rs, :] + pick(ic, g1c, g1l) * (_dot(o_ref[rs, :], wo_ref[...]) + bo_ref[...]) for rs, ic in chunks]
    h = jnp.concatenate([_norm_mod(x1_c, ng_ref[...], pick(ic, shc, shl), pick(ic, scc, scl)).astype(BF16)
                         for x1_c, (_, ic) in zip(x1, chunks)], axis=0)
    d_ff = wf_ref.shape[0]
    act = []
    for c0 in range(0, d_ff, FF_CHUNK):
        gate = _dot(h, wi_ref[:, c0:c0 + FF_CHUNK])
        up = _dot(h, wi_ref[:, d_ff + c0:d_ff + c0 + FF_CHUNK])
        act.append((_silu(gate) * up).astype(BF16))
    ff = _dot(jnp.concatenate(act, axis=1), wf_ref[...])
    for x1_c, (rs, ic) in zip(x1, chunks):
        y_ref[rs, :] = x1_c + pick(ic, g2c, g2l) * ff[rs, :]


def _out_proj_ffn(tok, x_all, o_all, mod, layer, w_o, b_o, norm_gain, w_in, w_out, lat_only, cast_stacks, cast_layer):
    n_rows, d = x_all.shape
    d_ff = w_out.shape[0]
    assert d_ff % FF_CHUNK == 0
    rows_b = tok.seq if lat_only else tok.t
    tm = _row_tile(math.gcd(rows_b, tok.n_ctx) if lat_only else rows_b, FFN_MAX_ROWS)
    per_b = rows_b // tm
    first = tok.n_ctx // tm if lat_only else 0
    stream_row = pl.BlockSpec((tm, d), lambda i: ((i // per_b) * (tok.t // tm) + first + i % per_b, 0))
    out_row = pl.BlockSpec((tm, d), lambda i: (i, 0))
    o_row = stream_row if o_all.shape[0] == n_rows else out_row
    mods = [s for chunk in (2, 3, 4, 5) for s in tok.mod_specs(d, layer, chunk, per_b)]
    grid = (tok.batch * per_b,)
    side = _SideCasts(cast_stacks, cast_layer, grid)
    in_specs = ([stream_row, o_row, _resident(w_o.shape), _vec_spec(d), _vec_spec(d)] + mods
                + [_resident(w_in.shape), _resident(w_out.shape)])
    y, *casts = pl.pallas_call(
        side.wrap(functools.partial(_ffn_kernel, tiles_per_batch=per_b, n_ctx=0 if lat_only else tok.n_ctx),
                  len(in_specs), 1),
        grid=grid,
        in_specs=in_specs + side.in_specs,
        out_specs=[out_row] + side.out_specs,
        out_shape=[jax.ShapeDtypeStruct((tok.batch * rows_b, d), F32)] + side.out_shapes,
        compiler_params=_cparams("arbitrary"),
        name="out_proj_ffn",
    )(x_all, o_all, w_o.astype(BF16), b_o.reshape(1, d), norm_gain.reshape(1, d), *([mod] * 8), w_in, w_out,
      *side.stacks)
    return y, casts


def kernel(x, c, ctx, c_ctx, w_ada, b_ada, norm_gain, ffn_w_in, ffn_w_out, fnet_w_out, fnet_b_out,
           diff_w_in, diff_q_gain, diff_k_gain, diff_lambda, diff_subln_gain, diff_w_out,
           hgrn_w_in, hgrn_lower_bound, hgrn_norm_gain, hgrn_w_out,
           gqa_w_in, gqa_q_gain, gqa_k_gain, gqa_w_out):
    batch, seq, d = x.shape
    n_ctx = ctx.shape[1]
    depth = w_ada.shape[0]
    assert batch + 1 <= COND_ROWS and d % LANES == 0
    tok = _Tokens(batch, n_ctx, seq)

    cond = jnp.zeros((COND_ROWS, d), F32).at[0].set(c_ctx).at[1:1 + batch].set(c)
    mod = _ada_modulation(cond, w_ada, b_ada)
    x_all = None
    zero_bias = jnp.zeros((d,), F32)
    ffn_stacks = [ffn_w_in, ffn_w_out]

    for i in range(depth):
        m, j = i % N_MIXERS, i // N_MIXERS
        need_ctx = i < depth - 1
        if m == 0:
            if x_all is not None:
                stream = x_all.reshape(batch, tok.t, d)
                ctx, x = stream[:, :n_ctx], stream[:, n_ctx:]
            o, x_all = _fnet_mix(tok, ctx, x, mod, i, norm_gain[i, 0], d // FNET_GROUPS)
            ffn_w = [w[i].astype(BF16) for w in ffn_stacks]
            w_o, b_o = fnet_w_out[j], fnet_b_out[j]
        elif m == 1:
            lam_init = 0.8 - 0.6 * math.exp(-0.3 * i)
            qg = diff_q_gain[j].reshape(1, LANES) * (DIFF_HEAD_DIM ** -0.5 * LOG2_E)
            kg = diff_k_gain[j].reshape(1, LANES)
            q, k, v = _qkv_project(tok, x_all, mod, i, norm_gain[i, 0], proj_w, qg, kg,
                                   DIFF_HEAD_DIM, d, d)
            o, ffn_w = _diff_attention(tok, q, k, v, qg, kg, diff_lambda[j], diff_subln_gain[j], lam_init,
                                       ffn_stacks, i)
            w_o, b_o = diff_w_out[j], zero_bias
        elif m == 2:
            q, lff, lfb, v, g = _hgrn_project(tok, x_all, mod, i, norm_gain[i, 0], proj_w, hgrn_lower_bound)
            o, ffn_w = _hgrn_scan(tok, q, lff, lfb, v, g, hgrn_norm_gain[j], ffn_stacks, i)
            w_o, b_o = hgrn_w_out[j], zero_bias
        else:
            kv = (gqa_w_in.shape[-1] - d) // 2
            qg = gqa_q_gain[j].reshape(1, LANES) * (GQA_HEAD_DIM ** -0.5 * LOG2_E)
            kg = gqa_k_gain[j].reshape(1, LANES)
            q, k, v = _qkv_project(tok, x_all, mod, i, norm_gain[i, 0], proj_w, qg, kg, GQA_HEAD_DIM, d, kv)
            o, ffn_w = _gqa_attention(tok, q, k, v, qg, kg, need_ctx, ffn_stacks, i)
            w_o, b_o = gqa_w_out[j], zero_bias
        next_proj = {1: diff_w_in, 2: hgrn_w_in, 3: gqa_w_in}.get((i + 1) % N_MIXERS) if i + 1 < depth else None
        x_all, proj_cast = _out_proj_ffn(tok, x_all, o, mod, i, w_o, b_o, norm_gain[i, 1], ffn_w[0], ffn_w[1],
                                         not need_ctx, [] if next_proj is None else [next_proj], (i + 1) // N_MIXERS)
        proj_w = proj_cast[0] if proj_cast else None
    return x_all.reshape(batch, seq, d)
```

```python
import functools
import math

import numpy as np
import jax
import jax.numpy as jnp
from jax import lax
from jax.experimental import pallas as pl
from jax.experimental.pallas import tpu as pltpu

F32 = jnp.float32
BF16 = jnp.bfloat16

EPS = 1e-6
GRID_W = 64
ROPE_THETA = 10000.0
N_MIXERS = 4

LANES = 128
SUBLANES = 8
BF16_SUBLANES = 16
TM = 256
COND_ROWS = 16
VMEM_LIMIT = 56 * 1024 * 1024

FNET_GROUPS = 8
DIFF_HEAD_DIM = 64
GQA_HEAD_DIM = 128
GQA_GROUP = 4
HGRN_CHUNK = 64
HGRN_HALF = 32
HGRN_TILE_UNROLL = 9
HGRN_SAFE_LOG2_DECAY = -115.0
FF_CHUNK = 256
FFN_MAX_ROWS = 768
PROJ_MAX_ROWS = 768
HGRN_PROJ_MAX_ROWS = 768
MXU_DIM = 256
LOG2_E = math.log2(math.e)
EXP2_SAFE_LOGIT = 100.0
BF16_SLACK = 1.02
DIFF_ATTN_UNROLL = 8
GQA_ATTN_UNROLL = 8


def _cparams(*sem):
    return pltpu.CompilerParams(dimension_semantics=sem, vmem_limit_bytes=VMEM_LIMIT)


def _resident(shape):
    nd = len(shape)
    return pl.BlockSpec(shape, lambda *_: (0,) * nd, pipeline_mode=pl.Buffered(1))


def _silu(x):
    return x * jax.nn.sigmoid(x)


def _norm_mod(x, gain, shift, scale):
    ms = jnp.mean(x * x, axis=-1, keepdims=True)
    y = x * lax.rsqrt(ms + EPS) * gain
    return y * (1.0 + scale) + shift


def _dot(a, b):
    return jnp.dot(a, b, preferred_element_type=F32)


def _dot_nt(a, b):
    return lax.dot_general(a, b, (((1,), (1,)), ((), ())), preferred_element_type=F32)


def _dot_tn(a, b):
    return lax.dot_general(a, b, (((0,), (0,)), ((), ())), preferred_element_type=F32)


def _ada_kernel(cond_ref, w_ref, b_ref, o_ref):
    a = _silu(cond_ref[...]).astype(BF16)
    o_ref[...] = _dot(a, w_ref[...].astype(BF16)) + b_ref[...]


def _ada_modulation(cond, w_ada, b_ada):
    depth, d, n6 = w_ada.shape
    bn = n6 // 4
    out = pl.pallas_call(
        _ada_kernel,
        grid=(depth, n6 // bn),
        in_specs=[
            pl.BlockSpec((COND_ROWS, d), lambda i, j: (0, 0)),
            pl.BlockSpec((None, d, bn), lambda i, j: (i, 0, j)),
            pl.BlockSpec((None, 1, bn), lambda i, j: (i, 0, j)),
        ],
        out_specs=pl.BlockSpec((None, COND_ROWS, bn), lambda i, j: (i, 0, j)),
        out_shape=jax.ShapeDtypeStruct((depth, COND_ROWS, n6), F32),
        compiler_params=_cparams("parallel", "parallel"),
        name="ada_modulation",
    )(cond, w_ada, b_ada.reshape(depth, 1, n6))
    return out.reshape(depth * COND_ROWS * 6, 1, d)


class _Tokens:
    def __init__(self, batch, n_ctx, seq):
        assert n_ctx % TM == 0 and seq % TM == 0
        self.batch, self.n_ctx, self.seq = batch, n_ctx, seq
        self.t = n_ctx + seq
        self.ctx_tiles = n_ctx // TM
        self.tiles = self.t // TM

    def mod_specs(self, d, layer, chunk, tiles_per_batch):
        ctx = pl.BlockSpec((None, 1, d), lambda i: (layer * COND_ROWS * 6 + chunk, 0, 0))
        lat = pl.BlockSpec((None, 1, d), lambda i: ((layer * COND_ROWS + 1 + i // tiles_per_batch) * 6 + chunk, 0, 0))
        return [ctx, lat]


def _ctx_chunks(rows, tiles_per_batch, n_ctx):
    row0 = (pl.program_id(0) % tiles_per_batch) * rows
    chunks = [(slice(r, r + TM), row0 + r < n_ctx) for r in range(0, rows, TM)]
    if n_ctx == 0:
        return chunks, lambda is_ctx, c_ref, l_ref: l_ref[...]
    return chunks, lambda is_ctx, c_ref, l_ref: jnp.where(is_ctx, c_ref[...], l_ref[...])


def _row_tile(n, cap, step=TM):
    return max(r for r in range(step, cap + 1, step) if n % r == 0)


def _vec_spec(n):
    return pl.BlockSpec((1, n), lambda *_: (0, 0))


class _SideCasts:
    def __init__(self, stacks, layer, grid):
        n0, n1 = (1,) * (2 - len(grid)) + tuple(grid)
        self.stacks, self.in_specs, self.out_specs, self.out_shapes = list(stacks), [], [], []
        for w in self.stacks:
            _, r, c = w.shape
            share = next(s for s in range(1, n1 + 1)
                         if n1 % s == 0 and r % (n0 * n1 // s) == 0 and (r * s // (n0 * n1)) % BF16_SUBLANES == 0)
            rows = r * share // (n0 * n1)
            assert c % LANES == 0
            self.in_specs.append(self._spec((None, rows, c), lambda blk, layer=layer: (layer, blk, 0), n1, share))
            self.out_specs.append(self._spec((rows, c), lambda blk: (blk, 0), n1, share))
            self.out_shapes.append(jax.ShapeDtypeStruct((r, c), BF16))

    @staticmethod
    def _spec(shape, index, n1, share):
        return pl.BlockSpec(shape, lambda *ids: index((sum(ids[:-1]) * n1 + ids[-1]) // share))

    def wrap(self, kernel_fn, n_in, n_out):
        n = len(self.stacks)

        def wrapped(*refs):
            for src, dst in zip(refs[n_in:n_in + n], refs[n_in + n + n_out:n_in + 2 * n + n_out]):
                dst[...] = src[...].astype(BF16)
            kernel_fn(*refs[:n_in], *refs[n_in + n:n_in + n + n_out], *refs[n_in + 2 * n + n_out:])
        return wrapped


def _dft_tables(n_ctx, seq, group_dim):
    def cs(n):
        k = np.arange(n, dtype=np.int64)
        ang = 2.0 * np.pi * ((k[:, None] * k[None, :]) % n).astype(np.float64) / n
        return np.cos(ang), np.sin(ang)
    cc, sc = cs(group_dim)
    chan = np.concatenate([cc, sc], axis=1)
    cx, sx = cs(n_ctx)
    pos_ctx = np.concatenate([cx, -sx], axis=1)
    cl, sl = cs(seq)
    pt = _half_spectrum_tile(seq)
    rows = (np.arange(seq // 2 // pt)[:, None] * pt + np.arange(pt + BF16_SUBLANES)[None, :]) % seq
    rev = np.zeros((pt, pt + BF16_SUBLANES), np.float32)
    rev[np.arange(pt), pt - np.arange(pt)] = 1.0
    as_bf16 = lambda a: jnp.asarray(a.astype(np.float32)).astype(BF16)
    return as_bf16(chan), as_bf16(pos_ctx), as_bf16(cl[rows]), as_bf16(sl[rows]), as_bf16(rev)


def _half_spectrum_tile(seq):
    return min(TM, seq // 2)


def _fnet_kernel(ctx_ref, lat_ref, ng_ref, shc_ref, scc_ref, shl_ref, scl_ref, chan_ref, pctx_ref, cos_ref, sin_ref,
                 rev_ref, y_ref, xs_ref, ab_ctx, ab_lat, *, group_dim):
    j = pl.program_id(1)
    n_ctx, d = ctx_ref.shape
    seq = lat_ref.shape[0]
    ctx_tiles = n_ctx // TM
    pt = _half_spectrum_tile(seq)
    p_tiles = seq // 2 // pt

    @pl.when(j == 0)
    def _():
        for src, sh, sc, dst, n in ((ctx_ref, shc_ref, scc_ref, ab_ctx, n_ctx), (lat_ref, shl_ref, scl_ref, ab_lat, seq)):
            for row0 in range(0, n, TM):
                h = _norm_mod(src[row0:row0 + TM, :], ng_ref[...], sh[...], sc[...]).astype(BF16)
                for g in range(d // group_dim):
                    cols = slice(g * group_dim, (g + 1) * group_dim)
                    ab = _dot(h[:, cols], chan_ref[...])
                    dst[row0:row0 + TM, cols] = ab[:, :group_dim].astype(BF16)
                    dst[n + row0:n + row0 + TM, cols] = ab[:, group_dim:].astype(BF16)

    @pl.when(j < ctx_tiles)
    def _():
        rows = pl.ds(pl.multiple_of(j * TM, TM), TM)
        y = _dot(pctx_ref[rows, :], ab_ctx[...]) * (1.0 / math.sqrt(n_ctx * group_dim))
        y_ref[rows, :] = y.astype(BF16)
        xs_ref[...] = ctx_ref[rows, :]

    @pl.when(j >= ctx_tiles)
    def _():
        xs_ref[...] = lat_ref[pl.ds(pl.multiple_of((j - ctx_tiles) * TM, TM), TM), :]

    first_p_step = pl.num_programs(1) - p_tiles

    @pl.when(j >= first_p_step)
    def _():
        tile = j - first_p_step
        scale = 1.0 / math.sqrt(seq * group_dim)
        p_part = _dot(cos_ref[...], ab_lat[0:seq, :])
        q_part = _dot(sin_ref[...], ab_lat[seq:2 * seq, :])
        y_ref[pl.ds(pl.multiple_of(n_ctx + tile * pt, pt), pt), :] = ((p_part - q_part)[0:pt] * scale).astype(BF16)
        mirrored = _dot(rev_ref[...], ((p_part + q_part) * scale).astype(BF16))
        y_ref[pl.ds(pl.multiple_of(n_ctx + seq - (tile + 1) * pt, pt), pt), :] = mirrored.astype(BF16)


def _fnet_mix(tok, ctx, x, mod, layer, norm_gain, group_dim):
    b, t, d = tok.batch, tok.t, x.shape[-1]
    chan, pos_ctx, cos_lat, sin_lat, rev = _dft_tables(tok.n_ctx, tok.seq, group_dim)
    ct, p_tiles = tok.ctx_tiles, cos_lat.shape[0]
    assert p_tiles < tok.tiles

    def mod_spec(chunk, is_ctx):
        return pl.BlockSpec((None, 1, d), lambda i, j: ((layer * COND_ROWS + (0 if is_ctx else 1 + i)) * 6 + chunk, 0, 0))

    table_spec = pl.BlockSpec((None,) + cos_lat.shape[1:],
                              lambda i, j: (jnp.maximum(j - (tok.tiles - p_tiles), 0), 0, 0))
    y, xs = pl.pallas_call(
        functools.partial(_fnet_kernel, group_dim=group_dim),
        grid=(b, tok.tiles),
        in_specs=[
            pl.BlockSpec((None, tok.n_ctx, d), lambda i, j: (i, 0, 0)),
            pl.BlockSpec((None, tok.seq, d), lambda i, j: (i, 0, 0)),
            pl.BlockSpec((1, d), lambda i, j: (0, 0)),
            mod_spec(0, True), mod_spec(1, True), mod_spec(0, False), mod_spec(1, False),
            _resident(chan.shape), _resident(pos_ctx.shape), table_spec, table_spec, _resident(rev.shape),
        ],
        out_specs=[pl.BlockSpec((None, t, d), lambda i, j: (i, 0, 0)),
                   pl.BlockSpec((TM, d), lambda i, j: (i * tok.tiles + j, 0))],
        out_shape=[jax.ShapeDtypeStruct((b, t, d), BF16), jax.ShapeDtypeStruct((b * t, d), F32)],
        scratch_shapes=[pltpu.VMEM((2 * tok.n_ctx, d), BF16), pltpu.VMEM((2 * tok.seq, d), BF16)],
        compiler_params=_cparams("parallel", "arbitrary"),
        name="fnet_mix",
    )(ctx, x, norm_gain.reshape(1, d), mod, mod, mod, mod, chan, pos_ctx, cos_lat, sin_lat, rev)
    return y.reshape(b * t, d), xs


def _rope_tables(tok, head_dim):
    rows = tok.seq // GRID_W
    row = jnp.repeat(jnp.arange(rows, dtype=F32), GRID_W)
    col = jnp.tile(jnp.arange(GRID_W, dtype=F32), rows)
    n_freq = head_dim // 4
    inv_freq = ROPE_THETA ** (-jnp.arange(n_freq, dtype=F32) / n_freq)
    ang = jnp.concatenate([row[:, None] * inv_freq, col[:, None] * inv_freq], axis=-1)
    cos = jnp.concatenate([jnp.ones((tok.n_ctx, head_dim // 2), F32), jnp.cos(ang)], axis=0)
    sin = jnp.concatenate([jnp.zeros((tok.n_ctx, head_dim // 2), F32), jnp.sin(ang)], axis=0)
    reps = LANES // head_dim
    cos = jnp.tile(jnp.concatenate([cos, cos], axis=-1), (1, reps))
    sin = jnp.tile(jnp.concatenate([-sin, sin], axis=-1), (1, reps))
    return cos, sin


def _group_mean_matrix(group, width):
    g = np.arange(width) // group
    return jnp.asarray((g[:, None] == g[None, :]).astype(np.float32) / group).astype(BF16)


def _rope(yn, cos, sin, head_dim):
    half = head_dim // 2
    if head_dim == LANES:
        partner = pltpu.roll(yn, half, 1)
    else:
        lane = lax.broadcasted_iota(jnp.int32, yn.shape, 1)
        partner = jnp.where(lane % head_dim < half, pltpu.roll(yn, LANES - half, 1), pltpu.roll(yn, half, 1))
    return yn * cos + partner * sin


def _qkv_kernel(x_ref, ng_ref, shc, shl, scc, scl, w_ref, qg_ref, kg_ref, cos_ref, sin_ref, gm_ref,
                q_ref, k_ref, vt_ref, *, head_dim, tiles_per_batch, n_ctx):
    chunks, pick = _ctx_chunks(x_ref.shape[0], tiles_per_batch, n_ctx)
    h = jnp.concatenate([_norm_mod(x_ref[rs, :], ng_ref[...], pick(ic, shc, shl), pick(ic, scc, scl)).astype(BF16)
                         for rs, ic in chunks], axis=0)
    cos, sin, gm = cos_ref[...], sin_ref[...], gm_ref[...]
    nq, nk = q_ref.shape[-1], k_ref.shape[-1]
    for o_ref, g_ref, col0, n in ((q_ref, qg_ref, 0, nq), (k_ref, kg_ref, nq, nk)):
        y = _dot(h, w_ref[:, col0:col0 + n])
        for c0 in range(0, n, MXU_DIM):
            yb = y[:, c0:c0 + MXU_DIM]
            yn = yb * lax.rsqrt(_dot((yb * yb).astype(BF16), gm) + EPS) * g_ref[...]
            for l0 in range(0, MXU_DIM, LANES):
                o_ref[:, c0 + l0:c0 + l0 + LANES] = _rope(yn[:, l0:l0 + LANES], cos, sin, head_dim).astype(BF16)
    vt_ref[...] = _dot(h, w_ref[:, nq + nk:]).astype(BF16).T


def _qkv_project(tok, x_all, mod, layer, norm_gain, w_in, q_gain, k_gain, head_dim, nq, nk):
    n_rows, d = x_all.shape
    nv = w_in.shape[1] - nq - nk
    assert nq % MXU_DIM == 0 and nk % MXU_DIM == 0
    cos, sin = _rope_tables(tok, head_dim)
    tm = _row_tile(tok.t, PROJ_MAX_ROWS)
    per_b = tok.t // tm
    row_spec = lambda n: pl.BlockSpec((tm, n), lambda i: (i, 0))
    tab_spec = pl.BlockSpec((tm, LANES), lambda i: (i % per_b, 0))
    gain2 = lambda g: jnp.tile(g, (1, MXU_DIM // LANES))
    return pl.pallas_call(
        functools.partial(_qkv_kernel, head_dim=head_dim, tiles_per_batch=per_b, n_ctx=tok.n_ctx),
        grid=(n_rows // tm,),
        in_specs=[row_spec(d), _vec_spec(d)] + tok.mod_specs(d, layer, 0, per_b) + tok.mod_specs(d, layer, 1, per_b)
                 + [_resident(w_in.shape), _vec_spec(MXU_DIM), _vec_spec(MXU_DIM), tab_spec, tab_spec,
                    _resident((MXU_DIM, MXU_DIM))],
        out_specs=[row_spec(nq), row_spec(nk), pl.BlockSpec((None, nv, tm), lambda i: (i // per_b, 0, i % per_b))],
        out_shape=[jax.ShapeDtypeStruct((n_rows, nq), BF16), jax.ShapeDtypeStruct((n_rows, nk), BF16),
                   jax.ShapeDtypeStruct((tok.batch, nv, tok.t), BF16)],
        compiler_params=_cparams("parallel"),
        name=f"qkv_project_hd{head_dim}",
    )(x_all, norm_gain.reshape(1, d), mod, mod, mod, mod, w_in, gain2(q_gain), gain2(k_gain),
      cos, sin, _group_mean_matrix(head_dim, MXU_DIM))


def _softmax_parts(q, k):
    s = _dot_nt(q, k)
    p = jnp.exp2(s - jnp.max(s, axis=-1, keepdims=True))
    return p, jnp.sum(p, axis=-1, keepdims=True)


def _softmax_parts_bounded(k, q, vt):
    p = jnp.exp2(_dot_nt(k, q))
    l = jnp.sum(p, axis=0, keepdims=True)
    return _dot(vt, p.astype(BF16)) * (1.0 / l)


def _logit_bound(qg_ref, kg_ref, head_dim):
    return BF16_SLACK * head_dim * jnp.max(jnp.abs(qg_ref[...])) * jnp.max(jnp.abs(kg_ref[...]))


def _diff_attn_kernel(lp_ref, sg_ref, qg_ref, kg_ref, q_ref, k_ref, vt, o_ref, *, lam_init, n_ctx):
    lp = lp_ref[...]
    lam = (jnp.exp(jnp.sum(lp[0:1] * lp[1:2], axis=-1, keepdims=True))
           - jnp.exp(jnp.sum(lp[2:3] * lp[3:4], axis=-1, keepdims=True)) + lam_init)
    t = k_ref.shape[0]
    lane = lax.broadcasted_iota(jnp.int32, (TM, LANES), 1)
    bounded = _logit_bound(qg_ref, kg_ref, DIFF_HEAD_DIM) <= EXP2_SAFE_LOGIT

    def split_q(rows):
        q = q_ref[rows, :]
        zero = jnp.zeros_like(q)
        return jnp.where(lane < DIFF_HEAD_DIM, q, zero), jnp.where(lane >= DIFF_HEAD_DIM, q, zero)

    def tile_bounded(rows, nk):
        o12 = _softmax_parts_bounded(k_ref[0:nk, :], jnp.concatenate(split_q(rows), axis=0), vt[:, 0:nk])
        ot = o12[:, :TM] - lam * o12[:, TM:]
        ms = jnp.mean(ot * ot, axis=0, keepdims=True)
        o_ref[rows, :] = ((ot * lax.rsqrt(ms + EPS)).T * sg_ref[...]).astype(BF16)

    def tile_exact(rows, nk):
        k = k_ref[0:nk, :]
        q1, q2 = split_q(rows)
        p1, l1 = _softmax_parts(q1, k)
        p2, l2 = _softmax_parts(q2, k)
        w = p1 * (1.0 / l1) - p2 * (lam / l2)
        o = _dot(w.astype(BF16), vt[:, 0:nk].T)
        ms = jnp.mean(o * o, axis=-1, keepdims=True)
        o_ref[rows, :] = (o * lax.rsqrt(ms + EPS) * sg_ref[...]).astype(BF16)

    @pl.when(bounded)
    def _():
        _for_query_tiles(tile_bounded, n_ctx, t, 0, DIFF_ATTN_UNROLL)

    @pl.when(jnp.logical_not(bounded))
    def _():
        _for_query_tiles(tile_exact, n_ctx, t, 0, 1)


def _for_query_tiles(tile, n_ctx, t, first_row, unroll):
    for r0 in range(first_row, n_ctx, TM):
        tile(pl.ds(r0, TM), n_ctx)
    lat0 = max(first_row, n_ctx)
    n_lat = (t - lat0) // TM
    unroll = unroll if n_lat % unroll == 0 else 1

    def group(i, _):
        for u in range(unroll):
            tile(pl.ds(pl.multiple_of(lat0 + (i * unroll + u) * TM, TM), TM), t)
        return 0

    lax.fori_loop(0, n_lat // unroll, group, 0)


def _diff_attention(tok, q, k, vt, q_gain, k_gain, lam_par, subln_gain, lam_init, cast_stacks, cast_layer):
    n_rows, d = q.shape
    heads = d // LANES
    t = tok.t
    grid = (tok.batch, heads)
    side = _SideCasts(cast_stacks, cast_layer, grid)
    as3 = lambda a: a.reshape(tok.batch, t, d)
    spec = pl.BlockSpec((None, t, LANES), lambda b, h: (b, 0, h))
    vt_spec = pl.BlockSpec((None, LANES, t), lambda b, h: (b, h, 0))
    vec = pl.BlockSpec((1, LANES), lambda b, h: (0, 0))
    in_specs = [pl.BlockSpec(lam_par.shape, lambda b, h: (0, 0)), vec, vec, vec, spec, spec, vt_spec]
    out, *casts = pl.pallas_call(
        side.wrap(functools.partial(_diff_attn_kernel, lam_init=lam_init, n_ctx=tok.n_ctx), len(in_specs), 1),
        grid=grid,
        in_specs=in_specs + side.in_specs,
        out_specs=[spec] + side.out_specs,
        out_shape=[jax.ShapeDtypeStruct((tok.batch, t, d), BF16)] + side.out_shapes,
        compiler_params=_cparams("parallel", "arbitrary"),
        name="diff_attention",
    )(lam_par, (subln_gain * (1.0 - lam_init)).reshape(1, LANES), q_gain, k_gain, as3(q), as3(k), vt, *side.stacks)
    return out.reshape(n_rows, d), casts


def _gqa_attn_kernel(qg_ref, kg_ref, q_ref, k_ref, vt, o_ref, *, n_ctx, first_row):
    t = k_ref.shape[0]
    bounded = _logit_bound(qg_ref, kg_ref, GQA_HEAD_DIM) <= EXP2_SAFE_LOGIT

    def out_rows(rows):
        if first_row == 0:
            return rows
        start = rows.start - first_row
        return pl.ds(start if isinstance(start, int) else pl.multiple_of(start, TM), TM)

    def tile_bounded(rows, nk):
        k, vt_k = k_ref[0:nk, :], vt[:, 0:nk]
        for g in range(0, GQA_GROUP, 2):
            q2h = jnp.concatenate([q_ref[rows, g * LANES:(g + 1) * LANES],
                                   q_ref[rows, (g + 1) * LANES:(g + 2) * LANES]], axis=0)
            o2h = _softmax_parts_bounded(k, q2h, vt_k)
            o_ref[out_rows(rows), g * LANES:(g + 1) * LANES] = o2h[:, :TM].T.astype(BF16)
            o_ref[out_rows(rows), (g + 1) * LANES:(g + 2) * LANES] = o2h[:, TM:].T.astype(BF16)

    def tile_exact(rows, nk):
        k, v = k_ref[0:nk, :], vt[:, 0:nk].T
        for g in range(GQA_GROUP):
            cols = slice(g * LANES, (g + 1) * LANES)
            p, l = _softmax_parts(q_ref[rows, cols], k)
            o_ref[out_rows(rows), cols] = (_dot(p.astype(BF16), v) * (1.0 / l)).astype(BF16)

    @pl.when(bounded)
    def _():
        _for_query_tiles(tile_bounded, n_ctx, t, first_row, GQA_ATTN_UNROLL)

    @pl.when(jnp.logical_not(bounded))
    def _():
        _for_query_tiles(tile_exact, n_ctx, t, first_row, 1)


def _gqa_attention(tok, q, k, vt, q_gain, k_gain, need_ctx, cast_stacks, cast_layer):
    d = q.shape[1]
    kv_heads = k.shape[1] // LANES
    t = tok.t
    grid = (tok.batch, kv_heads)
    side = _SideCasts(cast_stacks, cast_layer, grid)
    first_row = 0 if need_ctx else tok.n_ctx
    as3 = lambda a: a.reshape(tok.batch, t, a.shape[1])
    k_spec = pl.BlockSpec((None, t, LANES), lambda b, h: (b, 0, h))
    vt_spec = pl.BlockSpec((None, LANES, t), lambda b, h: (b, h, 0))
    q_spec = pl.BlockSpec((None, t, GQA_GROUP * LANES), lambda b, h: (b, 0, h))
    o_spec = pl.BlockSpec((None, t - first_row, GQA_GROUP * LANES), lambda b, h: (b, 0, h))
    vec = pl.BlockSpec((1, LANES), lambda b, h: (0, 0))
    in_specs = [vec, vec, q_spec, k_spec, vt_spec]
    out, *casts = pl.pallas_call(
        side.wrap(functools.partial(_gqa_attn_kernel, n_ctx=tok.n_ctx, first_row=first_row), len(in_specs), 1),
        grid=grid,
        in_specs=in_specs + side.in_specs,
        out_specs=[o_spec] + side.out_specs,
        out_shape=[jax.ShapeDtypeStruct((tok.batch, t - first_row, d), BF16)] + side.out_shapes,
        compiler_params=_cparams("parallel", "arbitrary"),
        name="gqa_attention",
    )(q_gain, k_gain, as3(q), as3(k), vt, *side.stacks)
    return out.reshape(tok.batch * (t - first_row), d), casts


def _hgrn_proj_kernel(x_ref, ng_ref, shc, shl, scc, scl, w_ref, lb_ref, q_ref, lff_ref, lfb_ref, v_ref, g_ref,
                      *, layer_idx, depth, tiles_per_batch, n_ctx):
    d = x_ref.shape[-1]
    chunks, pick = _ctx_chunks(x_ref.shape[0], tiles_per_batch, n_ctx)
    lbs = []
    for direction in range(2):
        rows = [lb_ref[direction * depth + i:direction * depth + i + 1, :] for i in range(depth)]
        m = functools.reduce(jnp.maximum, rows)
        e = [jnp.exp(r - m) for r in rows]
        lbs.append(sum(e[1:layer_idx + 1]) / sum(e) if layer_idx > 0 else jnp.zeros_like(m))

    def normed(rs, ic):
        return _norm_mod(x_ref[rs, :], ng_ref[...], pick(ic, shc, shl), pick(ic, scc, scl)).astype(BF16)

    def project(rs, h):
        q_ref[rs, :] = _silu(_dot(h, w_ref[:, 0:d]))
        for direction, o_ref in enumerate((lff_ref, lfb_ref)):
            z = _dot(h, w_ref[:, (1 + direction) * d:(2 + direction) * d])
            lb = lbs[direction]
            o_ref[rs, :] = jnp.log2(lb + (1.0 - lb) * jax.nn.sigmoid(z))
        v_ref[rs, :] = _dot(h, w_ref[:, 3 * d:4 * d]).astype(BF16)
        g_ref[rs, :] = _dot(h, w_ref[:, 4 * d:5 * d])

    h = normed(*chunks[0])
    for k, (rs, _) in enumerate(chunks):
        h_next = normed(*chunks[k + 1]) if k + 1 < len(chunks) else None
        project(rs, h)
        h = h_next


def _hgrn_project(tok, x_all, mod, layer, norm_gain, w_in, lower_bound):
    n_rows, d = x_all.shape
    depth = lower_bound.shape[1]
    tm = _row_tile(tok.t, HGRN_PROJ_MAX_ROWS)
    per_b = tok.t // tm
    row_spec = pl.BlockSpec((tm, d), lambda i: (i, 0))
    return pl.pallas_call(
        functools.partial(_hgrn_proj_kernel, layer_idx=layer, depth=depth, tiles_per_batch=per_b, n_ctx=tok.n_ctx),
        grid=(n_rows // tm,),
        in_specs=[row_spec, _vec_spec(d)] + tok.mod_specs(d, layer, 0, per_b) + tok.mod_specs(d, layer, 1, per_b)
                 + [_resident(w_in.shape), pl.BlockSpec((2 * depth, d), lambda i: (0, 0))],
        out_specs=[row_spec] * 5,
        out_shape=[jax.ShapeDtypeStruct((n_rows, d), dt) for dt in (F32, F32, F32, BF16, F32)],
        compiler_params=_cparams("parallel"),
        name="hgrn_project",
    )(x_all, norm_gain.reshape(1, d), mod, mod, mod, mod, w_in, lower_bound.reshape(2 * depth, d))


def _split2(x):
    hi = x.astype(BF16)
    return hi, (x - hi.astype(F32)).astype(BF16)


def _hgrn_scan_kernel(q_ref, lff_ref, lfb_ref, v_ref, g_ref, ng_ref, o_ref,
                      acc, qs, xs, st, ds, cum, kk, *, n_ctx):
    t, dk = q_ref.shape
    c, hh = HGRN_CHUNK, HGRN_HALF
    n_chunks, ctx_chunks = t // c, n_ctx // c
    cpt = TM // c
    row = lax.broadcasted_iota(jnp.int32, (TM, TM), 0)
    col = lax.broadcasted_iota(jnp.int32, (TM, TM), 1)
    same_chunk = (row // c) == (col // c)
    same_half = (row // hh) == (col // hh)
    cross_half = same_chunk & jnp.logical_not(same_half)
    cum_mat = jnp.where(same_chunk & (col <= row), 1.0, 0.0).astype(BF16)
    row_c = lax.broadcasted_iota(jnp.int32, (c, c), 0)
    col_c = lax.broadcasted_iota(jnp.int32, (c, c), 1)

    as4 = lambda a: a.reshape(cpt, c, dk)
    as8 = lambda a: a.reshape(2 * cpt, hh, dk)
    flat = lambda a: a.reshape(TM, dk)
    half_idx = lax.broadcasted_iota(jnp.int32, (2 * cpt, 1, 1), 0)
    per_half = lambda a: jnp.concatenate([a[i // 2:i // 2 + 1] for i in range(2 * cpt)], axis=0)
    n_tiles = t // TM
    unroll = next(u for u in (HGRN_TILE_UNROLL, 3, 2, 1) if n_tiles % u == 0)

    def cumulative(ti):
        rows = pl.ds(pl.multiple_of(ti * TM, TM), TM)
        lf_f, lf_b = lff_ref[rows, :], lfb_ref[rows, :]
        hi, lo = _split2(jnp.concatenate([lf_f, lf_b], axis=1))
        pre = _dot(cum_mat, hi) + _dot(cum_mat, lo)
        pre_b = as4(pre[:, dk:])
        bc_f = as4(pre[:, :dk])
        bc_b = pre_b[:, c - 1:c, :] - pre_b + as4(lf_b)
        return rows, (as4(lf_f), as4(lf_b)), (bc_f, bc_b)

    def min_half_total(bcs):
        bc_f, bc_b = bcs
        top_f, bot_b = bc_f[:, hh - 1:hh, :], bc_b[:, hh:hh + 1, :]
        halves = jnp.minimum(jnp.minimum(top_f, bc_f[:, c - 1:c, :] - top_f), jnp.minimum(bot_b, bc_b[:, 0:1, :] - bot_b))
        return jnp.min(halves, axis=0)

    def tile_operands(cumulated):
        rows, lfs, bcs = cumulated
        q8, v = as8(q_ref[rows, :]), v_ref[rows, :]
        q_d2, k_d2, q_s, k_bar, decay, q_o, k_o = [], [], [], [], [], [], []
        for lf4, bc4, reverse in zip(lfs, bcs, (False, True)):
            edge, last = (hh, 0) if reverse else (hh - 1, c - 1)
            second = (half_idx % 2 == 0) if reverse else (half_idx % 2 == 1)
            b_last, b_edge = bc4[:, last:last + 1, :], bc4[:, edge:edge + 1, :]
            b_last8, b_edge8 = per_half(b_last), per_half(b_edge)
            bc8, k8 = as8(bc4), 1.0 - jnp.exp2(as8(lf4))
            ref = bc8[:, hh // 2:hh // 2 + 1, :]
            rel = bc8 - ref
            q_d, k_d = q8 * jnp.exp2(rel), k8 * jnp.exp2(-rel)
            q_d2.append(flat(q_d).astype(BF16))
            k_d2.append(flat(k_d).astype(BF16))
            q_s.append(flat(q_d * jnp.exp2(ref)))
            k_bar.append(flat(k_d * jnp.exp2(b_last8 - ref)))
            decay.append(jnp.exp2(b_last))
            q_o.append(flat(q_d * jnp.where(second, jnp.exp2(ref - b_edge8), 0.0)))
            k_o.append(flat(k_d * jnp.where(second, 0.0, jnp.exp2(b_edge8 - ref))))
        stack = lambda parts: jnp.concatenate(parts, axis=1).astype(BF16)
        return rows, v, q_d2, k_d2, stack(q_o), stack(k_o), stack(q_s), stack(k_bar), jnp.concatenate(decay, axis=2)

    def tile_matmuls(ti, operands):
        rows, v, q_d2, k_d2, q_o2, k_o2, q_s2, k_bar2, decay2 = operands
        a_same = (jnp.where(col <= row, _dot_nt(q_d2[0], k_d2[0]), 0.0)
                  + jnp.where(col >= row, _dot_nt(q_d2[1], k_d2[1]), 0.0))
        a = jnp.where(same_half, a_same, jnp.where(cross_half, _dot_nt(q_o2, k_o2), 0.0))
        acc[rows, :] = _dot(a.astype(BF16), v)
        store_state_inputs(ti, rows, v, q_s2, k_bar2, decay2)

    def store_state_inputs(ti, rows, v, q_s2, k_bar2, decay2):
        qs[rows, :] = q_s2
        for cc in range(cpt):
            ci = ti * cpt + cc
            xs[ci] = _dot_tn(v[cc * c:(cc + 1) * c, :], k_bar2[cc * c:(cc + 1) * c, :])
            ds[ci] = jnp.broadcast_to(decay2[cc], ds.shape[1:])

    def tile_exact(ti, _):
        rows, lfs, bcs = cumulative(ti)
        q4, v = as4(q_ref[rows, :]), v_ref[rows, :]
        q_s, k_bar, decay = [], [], []
        for lf4, bc4, reverse in zip(lfs, bcs, (False, True)):
            b_last = bc4[:, 0:1, :] if reverse else bc4[:, c - 1:c, :]
            q_s.append(flat(q4 * jnp.exp2(bc4)))
            k_bar.append(flat((1.0 - jnp.exp2(lf4)) * jnp.exp2(b_last - bc4)))
            decay.append(jnp.exp2(b_last))
        stack = lambda parts: jnp.concatenate(parts, axis=1).astype(BF16)
        store_state_inputs(ti, rows, v, stack(q_s), stack(k_bar), jnp.concatenate(decay, axis=2))
        for cc in range(cpt):
            a = jnp.zeros((c, c), F32)
            q_c = q4[cc]
            for lf4, bc4, reverse in zip(lfs, bcs, (False, True)):
                bc_c = bc4[cc]
                cum[...] = bc_c
                kk[...] = 1.0 - jnp.exp2(lf4[cc])

                def col_step(s, a_):
                    d_s = jnp.exp2(jnp.minimum(bc_c - cum[pl.ds(s, 1), :], 0.0))
                    w = jnp.sum(q_c * kk[pl.ds(s, 1), :] * d_s, axis=-1, keepdims=True)
                    return jnp.where(col_c == s, w, a_)
                a_dir = lax.fori_loop(0, c, col_step, jnp.zeros((c, c), F32))
                a = a + jnp.where((col_c >= row_c) if reverse else (col_c <= row_c), a_dir, 0.0)
            crow = pl.ds(pl.multiple_of(ti * TM + cc * c, c), c)
            acc[crow, :] = _dot(a.astype(BF16), v[cc * c:(cc + 1) * c, :])
        return 0

    def tile_group(i, lowest):
        tiles = [i * unroll + u for u in range(unroll)]
        cums = {0: cumulative(tiles[0])}
        if unroll > 1:
            cums[1] = cumulative(tiles[1])
        ops = {0: tile_operands(cums[0])}
        for u in range(unroll):
            if u + 2 < unroll:
                cums[u + 2] = cumulative(tiles[u + 2])
            if u + 1 < unroll:
                ops[u + 1] = tile_operands(cums[u + 1])
            tile_matmuls(tiles[u], ops[u])
        return functools.reduce(jnp.minimum, [min_half_total(cm[2]) for cm in cums.values()], lowest)

    lowest = lax.fori_loop(0, n_tiles // unroll, tile_group, jnp.zeros((1, dk), F32))
    unsafe = jnp.logical_not(jnp.min(lowest) >= HGRN_SAFE_LOG2_DECAY)

    @pl.when(unsafe)
    def _():
        lax.fori_loop(0, n_tiles, tile_exact, 0)

    def state_step(i, carry):
        s_f, s_b = carry
        cf = i
        cb = jnp.where(i < ctx_chunks, ctx_chunks - 1 - i, n_chunks - 1 + ctx_chunks - i)
        st[cf, :, 0:dk] = s_f.astype(BF16)
        st[cb, :, dk:2 * dk] = s_b.astype(BF16)
        s_f = s_f * ds[cf, 0:1, 0:dk] + xs[cf, :, 0:dk]
        s_b = s_b * ds[cb, 0:1, dk:2 * dk] + xs[cb, :, dk:2 * dk]
        return s_f, s_b

    zero = jnp.zeros((dk, dk), F32)
    lax.fori_loop(0, n_chunks, state_step, (zero, zero))

    def finish_step(ti):
        rows = pl.ds(pl.multiple_of(ti * TM, TM), TM)
        inter = [_dot_nt(qs[pl.ds(pl.multiple_of(ti * TM + cc * c, c), c), :], st[ti * cpt + cc]) for cc in range(cpt)]
        o = acc[rows, :] + jnp.concatenate(inter, axis=0)
        ms = jnp.mean(o * o, axis=-1, keepdims=True)
        o_ref[rows, :] = (o * lax.rsqrt(ms + EPS) * ng_ref[...] * _silu(g_ref[rows, :])).astype(BF16)

    def finish_group(i, _):
        for u in range(unroll):
            finish_step(i * unroll + u)
        return 0

    lax.fori_loop(0, n_tiles // unroll, finish_group, 0)


def _hgrn_scan(tok, q, lff, lfb, v, g, norm_gain, cast_stacks, cast_layer):
    n_rows, d = q.shape
    heads = d // LANES
    t = tok.t
    n_chunks = t // HGRN_CHUNK
    grid = (tok.batch, heads)
    side = _SideCasts(cast_stacks, cast_layer, grid)
    spec = pl.BlockSpec((None, t, LANES), lambda b, h: (b, 0, h))
    as3 = lambda a: a.reshape(tok.batch, t, d)
    in_specs = [spec] * 5 + [pl.BlockSpec((1, LANES), lambda b, h: (0, 0))]
    out, *casts = pl.pallas_call(
        side.wrap(functools.partial(_hgrn_scan_kernel, n_ctx=tok.n_ctx), len(in_specs), 1),
        grid=grid,
        in_specs=in_specs + side.in_specs,
        out_specs=[spec] + side.out_specs,
        out_shape=[jax.ShapeDtypeStruct((tok.batch, t, d), BF16)] + side.out_shapes,
        scratch_shapes=[pltpu.VMEM((t, LANES), F32), pltpu.VMEM((t, 2 * LANES), BF16),
                        pltpu.VMEM((n_chunks, LANES, 2 * LANES), F32), pltpu.VMEM((n_chunks, LANES, 2 * LANES), BF16),
                        pltpu.VMEM((n_chunks, SUBLANES, 2 * LANES), F32),
                        pltpu.VMEM((HGRN_CHUNK, LANES), F32), pltpu.VMEM((HGRN_CHUNK, LANES), F32)],
        compiler_params=_cparams("parallel", "arbitrary"),
        name="hgrn_scan",
    )(as3(q), as3(lff), as3(lfb), as3(v), as3(g), norm_gain.reshape(1, LANES), *side.stacks)
    return out.reshape(n_rows, d), casts


def _ffn_kernel(x_ref, o_ref, wo_ref, bo_ref, ng_ref, g1c, g1l, shc, shl, scc, scl, g2c, g2l, wi_ref, wf_ref, y_ref,
                *, tiles_per_batch, n_ctx):
    chunks, pick = _ctx_chunks(x_ref.shape[0], tiles_per_batch, n_ctx)
    x1 = [x_ref[rs, :] + pick(ic, g1c, g1l) * (_dot(o_ref[rs, :], wo_ref[...]) + bo_ref[...]) for rs, ic in chunks]
    h = jnp.concatenate([_norm_mod(x1_c, ng_ref[...], pick(ic, shc, shl), pick(ic, scc, scl)).astype(BF16)
                         for x1_c, (_, ic) in zip(x1, chunks)], axis=0)
    d_ff = wf_ref.shape[0]
    act = []
    for c0 in range(0, d_ff, FF_CHUNK):
        gate = _dot(h, wi_ref[:, c0:c0 + FF_CHUNK])
        up = _dot(h, wi_ref[:, d_ff + c0:d_ff + c0 + FF_CHUNK])
        act.append((_silu(gate) * up).astype(BF16))
    ff = _dot(jnp.concatenate(act, axis=1), wf_ref[...])
    for x1_c, (rs, ic) in zip(x1, chunks):
        y_ref[rs, :] = x1_c + pick(ic, g2c, g2l) * ff[rs, :]


def _out_proj_ffn(tok, x_all, o_all, mod, layer, w_o, b_o, norm_gain, w_in, w_out, lat_only, cast_stacks, cast_layer):
    n_rows, d = x_all.shape
    d_ff = w_out.shape[0]
    assert d_ff % FF_CHUNK == 0
    rows_b = tok.seq if lat_only else tok.t
    tm = _row_tile(math.gcd(rows_b, tok.n_ctx) if lat_only else rows_b, FFN_MAX_ROWS)
    per_b = rows_b // tm
    first = tok.n_ctx // tm if lat_only else 0
    stream_row = pl.BlockSpec((tm, d), lambda i: ((i // per_b) * (tok.t // tm) + first + i % per_b, 0))
    out_row = pl.BlockSpec((tm, d), lambda i: (i, 0))
    o_row = stream_row if o_all.shape[0] == n_rows else out_row
    mods = [s for chunk in (2, 3, 4, 5) for s in tok.mod_specs(d, layer, chunk, per_b)]
    grid = (tok.batch * per_b,)
    side = _SideCasts(cast_stacks, cast_layer, grid)
    in_specs = ([stream_row, o_row, _resident(w_o.shape), _vec_spec(d), _vec_spec(d)] + mods
                + [_resident(w_in.shape), _resident(w_out.shape)])
    y, *casts = pl.pallas_call(
        side.wrap(functools.partial(_ffn_kernel, tiles_per_batch=per_b, n_ctx=0 if lat_only else tok.n_ctx),
                  len(in_specs), 1),
        grid=grid,
        in_specs=in_specs + side.in_specs,
        out_specs=[out_row] + side.out_specs,
        out_shape=[jax.ShapeDtypeStruct((tok.batch * rows_b, d), F32)] + side.out_shapes,
        compiler_params=_cparams("arbitrary"),
        name="out_proj_ffn",
    )(x_all, o_all, w_o.astype(BF16), b_o.reshape(1, d), norm_gain.reshape(1, d), *([mod] * 8), w_in, w_out,
      *side.stacks)
    return y, casts


def kernel(x, c, ctx, c_ctx, w_ada, b_ada, norm_gain, ffn_w_in, ffn_w_out, fnet_w_out, fnet_b_out,
           diff_w_in, diff_q_gain, diff_k_gain, diff_lambda, diff_subln_gain, diff_w_out,
           hgrn_w_in, hgrn_lower_bound, hgrn_norm_gain, hgrn_w_out,
           gqa_w_in, gqa_q_gain, gqa_k_gain, gqa_w_out):
    batch, seq, d = x.shape
    n_ctx = ctx.shape[1]
    depth = w_ada.shape[0]
    assert batch + 1 <= COND_ROWS and d % LANES == 0
    tok = _Tokens(batch, n_ctx, seq)

    cond = jnp.zeros((COND_ROWS, d), F32).at[0].set(c_ctx).at[1:1 + batch].set(c)
    mod = _ada_modulation(cond, w_ada, b_ada)
    x_all = None
    zero_bias = jnp.zeros((d,), F32)
    ffn_stacks = [ffn_w_in, ffn_w_out]

    for i in range(depth):
        m, j = i % N_MIXERS, i // N_MIXERS
        need_ctx = i < depth - 1
        if m == 0:
            if x_all is not None:
                stream = x_all.reshape(batch, tok.t, d)
                ctx, x = stream[:, :n_ctx], stream[:, n_ctx:]
            o, x_all = _fnet_mix(tok, ctx, x, mod, i, norm_gain[i, 0], d // FNET_GROUPS)
            ffn_w = [w[i].astype(BF16) for w in ffn_stacks]
            w_o, b_o = fnet_w_out[j], fnet_b_out[j]
        elif m == 1:
            lam_init = 0.8 - 0.6 * math.exp(-0.3 * i)
            qg = diff_q_gain[j].reshape(1, LANES) * (DIFF_HEAD_DIM ** -0.5 * LOG2_E)
            kg = diff_k_gain[j].reshape(1, LANES)
            q, k, v = _qkv_project(tok, x_all, mod, i, norm_gain[i, 0], proj_w, qg, kg,
                                   DIFF_HEAD_DIM, d, d)
            o, ffn_w = _diff_attention(tok, q, k, v, qg, kg, diff_lambda[j], diff_subln_gain[j], lam_init,
                                       ffn_stacks, i)
            w_o, b_o = diff_w_out[j], zero_bias
        elif m == 2:
            q, lff, lfb, v, g = _hgrn_project(tok, x_all, mod, i, norm_gain[i, 0], proj_w, hgrn_lower_bound)
            o, ffn_w = _hgrn_scan(tok, q, lff, lfb, v, g, hgrn_norm_gain[j], ffn_stacks, i)
            w_o, b_o = hgrn_w_out[j], zero_bias
        else:
            kv = (gqa_w_in.shape[-1] - d) // 2
            qg = gqa_q_gain[j].reshape(1, LANES) * (GQA_HEAD_DIM ** -0.5 * LOG2_E)
            kg = gqa_k_gain[j].reshape(1, LANES)
            q, k, v = _qkv_project(tok, x_all, mod, i, norm_gain[i, 0], proj_w, qg, kg, GQA_HEAD_DIM, d, kv)
            o, ffn_w = _gqa_attention(tok, q, k, v, qg, kg, need_ctx, ffn_stacks, i)
            w_o, b_o = gqa_w_out[j], zero_bias
        next_proj = {1: diff_w_in, 2: hgrn_w_in, 3: gqa_w_in}.get((i + 1) % N_MIXERS) if i + 1 < depth else None
        x_all, proj_cast = _out_proj_ffn(tok, x_all, o, mod, i, w_o, b_o, norm_gain[i, 1], ffn_w[0], ffn_w[1],
                                         not need_ctx, [] if next_proj is None else [next_proj], (i + 1) // N_MIXERS)
        proj_w = proj_cast[0] if proj_cast else None
    return x_all.reshape(batch, seq, d)
```

```python
import functools
import math

import numpy as np
import jax
import jax.numpy as jnp
from jax import lax
from jax.experimental import pallas as pl
from jax.experimental.pallas import tpu as pltpu

F32 = jnp.float32
BF16 = jnp.bfloat16

EPS = 1e-6
GRID_W = 64
ROPE_THETA = 10000.0
N_MIXERS = 4

LANES = 128
SUBLANES = 8
BF16_SUBLANES = 16
TM = 256
COND_ROWS = 16
VMEM_LIMIT = 56 * 1024 * 1024

FNET_GROUPS = 8
DIFF_HEAD_DIM = 64
GQA_HEAD_DIM = 128
GQA_GROUP = 4
HGRN_CHUNK = 64
HGRN_HALF = 32
HGRN_TILE_UNROLL = 9
HGRN_SAFE_LOG2_DECAY = -115.0
FF_CHUNK = 256
FFN_MAX_ROWS = 768
FFN_LAT_MAX_ROWS = 512
PROJ_MAX_ROWS = 768
HGRN_PROJ_MAX_ROWS = 768
MXU_DIM = 256
LOG2_E = math.log2(math.e)
EXP2_SAFE_LOGIT = 100.0
BF16_SLACK = 1.02
DIFF_ATTN_UNROLL = 8
GQA_ATTN_UNROLL = 8


def _cparams(*sem):
    return pltpu.CompilerParams(dimension_semantics=sem, vmem_limit_bytes=VMEM_LIMIT)


def _resident(shape):
    nd = len(shape)
    return pl.BlockSpec(shape, lambda *_: (0,) * nd, pipeline_mode=pl.Buffered(1))


def _silu(x):
    return x * jax.nn.sigmoid(x)


def _norm_mod(x, gain, shift, scale):
    ms = jnp.mean(x * x, axis=-1, keepdims=True)
    y = x * lax.rsqrt(ms + EPS) * gain
    return y * (1.0 + scale) + shift


def _dot(a, b):
    return jnp.dot(a, b, preferred_element_type=F32)


def _dot_nt(a, b):
    return lax.dot_general(a, b, (((1,), (1,)), ((), ())), preferred_element_type=F32)


def _dot_tn(a, b):
    return lax.dot_general(a, b, (((0,), (0,)), ((), ())), preferred_element_type=F32)


def _ada_kernel(cond_ref, w_ref, b_ref, o_ref):
    a = _silu(cond_ref[...]).astype(BF16)
    o_ref[...] = _dot(a, w_ref[...].astype(BF16)) + b_ref[...]


def _ada_modulation(cond, w_ada, b_ada):
    depth, d, n6 = w_ada.shape
    bn = n6 // 4
    out = pl.pallas_call(
        _ada_kernel,
        grid=(depth, n6 // bn),
        in_specs=[
            pl.BlockSpec((COND_ROWS, d), lambda i, j: (0, 0)),
            pl.BlockSpec((None, d, bn), lambda i, j: (i, 0, j)),
            pl.BlockSpec((None, 1, bn), lambda i, j: (i, 0, j)),
        ],
        out_specs=pl.BlockSpec((None, COND_ROWS, bn), lambda i, j: (i, 0, j)),
        out_shape=jax.ShapeDtypeStruct((depth, COND_ROWS, n6), F32),
        compiler_params=_cparams("parallel", "parallel"),
        name="ada_modulation",
    )(cond, w_ada, b_ada.reshape(depth, 1, n6))
    return out.reshape(depth * COND_ROWS * 6, 1, d)


class _Tokens:
    def __init__(self, batch, n_ctx, seq):
        assert n_ctx % TM == 0 and seq % TM == 0
        self.batch, self.n_ctx, self.seq = batch, n_ctx, seq
        self.t = n_ctx + seq
        self.ctx_tiles = n_ctx // TM
        self.tiles = self.t // TM

    def mod_specs(self, d, layer, chunk, tiles_per_batch):
        ctx = pl.BlockSpec((None, 1, d), lambda i: (layer * COND_ROWS * 6 + chunk, 0, 0))
        lat = pl.BlockSpec((None, 1, d), lambda i: ((layer * COND_ROWS + 1 + i // tiles_per_batch) * 6 + chunk, 0, 0))
        return [ctx, lat]


def _ctx_chunks(rows, tiles_per_batch, n_ctx):
    row0 = (pl.program_id(0) % tiles_per_batch) * rows
    chunks = [(slice(r, r + TM), row0 + r < n_ctx) for r in range(0, rows, TM)]
    if n_ctx == 0:
        return chunks, lambda is_ctx, c_ref, l_ref: l_ref[...]
    return chunks, lambda is_ctx, c_ref, l_ref: jnp.where(is_ctx, c_ref[...], l_ref[...])


def _row_tile(n, cap, step=TM):
    return max(r for r in range(step, cap + 1, step) if n % r == 0)


def _vec_spec(n):
    return pl.BlockSpec((1, n), lambda *_: (0, 0))


class _SideCasts:
    def __init__(self, stacks, layer, grid):
        n0, n1 = (1,) * (2 - len(grid)) + tuple(grid)
        self.stacks, self.in_specs, self.out_specs, self.out_shapes = list(stacks), [], [], []
        for w in self.stacks:
            _, r, c = w.shape
            share = next(s for s in range(1, n1 + 1)
                         if n1 % s == 0 and r % (n0 * n1 // s) == 0 and (r * s // (n0 * n1)) % BF16_SUBLANES == 0)
            rows = r * share // (n0 * n1)
            assert c % LANES == 0
            self.in_specs.append(self._spec((None, rows, c), lambda blk, layer=layer: (layer, blk, 0), n1, share))
            self.out_specs.append(self._spec((rows, c), lambda blk: (blk, 0), n1, share))
            self.out_shapes.append(jax.ShapeDtypeStruct((r, c), BF16))

    @staticmethod
    def _spec(shape, index, n1, share):
        return pl.BlockSpec(shape, lambda *ids: index((sum(ids[:-1]) * n1 + ids[-1]) // share))

    def wrap(self, kernel_fn, n_in, n_out):
        n = len(self.stacks)

        def wrapped(*refs):
            for src, dst in zip(refs[n_in:n_in + n], refs[n_in + n + n_out:n_in + 2 * n + n_out]):
                dst[...] = src[...].astype(BF16)
            kernel_fn(*refs[:n_in], *refs[n_in + n:n_in + n + n_out], *refs[n_in + 2 * n + n_out:])
        return wrapped


def _dft_tables(n_ctx, seq, group_dim):
    def cs(n):
        k = np.arange(n, dtype=np.int64)
        ang = 2.0 * np.pi * ((k[:, None] * k[None, :]) % n).astype(np.float64) / n
        return np.cos(ang), np.sin(ang)
    cc, sc = cs(group_dim)
    chan = np.concatenate([cc, sc], axis=1)
    cx, sx = cs(n_ctx)
    pos_ctx = np.concatenate([cx, -sx], axis=1)
    cl, sl = cs(seq)
    pt = _half_spectrum_tile(seq)
    rows = (np.arange(seq // 2 // pt)[:, None] * pt + np.arange(pt + BF16_SUBLANES)[None, :]) % seq
    rev = np.zeros((pt, pt + BF16_SUBLANES), np.float32)
    rev[np.arange(pt), pt - np.arange(pt)] = 1.0
    as_bf16 = lambda a: jnp.asarray(a.astype(np.float32)).astype(BF16)
    return as_bf16(chan), as_bf16(pos_ctx), as_bf16(cl[rows]), as_bf16(sl[rows]), as_bf16(rev)


def _half_spectrum_tile(seq):
    return min(TM, seq // 2)


def _fnet_kernel(ctx_ref, lat_ref, ng_ref, shc_ref, scc_ref, shl_ref, scl_ref, chan_ref, pctx_ref, cos_ref, sin_ref,
                 rev_ref, y_ref, xs_ref, ab_ctx, ab_lat, *, group_dim):
    j = pl.program_id(1)
    n_ctx, d = ctx_ref.shape
    seq = lat_ref.shape[0]
    ctx_tiles = n_ctx // TM
    pt = _half_spectrum_tile(seq)
    p_tiles = seq // 2 // pt

    @pl.when(j == 0)
    def _():
        for src, sh, sc, dst, n in ((ctx_ref, shc_ref, scc_ref, ab_ctx, n_ctx), (lat_ref, shl_ref, scl_ref, ab_lat, seq)):
            for row0 in range(0, n, TM):
                h = _norm_mod(src[row0:row0 + TM, :], ng_ref[...], sh[...], sc[...]).astype(BF16)
                for g in range(d // group_dim):
                    cols = slice(g * group_dim, (g + 1) * group_dim)
                    ab = _dot(h[:, cols], chan_ref[...])
                    dst[row0:row0 + TM, cols] = ab[:, :group_dim].astype(BF16)
                    dst[n + row0:n + row0 + TM, cols] = ab[:, group_dim:].astype(BF16)

    @pl.when(j < ctx_tiles)
    def _():
        rows = pl.ds(pl.multiple_of(j * TM, TM), TM)
        y = _dot(pctx_ref[rows, :], ab_ctx[...]) * (1.0 / math.sqrt(n_ctx * group_dim))
        y_ref[rows, :] = y.astype(BF16)
        xs_ref[...] = ctx_ref[rows, :]

    @pl.when(j >= ctx_tiles)
    def _():
        xs_ref[...] = lat_ref[pl.ds(pl.multiple_of((j - ctx_tiles) * TM, TM), TM), :]

    first_p_step = pl.num_programs(1) - p_tiles

    @pl.when(j >= first_p_step)
    def _():
        tile = j - first_p_step
        scale = 1.0 / math.sqrt(seq * group_dim)
        p_part = _dot(cos_ref[...], ab_lat[0:seq, :])
        q_part = _dot(sin_ref[...], ab_lat[seq:2 * seq, :])
        y_ref[pl.ds(pl.multiple_of(n_ctx + tile * pt, pt), pt), :] = ((p_part - q_part)[0:pt] * scale).astype(BF16)
        mirrored = _dot(rev_ref[...], ((p_part + q_part) * scale).astype(BF16))
        y_ref[pl.ds(pl.multiple_of(n_ctx + seq - (tile + 1) * pt, pt), pt), :] = mirrored.astype(BF16)


def _fnet_mix(tok, ctx, x, mod, layer, norm_gain, group_dim):
    b, t, d = tok.batch, tok.t, x.shape[-1]
    chan, pos_ctx, cos_lat, sin_lat, rev = _dft_tables(tok.n_ctx, tok.seq, group_dim)
    ct, p_tiles = tok.ctx_tiles, cos_lat.shape[0]
    assert p_tiles < tok.tiles

    def mod_spec(chunk, is_ctx):
        return pl.BlockSpec((None, 1, d), lambda i, j: ((layer * COND_ROWS + (0 if is_ctx else 1 + i)) * 6 + chunk, 0, 0))

    table_spec = pl.BlockSpec((None,) + cos_lat.shape[1:],
                              lambda i, j: (jnp.maximum(j - (tok.tiles - p_tiles), 0), 0, 0))
    y, xs = pl.pallas_call(
        functools.partial(_fnet_kernel, group_dim=group_dim),
        grid=(b, tok.tiles),
        in_specs=[
            pl.BlockSpec((None, tok.n_ctx, d), lambda i, j: (i, 0, 0)),
            pl.BlockSpec((None, tok.seq, d), lambda i, j: (i, 0, 0)),
            pl.BlockSpec((1, d), lambda i, j: (0, 0)),
            mod_spec(0, True), mod_spec(1, True), mod_spec(0, False), mod_spec(1, False),
            _resident(chan.shape), _resident(pos_ctx.shape), table_spec, table_spec, _resident(rev.shape),
        ],
        out_specs=[pl.BlockSpec((None, t, d), lambda i, j: (i, 0, 0)),
                   pl.BlockSpec((TM, d), lambda i, j: (i * tok.tiles + j, 0))],
        out_shape=[jax.ShapeDtypeStruct((b, t, d), BF16), jax.ShapeDtypeStruct((b * t, d), F32)],
        scratch_shapes=[pltpu.VMEM((2 * tok.n_ctx, d), BF16), pltpu.VMEM((2 * tok.seq, d), BF16)],
        compiler_params=_cparams("parallel", "arbitrary"),
        name="fnet_mix",
    )(ctx, x, norm_gain.reshape(1, d), mod, mod, mod, mod, chan, pos_ctx, cos_lat, sin_lat, rev)
    return y.reshape(b * t, d), xs


def _rope_tables(tok, head_dim):
    rows = tok.seq // GRID_W
    row = jnp.repeat(jnp.arange(rows, dtype=F32), GRID_W)
    col = jnp.tile(jnp.arange(GRID_W, dtype=F32), rows)
    n_freq = head_dim // 4
    inv_freq = ROPE_THETA ** (-jnp.arange(n_freq, dtype=F32) / n_freq)
    ang = jnp.concatenate([row[:, None] * inv_freq, col[:, None] * inv_freq], axis=-1)
    cos = jnp.concatenate([jnp.ones((tok.n_ctx, head_dim // 2), F32), jnp.cos(ang)], axis=0)
    sin = jnp.concatenate([jnp.zeros((tok.n_ctx, head_dim // 2), F32), jnp.sin(ang)], axis=0)
    reps = LANES // head_dim
    cos = jnp.tile(jnp.concatenate([cos, cos], axis=-1), (1, reps))
    sin = jnp.tile(jnp.concatenate([-sin, sin], axis=-1), (1, reps))
    return cos, sin


def _group_mean_matrix(group, width):
    g = np.arange(width) // group
    return jnp.asarray((g[:, None] == g[None, :]).astype(np.float32) / group).astype(BF16)


def _rope(yn, cos, sin, head_dim):
    half = head_dim // 2
    if head_dim == LANES:
        partner = pltpu.roll(yn, half, 1)
    else:
        lane = lax.broadcasted_iota(jnp.int32, yn.shape, 1)
        partner = jnp.where(lane % head_dim < half, pltpu.roll(yn, LANES - half, 1), pltpu.roll(yn, half, 1))
    return yn * cos + partner * sin


def _qkv_kernel(x_ref, ng_ref, shc, shl, scc, scl, w_ref, qg_ref, kg_ref, cos_ref, sin_ref, gm_ref,
                q_ref, k_ref, vt_ref, *, head_dim, tiles_per_batch, n_ctx):
    chunks, pick = _ctx_chunks(x_ref.shape[0], tiles_per_batch, n_ctx)
    h = jnp.concatenate([_norm_mod(x_ref[rs, :], ng_ref[...], pick(ic, shc, shl), pick(ic, scc, scl)).astype(BF16)
                         for rs, ic in chunks], axis=0)
    cos, sin, gm = cos_ref[...], sin_ref[...], gm_ref[...]
    nq, nk = q_ref.shape[-1], k_ref.shape[-1]
    for o_ref, g_ref, col0, n in ((q_ref, qg_ref, 0, nq), (k_ref, kg_ref, nq, nk)):
        y = _dot(h, w_ref[:, col0:col0 + n])
        for c0 in range(0, n, MXU_DIM):
            yb = y[:, c0:c0 + MXU_DIM]
            yn = yb * lax.rsqrt(_dot((yb * yb).astype(BF16), gm) + EPS) * g_ref[...]
            for l0 in range(0, MXU_DIM, LANES):
                o_ref[:, c0 + l0:c0 + l0 + LANES] = _rope(yn[:, l0:l0 + LANES], cos, sin, head_dim).astype(BF16)
    vt_ref[...] = _dot(h, w_ref[:, nq + nk:]).astype(BF16).T


def _qkv_project(tok, x_all, mod, layer, norm_gain, w_in, q_gain, k_gain, head_dim, nq, nk):
    n_rows, d = x_all.shape
    nv = w_in.shape[1] - nq - nk
    assert nq % MXU_DIM == 0 and nk % MXU_DIM == 0
    cos, sin = _rope_tables(tok, head_dim)
    tm = _row_tile(tok.t, PROJ_MAX_ROWS)
    per_b = tok.t // tm
    row_spec = lambda n: pl.BlockSpec((tm, n), lambda i: (i, 0))
    tab_spec = pl.BlockSpec((tm, LANES), lambda i: (i % per_b, 0))
    gain2 = lambda g: jnp.tile(g, (1, MXU_DIM // LANES))
    return pl.pallas_call(
        functools.partial(_qkv_kernel, head_dim=head_dim, tiles_per_batch=per_b, n_ctx=tok.n_ctx),
        grid=(n_rows // tm,),
        in_specs=[row_spec(d), _vec_spec(d)] + tok.mod_specs(d, layer, 0, per_b) + tok.mod_specs(d, layer, 1, per_b)
                 + [_resident(w_in.shape), _vec_spec(MXU_DIM), _vec_spec(MXU_DIM), tab_spec, tab_spec,
                    _resident((MXU_DIM, MXU_DIM))],
        out_specs=[row_spec(nq), row_spec(nk), pl.BlockSpec((None, nv, tm), lambda i: (i // per_b, 0, i % per_b))],
        out_shape=[jax.ShapeDtypeStruct((n_rows, nq), BF16), jax.ShapeDtypeStruct((n_rows, nk), BF16),
                   jax.ShapeDtypeStruct((tok.batch, nv, tok.t), BF16)],
        compiler_params=_cparams("parallel"),
        name=f"qkv_project_hd{head_dim}",
    )(x_all, norm_gain.reshape(1, d), mod, mod, mod, mod, w_in, gain2(q_gain), gain2(k_gain),
      cos, sin, _group_mean_matrix(head_dim, MXU_DIM))


def _softmax_parts(q, k):
    s = _dot_nt(q, k)
    p = jnp.exp2(s - jnp.max(s, axis=-1, keepdims=True))
    return p, jnp.sum(p, axis=-1, keepdims=True)


def _softmax_parts_bounded(k, q, vt):
    p = jnp.exp2(_dot_nt(k, q))
    l = jnp.sum(p, axis=0, keepdims=True)
    return _dot(vt, p.astype(BF16)) * (1.0 / l)


def _logit_bound(qg_ref, kg_ref, head_dim):
    return BF16_SLACK * head_dim * jnp.max(jnp.abs(qg_ref[...])) * jnp.max(jnp.abs(kg_ref[...]))


def _diff_attn_kernel(lp_ref, sg_ref, qg_ref, kg_ref, q_ref, k_ref, vt, o_ref, *, lam_init, n_ctx):
    lp = lp_ref[...]
    lam = (jnp.exp(jnp.sum(lp[0:1] * lp[1:2], axis=-1, keepdims=True))
           - jnp.exp(jnp.sum(lp[2:3] * lp[3:4], axis=-1, keepdims=True)) + lam_init)
    t = k_ref.shape[0]
    lane = lax.broadcasted_iota(jnp.int32, (TM, LANES), 1)
    bounded = _logit_bound(qg_ref, kg_ref, DIFF_HEAD_DIM) <= EXP2_SAFE_LOGIT

    def split_q(rows):
        q = q_ref[rows, :]
        zero = jnp.zeros_like(q)
        return jnp.where(lane < DIFF_HEAD_DIM, q, zero), jnp.where(lane >= DIFF_HEAD_DIM, q, zero)

    def tile_bounded(rows, nk):
        o12 = _softmax_parts_bounded(k_ref[0:nk, :], jnp.concatenate(split_q(rows), axis=0), vt[:, 0:nk])
        ot = o12[:, :TM] - lam * o12[:, TM:]
        ms = jnp.mean(ot * ot, axis=0, keepdims=True)
        o_ref[rows, :] = ((ot * lax.rsqrt(ms + EPS)).T * sg_ref[...]).astype(BF16)

    def tile_exact(rows, nk):
        k = k_ref[0:nk, :]
        q1, q2 = split_q(rows)
        p1, l1 = _softmax_parts(q1, k)
        p2, l2 = _softmax_parts(q2, k)
        w = p1 * (1.0 / l1) - p2 * (lam / l2)
        o = _dot(w.astype(BF16), vt[:, 0:nk].T)
        ms = jnp.mean(o * o, axis=-1, keepdims=True)
        o_ref[rows, :] = (o * lax.rsqrt(ms + EPS) * sg_ref[...]).astype(BF16)

    @pl.when(bounded)
    def _():
        _for_query_tiles(tile_bounded, n_ctx, t, 0, DIFF_ATTN_UNROLL)

    @pl.when(jnp.logical_not(bounded))
    def _():
        _for_query_tiles(tile_exact, n_ctx, t, 0, 1)


def _for_query_tiles(tile, n_ctx, t, first_row, unroll):
    for r0 in range(first_row, n_ctx, TM):
        tile(pl.ds(r0, TM), n_ctx)
    lat0 = max(first_row, n_ctx)
    n_lat = (t - lat0) // TM
    unroll = unroll if n_lat % unroll == 0 else 1

    def group(i, _):
        for u in range(unroll):
            tile(pl.ds(pl.multiple_of(lat0 + (i * unroll + u) * TM, TM), TM), t)
        return 0

    lax.fori_loop(0, n_lat // unroll, group, 0)


def _diff_attention(tok, q, k, vt, q_gain, k_gain, lam_par, subln_gain, lam_init, cast_stacks, cast_layer):
    n_rows, d = q.shape
    heads = d // LANES
    t = tok.t
    grid = (tok.batch, heads)
    side = _SideCasts(cast_stacks, cast_layer, grid)
    as3 = lambda a: a.reshape(tok.batch, t, d)
    spec = pl.BlockSpec((None, t, LANES), lambda b, h: (b, 0, h))
    vt_spec = pl.BlockSpec((None, LANES, t), lambda b, h: (b, h, 0))
    vec = pl.BlockSpec((1, LANES), lambda b, h: (0, 0))
    in_specs = [pl.BlockSpec(lam_par.shape, lambda b, h: (0, 0)), vec, vec, vec, spec, spec, vt_spec]
    out, *casts = pl.pallas_call(
        side.wrap(functools.partial(_diff_attn_kernel, lam_init=lam_init, n_ctx=tok.n_ctx), len(in_specs), 1),
        grid=grid,
        in_specs=in_specs + side.in_specs,
        out_specs=[spec] + side.out_specs,
        out_shape=[jax.ShapeDtypeStruct((tok.batch, t, d), BF16)] + side.out_shapes,
        compiler_params=_cparams("parallel", "arbitrary"),
        name="diff_attention",
    )(lam_par, (subln_gain * (1.0 - lam_init)).reshape(1, LANES), q_gain, k_gain, as3(q), as3(k), vt, *side.stacks)
    return out.reshape(n_rows, d), casts


def _gqa_attn_kernel(qg_ref, kg_ref, q_ref, k_ref, vt, o_ref, *, n_ctx, first_row):
    t = k_ref.shape[0]
    bounded = _logit_bound(qg_ref, kg_ref, GQA_HEAD_DIM) <= EXP2_SAFE_LOGIT

    def out_rows(rows):
        if first_row == 0:
            return rows
        start = rows.start - first_row
        return pl.ds(start if isinstance(start, int) else pl.multiple_of(start, TM), TM)

    def tile_bounded(rows, nk):
        k, vt_k = k_ref[0:nk, :], vt[:, 0:nk]
        for g in range(0, GQA_GROUP, 2):
            q2h = jnp.concatenate([q_ref[rows, g * LANES:(g + 1) * LANES],
                                   q_ref[rows, (g + 1) * LANES:(g + 2) * LANES]], axis=0)
            o2h = _softmax_parts_bounded(k, q2h, vt_k)
            o_ref[out_rows(rows), g * LANES:(g + 1) * LANES] = o2h[:, :TM].T.astype(BF16)
            o_ref[out_rows(rows), (g + 1) * LANES:(g + 2) * LANES] = o2h[:, TM:].T.astype(BF16)

    def tile_exact(rows, nk):
        k, v = k_ref[0:nk, :], vt[:, 0:nk].T
        for g in range(GQA_GROUP):
            cols = slice(g * LANES, (g + 1) * LANES)
            p, l = _softmax_parts(q_ref[rows, cols], k)
            o_ref[out_rows(rows), cols] = (_dot(p.astype(BF16), v) * (1.0 / l)).astype(BF16)

    @pl.when(bounded)
    def _():
        _for_query_tiles(tile_bounded, n_ctx, t, first_row, GQA_ATTN_UNROLL)

    @pl.when(jnp.logical_not(bounded))
    def _():
        _for_query_tiles(tile_exact, n_ctx, t, first_row, 1)


def _gqa_attention(tok, q, k, vt, q_gain, k_gain, need_ctx, cast_stacks, cast_layer):
    d = q.shape[1]
    kv_heads = k.shape[1] // LANES
    t = tok.t
    grid = (tok.batch, kv_heads)
    side = _SideCasts(cast_stacks, cast_layer, grid)
    first_row = 0 if need_ctx else tok.n_ctx
    as3 = lambda a: a.reshape(tok.batch, t, a.shape[1])
    k_spec = pl.BlockSpec((None, t, LANES), lambda b, h: (b, 0, h))
    vt_spec = pl.BlockSpec((None, LANES, t), lambda b, h: (b, h, 0))
    q_spec = pl.BlockSpec((None, t, GQA_GROUP * LANES), lambda b, h: (b, 0, h))
    o_spec = pl.BlockSpec((None, t - first_row, GQA_GROUP * LANES), lambda b, h: (b, 0, h))
    vec = pl.BlockSpec((1, LANES), lambda b, h: (0, 0))
    in_specs = [vec, vec, q_spec, k_spec, vt_spec]
    out, *casts = pl.pallas_call(
        side.wrap(functools.partial(_gqa_attn_kernel, n_ctx=tok.n_ctx, first_row=first_row), len(in_specs), 1),
        grid=grid,
        in_specs=in_specs + side.in_specs,
        out_specs=[o_spec] + side.out_specs,
        out_shape=[jax.ShapeDtypeStruct((tok.batch, t - first_row, d), BF16)] + side.out_shapes,
        compiler_params=_cparams("parallel", "arbitrary"),
        name="gqa_attention",
    )(q_gain, k_gain, as3(q), as3(k), vt, *side.stacks)
    return out.reshape(tok.batch * (t - first_row), d), casts


def _hgrn_proj_kernel(x_ref, ng_ref, shc, shl, scc, scl, w_ref, lb_ref, q_ref, lff_ref, lfb_ref, v_ref, g_ref,
                      *, layer_idx, depth, tiles_per_batch, n_ctx):
    d = x_ref.shape[-1]
    chunks, pick = _ctx_chunks(x_ref.shape[0], tiles_per_batch, n_ctx)
    lbs = []
    for direction in range(2):
        rows = [lb_ref[direction * depth + i:direction * depth + i + 1, :] for i in range(depth)]
        m = functools.reduce(jnp.maximum, rows)
        e = [jnp.exp(r - m) for r in rows]
        lbs.append(sum(e[1:layer_idx + 1]) / sum(e) if layer_idx > 0 else jnp.zeros_like(m))

    def normed(rs, ic):
        return _norm_mod(x_ref[rs, :], ng_ref[...], pick(ic, shc, shl), pick(ic, scc, scl)).astype(BF16)

    def project(rs, h):
        q_ref[rs, :] = _silu(_dot(h, w_ref[:, 0:d]))
        for direction, o_ref in enumerate((lff_ref, lfb_ref)):
            z = _dot(h, w_ref[:, (1 + direction) * d:(2 + direction) * d])
            lb = lbs[direction]
            o_ref[rs, :] = jnp.log2(lb + (1.0 - lb) * jax.nn.sigmoid(z))
        v_ref[rs, :] = _dot(h, w_ref[:, 3 * d:4 * d]).astype(BF16)
        g_ref[rs, :] = _dot(h, w_ref[:, 4 * d:5 * d])

    h = normed(*chunks[0])
    for k, (rs, _) in enumerate(chunks):
        h_next = normed(*chunks[k + 1]) if k + 1 < len(chunks) else None
        project(rs, h)
        h = h_next


def _hgrn_project(tok, x_all, mod, layer, norm_gain, w_in, lower_bound):
    n_rows, d = x_all.shape
    depth = lower_bound.shape[1]
    tm = _row_tile(tok.t, HGRN_PROJ_MAX_ROWS)
    per_b = tok.t // tm
    row_spec = pl.BlockSpec((tm, d), lambda i: (i, 0))
    return pl.pallas_call(
        functools.partial(_hgrn_proj_kernel, layer_idx=layer, depth=depth, tiles_per_batch=per_b, n_ctx=tok.n_ctx),
        grid=(n_rows // tm,),
        in_specs=[row_spec, _vec_spec(d)] + tok.mod_specs(d, layer, 0, per_b) + tok.mod_specs(d, layer, 1, per_b)
                 + [_resident(w_in.shape), pl.BlockSpec((2 * depth, d), lambda i: (0, 0))],
        out_specs=[row_spec] * 5,
        out_shape=[jax.ShapeDtypeStruct((n_rows, d), dt) for dt in (F32, F32, F32, BF16, F32)],
        compiler_params=_cparams("parallel"),
        name="hgrn_project",
    )(x_all, norm_gain.reshape(1, d), mod, mod, mod, mod, w_in, lower_bound.reshape(2 * depth, d))


def _split2(x):
    hi = x.astype(BF16)
    return hi, (x - hi.astype(F32)).astype(BF16)


def _hgrn_scan_kernel(q_ref, lff_ref, lfb_ref, v_ref, g_ref, ng_ref, o_ref,
                      acc, qs, xs, st, ds, cum, kk, *, n_ctx):
    t, dk = q_ref.shape
    c, hh = HGRN_CHUNK, HGRN_HALF
    n_chunks, ctx_chunks = t // c, n_ctx // c
    cpt = TM // c
    row = lax.broadcasted_iota(jnp.int32, (TM, TM), 0)
    col = lax.broadcasted_iota(jnp.int32, (TM, TM), 1)
    same_chunk = (row // c) == (col // c)
    same_half = (row // hh) == (col // hh)
    cross_half = same_chunk & jnp.logical_not(same_half)
    cum_mat = jnp.where(same_chunk & (col <= row), 1.0, 0.0).astype(BF16)
    row_c = lax.broadcasted_iota(jnp.int32, (c, c), 0)
    col_c = lax.broadcasted_iota(jnp.int32, (c, c), 1)

    as4 = lambda a: a.reshape(cpt, c, dk)
    as8 = lambda a: a.reshape(2 * cpt, hh, dk)
    flat = lambda a: a.reshape(TM, dk)
    half_idx = lax.broadcasted_iota(jnp.int32, (2 * cpt, 1, 1), 0)
    per_half = lambda a: jnp.concatenate([a[i // 2:i // 2 + 1] for i in range(2 * cpt)], axis=0)
    n_tiles = t // TM
    unroll = next(u for u in (HGRN_TILE_UNROLL, 3, 2, 1) if n_tiles % u == 0)

    def cumulative(ti):
        rows = pl.ds(pl.multiple_of(ti * TM, TM), TM)
        lf_f, lf_b = lff_ref[rows, :], lfb_ref[rows, :]
        hi, lo = _split2(jnp.concatenate([lf_f, lf_b], axis=1))
        pre = _dot(cum_mat, hi) + _dot(cum_mat, lo)
        pre_b = as4(pre[:, dk:])
        bc_f = as4(pre[:, :dk])
        bc_b = pre_b[:, c - 1:c, :] - pre_b + as4(lf_b)
        return rows, (as4(lf_f), as4(lf_b)), (bc_f, bc_b)

    def min_half_total(bcs):
        bc_f, bc_b = bcs
        top_f, bot_b = bc_f[:, hh - 1:hh, :], bc_b[:, hh:hh + 1, :]
        halves = jnp.minimum(jnp.minimum(top_f, bc_f[:, c - 1:c, :] - top_f), jnp.minimum(bot_b, bc_b[:, 0:1, :] - bot_b))
        return jnp.min(halves, axis=0)

    def tile_operands(cumulated):
        rows, lfs, bcs = cumulated
        q8, v = as8(q_ref[rows, :]), v_ref[rows, :]
        q_d2, k_d2, q_s, k_bar, decay, q_o, k_o = [], [], [], [], [], [], []
        for lf4, bc4, reverse in zip(lfs, bcs, (False, True)):
            edge, last = (hh, 0) if reverse else (hh - 1, c - 1)
            second = (half_idx % 2 == 0) if reverse else (half_idx % 2 == 1)
            b_last, b_edge = bc4[:, last:last + 1, :], bc4[:, edge:edge + 1, :]
            b_last8, b_edge8 = per_half(b_last), per_half(b_edge)
            bc8, k8 = as8(bc4), 1.0 - jnp.exp2(as8(lf4))
            ref = bc8[:, hh // 2:hh // 2 + 1, :]
            rel = bc8 - ref
            q_d, k_d = q8 * jnp.exp2(rel), k8 * jnp.exp2(-rel)
            q_d2.append(flat(q_d).astype(BF16))
            k_d2.append(flat(k_d).astype(BF16))
            q_s.append(flat(q_d * jnp.exp2(ref)))
            k_bar.append(flat(k_d * jnp.exp2(b_last8 - ref)))
            decay.append(jnp.exp2(b_last))
            q_o.append(flat(q_d * jnp.where(second, jnp.exp2(ref - b_edge8), 0.0)))
            k_o.append(flat(k_d * jnp.where(second, 0.0, jnp.exp2(b_edge8 - ref))))
        stack = lambda parts: jnp.concatenate(parts, axis=1).astype(BF16)
        return rows, v, q_d2, k_d2, stack(q_o), stack(k_o), stack(q_s), stack(k_bar), jnp.concatenate(decay, axis=2)

    def tile_matmuls(ti, operands):
        rows, v, q_d2, k_d2, q_o2, k_o2, q_s2, k_bar2, decay2 = operands
        a_same = (jnp.where(col <= row, _dot_nt(q_d2[0], k_d2[0]), 0.0)
                  + jnp.where(col >= row, _dot_nt(q_d2[1], k_d2[1]), 0.0))
        a = jnp.where(same_half, a_same, jnp.where(cross_half, _dot_nt(q_o2, k_o2), 0.0))
        acc[rows, :] = _dot(a.astype(BF16), v)
        store_state_inputs(ti, rows, v, q_s2, k_bar2, decay2)

    def store_state_inputs(ti, rows, v, q_s2, k_bar2, decay2):
        qs[rows, :] = q_s2
        for cc in range(cpt):
            ci = ti * cpt + cc
            xs[ci] = _dot_tn(v[cc * c:(cc + 1) * c, :], k_bar2[cc * c:(cc + 1) * c, :])
            ds[ci] = jnp.broadcast_to(decay2[cc], ds.shape[1:])

    def tile_exact(ti, _):
        rows, lfs, bcs = cumulative(ti)
        q4, v = as4(q_ref[rows, :]), v_ref[rows, :]
        q_s, k_bar, decay = [], [], []
        for lf4, bc4, reverse in zip(lfs, bcs, (False, True)):
            b_last = bc4[:, 0:1, :] if reverse else bc4[:, c - 1:c, :]
            q_s.append(flat(q4 * jnp.exp2(bc4)))
            k_bar.append(flat((1.0 - jnp.exp2(lf4)) * jnp.exp2(b_last - bc4)))
            decay.append(jnp.exp2(b_last))
        stack = lambda parts: jnp.concatenate(parts, axis=1).astype(BF16)
        store_state_inputs(ti, rows, v, stack(q_s), stack(k_bar), jnp.concatenate(decay, axis=2))
        for cc in range(cpt):
            a = jnp.zeros((c, c), F32)
            q_c = q4[cc]
            for lf4, bc4, reverse in zip(lfs, bcs, (False, True)):
                bc_c = bc4[cc]
                cum[...] = bc_c
                kk[...] = 1.0 - jnp.exp2(lf4[cc])

                def col_step(s, a_):
                    d_s = jnp.exp2(jnp.minimum(bc_c - cum[pl.ds(s, 1), :], 0.0))
                    w = jnp.sum(q_c * kk[pl.ds(s, 1), :] * d_s, axis=-1, keepdims=True)
                    return jnp.where(col_c == s, w, a_)
                a_dir = lax.fori_loop(0, c, col_step, jnp.zeros((c, c), F32))
                a = a + jnp.where((col_c >= row_c) if reverse else (col_c <= row_c), a_dir, 0.0)
            crow = pl.ds(pl.multiple_of(ti * TM + cc * c, c), c)
            acc[crow, :] = _dot(a.astype(BF16), v[cc * c:(cc + 1) * c, :])
        return 0

    def tile_group(i, lowest):
        tiles = [i * unroll + u for u in range(unroll)]
        cums = {0: cumulative(tiles[0])}
        if unroll > 1:
            cums[1] = cumulative(tiles[1])
        ops = {0: tile_operands(cums[0])}
        for u in range(unroll):
            if u + 2 < unroll:
                cums[u + 2] = cumulative(tiles[u + 2])
            if u + 1 < unroll:
                ops[u + 1] = tile_operands(cums[u + 1])
            tile_matmuls(tiles[u], ops[u])
        return functools.reduce(jnp.minimum, [min_half_total(cm[2]) for cm in cums.values()], lowest)

    lowest = lax.fori_loop(0, n_tiles // unroll, tile_group, jnp.zeros((1, dk), F32))
    unsafe = jnp.logical_not(jnp.min(lowest) >= HGRN_SAFE_LOG2_DECAY)

    @pl.when(unsafe)
    def _():
        lax.fori_loop(0, n_tiles, tile_exact, 0)

    def state_step(i, carry):
        s_f, s_b = carry
        cf = i
        cb = jnp.where(i < ctx_chunks, ctx_chunks - 1 - i, n_chunks - 1 + ctx_chunks - i)
        st[cf, :, 0:dk] = s_f.astype(BF16)
        st[cb, :, dk:2 * dk] = s_b.astype(BF16)
        s_f = s_f * ds[cf, 0:1, 0:dk] + xs[cf, :, 0:dk]
        s_b = s_b * ds[cb, 0:1, dk:2 * dk] + xs[cb, :, dk:2 * dk]
        return s_f, s_b

    zero = jnp.zeros((dk, dk), F32)
    lax.fori_loop(0, n_chunks, state_step, (zero, zero))

    def finish_step(ti):
        rows = pl.ds(pl.multiple_of(ti * TM, TM), TM)
        inter = [_dot_nt(qs[pl.ds(pl.multiple_of(ti * TM + cc * c, c), c), :], st[ti * cpt + cc]) for cc in range(cpt)]
        o = acc[rows, :] + jnp.concatenate(inter, axis=0)
        ms = jnp.mean(o * o, axis=-1, keepdims=True)
        o_ref[rows, :] = (o * lax.rsqrt(ms + EPS) * ng_ref[...] * _silu(g_ref[rows, :])).astype(BF16)

    def finish_group(i, _):
        for u in range(unroll):
            finish_step(i * unroll + u)
        return 0

    lax.fori_loop(0, n_tiles // unroll, finish_group, 0)


def _hgrn_scan(tok, q, lff, lfb, v, g, norm_gain, cast_stacks, cast_layer):
    n_rows, d = q.shape
    heads = d // LANES
    t = tok.t
    n_chunks = t // HGRN_CHUNK
    grid = (tok.batch, heads)
    side = _SideCasts(cast_stacks, cast_layer, grid)
    spec = pl.BlockSpec((None, t, LANES), lambda b, h: (b, 0, h))
    as3 = lambda a: a.reshape(tok.batch, t, d)
    in_specs = [spec] * 5 + [pl.BlockSpec((1, LANES), lambda b, h: (0, 0))]
    out, *casts = pl.pallas_call(
        side.wrap(functools.partial(_hgrn_scan_kernel, n_ctx=tok.n_ctx), len(in_specs), 1),
        grid=grid,
        in_specs=in_specs + side.in_specs,
        out_specs=[spec] + side.out_specs,
        out_shape=[jax.ShapeDtypeStruct((tok.batch, t, d), BF16)] + side.out_shapes,
        scratch_shapes=[pltpu.VMEM((t, LANES), F32), pltpu.VMEM((t, 2 * LANES), BF16),
                        pltpu.VMEM((n_chunks, LANES, 2 * LANES), F32), pltpu.VMEM((n_chunks, LANES, 2 * LANES), BF16),
                        pltpu.VMEM((n_chunks, SUBLANES, 2 * LANES), F32),
                        pltpu.VMEM((HGRN_CHUNK, LANES), F32), pltpu.VMEM((HGRN_CHUNK, LANES), F32)],
        compiler_params=_cparams("parallel", "arbitrary"),
        name="hgrn_scan",
    )(as3(q), as3(lff), as3(lfb), as3(v), as3(g), norm_gain.reshape(1, LANES), *side.stacks)
    return out.reshape(n_rows, d), casts


def _ffn_kernel(*refs, n_x, tiles_per_batch, n_ctx):
    x_refs = refs[:n_x]
    o_ref, wo_ref, bo_ref, ng_ref, g1c, g1l, shc, shl, scc, scl, g2c, g2l, wi_ref, wf_ref, y_ref = refs[n_x:]
    chunks, pick = _ctx_chunks(y_ref.shape[0], tiles_per_batch, n_ctx)
    x_chunk = (lambda k, rs: x_refs[0][rs, :]) if n_x == 1 else (lambda k, rs: x_refs[k][...])
    x1 = [x_chunk(k, rs) + pick(ic, g1c, g1l) * (_dot(o_ref[rs, :], wo_ref[...]) + bo_ref[...])
          for k, (rs, ic) in enumerate(chunks)]
    h = jnp.concatenate([_norm_mod(x1_c, ng_ref[...], pick(ic, shc, shl), pick(ic, scc, scl)).astype(BF16)
                         for x1_c, (_, ic) in zip(x1, chunks)], axis=0)
    d_ff = wf_ref.shape[0]
    act = []
    for c0 in range(0, d_ff, FF_CHUNK):
        gate = _dot(h, wi_ref[:, c0:c0 + FF_CHUNK])
        up = _dot(h, wi_ref[:, d_ff + c0:d_ff + c0 + FF_CHUNK])
        act.append((_silu(gate) * up).astype(BF16))
    ff = _dot(jnp.concatenate(act, axis=1), wf_ref[...])
    for x1_c, (rs, ic) in zip(x1, chunks):
        y_ref[rs, :] = x1_c + pick(ic, g2c, g2l) * ff[rs, :]


def _out_proj_ffn(tok, x_all, o_all, mod, layer, w_o, b_o, norm_gain, w_in, w_out, lat_only, cast_stacks, cast_layer):
    n_rows, d = x_all.shape
    d_ff = w_out.shape[0]
    assert d_ff % FF_CHUNK == 0
    rows_b = tok.seq if lat_only else tok.t
    o_is_stream = o_all.shape[0] == n_rows
    if lat_only and not o_is_stream:
        tm = _row_tile(rows_b, FFN_LAT_MAX_ROWS)
        per_b = rows_b // tm
        x_specs = [pl.BlockSpec((TM, d), lambda i, s=s: ((i // per_b) * tok.tiles + tok.ctx_tiles
                                                         + (i % per_b) * (tm // TM) + s, 0))
                   for s in range(tm // TM)]
    else:
        tm = _row_tile(math.gcd(rows_b, tok.n_ctx) if lat_only else rows_b, FFN_MAX_ROWS)
        per_b = rows_b // tm
        first = tok.n_ctx // tm if lat_only else 0
        x_specs = [pl.BlockSpec((tm, d), lambda i: ((i // per_b) * (tok.t // tm) + first + i % per_b, 0))]
    out_row = pl.BlockSpec((tm, d), lambda i: (i, 0))
    o_row = x_specs[0] if o_is_stream else out_row
    mods = [s for chunk in (2, 3, 4, 5) for s in tok.mod_specs(d, layer, chunk, per_b)]
    grid = (tok.batch * per_b,)
    side = _SideCasts(cast_stacks, cast_layer, grid)
    in_specs = (x_specs + [o_row, _resident(w_o.shape), _vec_spec(d), _vec_spec(d)] + mods
                + [_resident(w_in.shape), _resident(w_out.shape)])
    y, *casts = pl.pallas_call(
        side.wrap(functools.partial(_ffn_kernel, n_x=len(x_specs), tiles_per_batch=per_b,
                                    n_ctx=0 if lat_only else tok.n_ctx), len(in_specs), 1),
        grid=grid,
        in_specs=in_specs + side.in_specs,
        out_specs=[out_row] + side.out_specs,
        out_shape=[jax.ShapeDtypeStruct((tok.batch * rows_b, d), F32)] + side.out_shapes,
        compiler_params=_cparams("arbitrary"),
        name="out_proj_ffn",
    )(*([x_all] * len(x_specs)), o_all, w_o.astype(BF16), b_o.reshape(1, d), norm_gain.reshape(1, d), *([mod] * 8),
      w_in, w_out, *side.stacks)
    return y, casts


def kernel(x, c, ctx, c_ctx, w_ada, b_ada, norm_gain, ffn_w_in, ffn_w_out, fnet_w_out, fnet_b_out,
           diff_w_in, diff_q_gain, diff_k_gain, diff_lambda, diff_subln_gain, diff_w_out,
           hgrn_w_in, hgrn_lower_bound, hgrn_norm_gain, hgrn_w_out,
           gqa_w_in, gqa_q_gain, gqa_k_gain, gqa_w_out):
    batch, seq, d = x.shape
    n_ctx = ctx.shape[1]
    depth = w_ada.shape[0]
    assert batch + 1 <= COND_ROWS and d % LANES == 0
    tok = _Tokens(batch, n_ctx, seq)

    cond = jnp.zeros((COND_ROWS, d), F32).at[0].set(c_ctx).at[1:1 + batch].set(c)
    mod = _ada_modulation(cond, w_ada, b_ada)
    x_all = None
    zero_bias = jnp.zeros((d,), F32)
    ffn_stacks = [ffn_w_in, ffn_w_out]

    for i in range(depth):
        m, j = i % N_MIXERS, i // N_MIXERS
        need_ctx = i < depth - 1
        if m == 0:
            if x_all is not None:
                stream = x_all.reshape(batch, tok.t, d)
                ctx, x = stream[:, :n_ctx], stream[:, n_ctx:]
            o, x_all = _fnet_mix(tok, ctx, x, mod, i, norm_gain[i, 0], d // FNET_GROUPS)
            ffn_w = [w[i].astype(BF16) for w in ffn_stacks]
            w_o, b_o = fnet_w_out[j], fnet_b_out[j]
        elif m == 1:
            lam_init = 0.8 - 0.6 * math.exp(-0.3 * i)
            qg = diff_q_gain[j].reshape(1, LANES) * (DIFF_HEAD_DIM ** -0.5 * LOG2_E)
            kg = diff_k_gain[j].reshape(1, LANES)
            q, k, v = _qkv_project(tok, x_all, mod, i, norm_gain[i, 0], proj_w, qg, kg,
                                   DIFF_HEAD_DIM, d, d)
            o, ffn_w = _diff_attention(tok, q, k, v, qg, kg, diff_lambda[j], diff_subln_gain[j], lam_init,
                                       ffn_stacks, i)
            w_o, b_o = diff_w_out[j], zero_bias
        elif m == 2:
            q, lff, lfb, v, g = _hgrn_project(tok, x_all, mod, i, norm_gain[i, 0], proj_w, hgrn_lower_bound)
            o, ffn_w = _hgrn_scan(tok, q, lff, lfb, v, g, hgrn_norm_gain[j], ffn_stacks, i)
            w_o, b_o = hgrn_w_out[j], zero_bias
        else:
            kv = (gqa_w_in.shape[-1] - d) // 2
            qg = gqa_q_gain[j].reshape(1, LANES) * (GQA_HEAD_DIM ** -0.5 * LOG2_E)
            kg = gqa_k_gain[j].reshape(1, LANES)
            q, k, v = _qkv_project(tok, x_all, mod, i, norm_gain[i, 0], proj_w, qg, kg, GQA_HEAD_DIM, d, kv)
            o, ffn_w = _gqa_attention(tok, q, k, v, qg, kg, need_ctx, ffn_stacks, i)
            w_o, b_o = gqa_w_out[j], zero_bias
        next_proj = {1: diff_w_in, 2: hgrn_w_in, 3: gqa_w_in}.get((i + 1) % N_MIXERS) if i + 1 < depth else None
        x_all, proj_cast = _out_proj_ffn(tok, x_all, o, mod, i, w_o, b_o, norm_gain[i, 1], ffn_w[0], ffn_w[1],
                                         not need_ctx, [] if next_proj is None else [next_proj], (i + 1) // N_MIXERS)
        proj_w = proj_cast[0] if proj_cast else None
    return x_all.reshape(batch, seq, d)
```
